```python
import math
import jax, jax.numpy as jnp
from jax import lax
import numpy as np

D_MODEL = 2048
BATCH = 8
SEQ = 8192
DEPTH = 4

CHUNK = 64
N_A_LAYERS = DEPTH // 2
N_B_LAYERS = DEPTH - N_A_LAYERS
S5_GROUP = 16
S5_GROUPS = D_MODEL // S5_GROUP
S5_STATE = 64
DT_MIN = 1e-3
DT_MAX = 1e-1
FOX_HEAD_DIM = 128
FOX_HEADS = D_MODEL // FOX_HEAD_DIM
Q_BLOCK = 128
FGATE_BIAS_LO = 1.0
FGATE_BIAS_HI = 6.0
D_FF = 5632
CONV_WIDTH = 3
NORM_EPS = 1e-6

kernel_name = "s5_fox_yoco_convglu_trunk"


def rms_norm(x, g):
    xf = x.astype(jnp.float32)
    y = xf * lax.rsqrt(jnp.mean(xf * xf, axis=-1, keepdims=True) + NORM_EPS)
    return (y * g.astype(jnp.float32)).astype(x.dtype)


def s5_mixer(h, w_in, lam_re, lam_im, log_step, b_re, b_im, c_re, c_im, d, w_glu):
    f32 = jnp.float32
    bsz, seq, _ = h.shape
    u = (h @ w_in).astype(f32).reshape(bsz, seq, S5_GROUPS, S5_GROUP)
    lam = lax.complex(lam_re.astype(f32), lam_im.astype(f32))
    dt = jnp.exp(log_step.astype(f32))[:, None]
    lam_bar = jnp.exp(lam * dt)
    b = lax.complex(b_re.astype(f32), b_im.astype(f32))
    b_bar = ((lam_bar - 1.0) / lam)[..., None] * b
    bu = jnp.einsum('blgc,gpc->blgp', u.astype(jnp.complex64), b_bar)
    a = jnp.broadcast_to(lam_bar, bu.shape)

    def combine(e1, e2):
        a1, s1 = e1
        a2, s2 = e2
        return a1 * a2, a2 * s1 + s2

    _, states = lax.associative_scan(combine, (a, bu), axis=1)
    c = lax.complex(c_re.astype(f32), c_im.astype(f32))
    y = jnp.real(jnp.einsum('blgp,gcp->blgc', states, c)) + d.astype(f32) * u
    y = jax.nn.gelu(y.reshape(bsz, seq, D_MODEL)).astype(h.dtype)
    val, gate = jnp.split(y @ w_glu, 2, axis=-1)
    return val * jax.nn.sigmoid(gate)


def fox_attention(h, w_q, w_o, k, v, logf_cum):
    bsz, seq, _ = h.shape
    n_blk = seq // Q_BLOCK
    q = (h @ w_q).reshape(bsz, n_blk, Q_BLOCK, FOX_HEADS, FOX_HEAD_DIM).transpose(1, 0, 2, 3, 4)
    cq = logf_cum.reshape(bsz, FOX_HEADS, n_blk, Q_BLOCK).transpose(2, 0, 1, 3)
    k_pos = jnp.arange(seq)
    scale = FOX_HEAD_DIM ** -0.5

    def block(args):
        qb, cqb, blk = args
        q_pos = blk * Q_BLOCK + jnp.arange(Q_BLOCK)
        s = jnp.einsum('bqhd,bkhd->bhqk', qb, k).astype(jnp.float32) * scale
        s = s + (cqb[..., :, None] - logf_cum[:, :, None, :])
        s = jnp.where(k_pos[None, :] <= q_pos[:, None], s, -jnp.inf)
        p = jax.nn.softmax(s, axis=-1)
        return jnp.einsum('bhqk,bkhd->bqhd', p.astype(v.dtype), v)

    o = lax.map(block, (q, cq, jnp.arange(n_blk)))
    o = o.transpose(1, 0, 2, 3, 4).reshape(bsz, seq, D_MODEL)
    return o @ w_o


def conv_glu_ffn(h, w_up, conv_w, conv_b, w_down):
    gate, val = jnp.split(h @ w_up, 2, axis=-1)
    gate = lax.conv_general_dilated(
        gate, conv_w[:, None, :], window_strides=(1,),
        padding=[(CONV_WIDTH - 1, 0)],
        dimension_numbers=('NWC', 'WIO', 'NWC'),
        feature_group_count=D_FF) + conv_b
    return (jax.nn.gelu(gate) * val) @ w_down


def _fwd_setup_inputs(seed: int = 0) -> dict:
    key = jax.random.key(seed)
    ks = jax.random.split(key, 32)
    f32 = jnp.float32

    def nrm(k, shape, scale):
        return jax.random.normal(k, shape, f32) * scale

    na, nb, G, P, C, H = N_A_LAYERS, N_B_LAYERS, S5_GROUPS, S5_STATE, S5_GROUP, FOX_HEADS
    n_idx = jnp.arange(P, dtype=f32)
    return {
        "x": nrm(ks[0], (BATCH, SEQ, D_MODEL), 1.0),
        "a_norm": 1.0 + nrm(ks[1], (na, D_MODEL), 0.02),
        "a_w_in": nrm(ks[2], (na, D_MODEL, D_MODEL), D_MODEL ** -0.5),
        "a_lambda_re": -0.5 + nrm(ks[3], (na, G, P), 0.01),
        "a_lambda_im": math.pi * n_idx + nrm(ks[4], (na, G, P), 0.01),
        "a_log_step": jax.random.uniform(ks[5], (na, G), f32, math.log(DT_MIN), math.log(DT_MAX)),
        "a_b_re": nrm(ks[6], (na, G, P, C), (2 * C) ** -0.5),
        "a_b_im": nrm(ks[7], (na, G, P, C), (2 * C) ** -0.5),
        "a_c_re": nrm(ks[8], (na, G, C, P), (2 * P) ** -0.5),
        "a_c_im": nrm(ks[9], (na, G, C, P), (2 * P) ** -0.5),
        "a_d": nrm(ks[10], (na, G, C), 0.5),
        "a_w_glu": nrm(ks[11], (na, D_MODEL, 2 * D_MODEL), D_MODEL ** -0.5),
        "kv_norm": 1.0 + nrm(ks[12], (D_MODEL,), 0.02),
        "w_k": nrm(ks[13], (D_MODEL, D_MODEL), D_MODEL ** -0.5),
        "w_v": nrm(ks[14], (D_MODEL, D_MODEL), D_MODEL ** -0.5),
        "w_f": nrm(ks[15], (D_MODEL, H), D_MODEL ** -0.5),
        "b_f": jnp.linspace(FGATE_BIAS_LO, FGATE_BIAS_HI, H, dtype=f32) + nrm(ks[16], (H,), 0.1),
        "b_norm": 1.0 + nrm(ks[17], (nb, D_MODEL), 0.02),
        "b_w_q": nrm(ks[18], (nb, D_MODEL, D_MODEL), D_MODEL ** -0.5),
        "b_w_o": nrm(ks[19], (nb, D_MODEL, D_MODEL), D_MODEL ** -0.5),
        "ffn_norm": 1.0 + nrm(ks[20], (DEPTH, D_MODEL), 0.02),
        "ffn_w_up": nrm(ks[21], (DEPTH, D_MODEL, 2 * D_FF), D_MODEL ** -0.5),
        "ffn_conv_w": nrm(ks[22], (DEPTH, CONV_WIDTH, D_FF), CONV_WIDTH ** -0.5),
        "ffn_conv_b": nrm(ks[23], (DEPTH, D_FF), 0.01),
        "ffn_w_down": nrm(ks[24], (DEPTH, D_FF, D_MODEL), D_FF ** -0.5),
        "final_norm": 1.0 + nrm(ks[25], (D_MODEL,), 0.02),
    }


def _fwd_reference(x, a_norm, a_w_in, a_lambda_re, a_lambda_im, a_log_step, a_b_re, a_b_im,
              a_c_re, a_c_im, a_d, a_w_glu, kv_norm, w_k, w_v, w_f, b_f, b_norm, b_w_q,
              b_w_o, ffn_norm, ffn_w_up, ffn_conv_w, ffn_conv_b, ffn_w_down, final_norm):
    bsz, seq, _ = x.shape
    for i in range(N_A_LAYERS):
        x = x + s5_mixer(rms_norm(x, a_norm[i]), a_w_in[i], a_lambda_re[i], a_lambda_im[i],
                         a_log_step[i], a_b_re[i], a_b_im[i], a_c_re[i], a_c_im[i], a_d[i],
                         a_w_glu[i])
        x = x + conv_glu_ffn(rms_norm(x, ffn_norm[i]), ffn_w_up[i], ffn_conv_w[i],
                             ffn_conv_b[i], ffn_w_down[i])
    h_kv = rms_norm(x, kv_norm)
    k = (h_kv @ w_k).reshape(bsz, seq, FOX_HEADS, FOX_HEAD_DIM)
    v = (h_kv @ w_v).reshape(bsz, seq, FOX_HEADS, FOX_HEAD_DIM)
    log_f = jax.nn.log_sigmoid((h_kv @ w_f + b_f).astype(jnp.float32))
    logf_cum = jnp.cumsum(log_f, axis=1).transpose(0, 2, 1)
    for j in range(N_B_LAYERS):
        li = N_A_LAYERS + j
        x = x + fox_attention(rms_norm(x, b_norm[j]), b_w_q[j], b_w_o[j], k, v, logf_cum)
        x = x + conv_glu_ffn(rms_norm(x, ffn_norm[li]), ffn_w_up[li], ffn_conv_w[li],
                             ffn_conv_b[li], ffn_w_down[li])
    return rms_norm(x, final_norm)


import jax as _jax
import jax.numpy as _jnp

TWIN_FORMAT = 'train_step'
FWD_PARAMS = ['x', 'a_norm', 'a_w_in', 'a_lambda_re', 'a_lambda_im', 'a_log_step', 'a_b_re', 'a_b_im', 'a_c_re', 'a_c_im', 'a_d', 'a_w_glu', 'kv_norm', 'w_k', 'w_v', 'w_f', 'b_f', 'b_norm', 'b_w_q', 'b_w_o', 'ffn_norm', 'ffn_w_up', 'ffn_conv_w', 'ffn_conv_b', 'ffn_w_down', 'final_norm']
TWIN_WEIGHTS = ['a_norm', 'a_w_in', 'a_lambda_re', 'a_lambda_im', 'a_log_step', 'a_b_re', 'a_b_im', 'a_c_re', 'a_c_im', 'a_d', 'a_w_glu', 'kv_norm', 'w_k', 'w_v', 'w_f', 'b_f', 'b_norm', 'b_w_q', 'b_w_o', 'ffn_norm', 'ffn_w_up', 'ffn_conv_w', 'ffn_conv_b', 'ffn_w_down', 'final_norm']
TWIN_DIFF_INPUT = 'x'
TWIN_INPUTS = ['x', 'a_norm', 'a_w_in', 'a_lambda_re', 'a_lambda_im', 'a_log_step', 'a_b_re', 'a_b_im', 'a_c_re', 'a_c_im', 'a_d', 'a_w_glu', 'kv_norm', 'w_k', 'w_v', 'w_f', 'b_f', 'b_norm', 'b_w_q', 'b_w_o', 'ffn_norm', 'ffn_w_up', 'ffn_conv_w', 'ffn_conv_b', 'ffn_w_down', 'final_norm', 'loss_target', 'm_a_norm', 'm_a_w_in', 'm_a_lambda_re', 'm_a_lambda_im', 'm_a_log_step', 'm_a_b_re', 'm_a_b_im', 'm_a_c_re', 'm_a_c_im', 'm_a_d', 'm_a_w_glu', 'm_kv_norm', 'm_w_k', 'm_w_v', 'm_w_f', 'm_b_f', 'm_b_norm', 'm_b_w_q', 'm_b_w_o', 'm_ffn_norm', 'm_ffn_w_up', 'm_ffn_conv_w', 'm_ffn_conv_b', 'm_ffn_w_down', 'm_final_norm', 'v_a_norm', 'v_a_w_in', 'v_a_lambda_re', 'v_a_lambda_im', 'v_a_log_step', 'v_a_b_re', 'v_a_b_im', 'v_a_c_re', 'v_a_c_im', 'v_a_d', 'v_a_w_glu', 'v_kv_norm', 'v_w_k', 'v_w_v', 'v_w_f', 'v_b_f', 'v_b_norm', 'v_b_w_q', 'v_b_w_o', 'v_ffn_norm', 'v_ffn_w_up', 'v_ffn_conv_w', 'v_ffn_conv_b', 'v_ffn_w_down', 'v_final_norm']
TWIN_OUTPUTS = ['loss', 'grad_x', 'grad_a_norm', 'grad_a_w_in', 'grad_a_lambda_re', 'grad_a_lambda_im', 'grad_a_log_step', 'grad_a_b_re', 'grad_a_b_im', 'grad_a_c_re', 'grad_a_c_im', 'grad_a_d', 'grad_a_w_glu', 'grad_kv_norm', 'grad_w_k', 'grad_w_v', 'grad_w_f', 'grad_b_f', 'grad_b_norm', 'grad_b_w_q', 'grad_b_w_o', 'grad_ffn_norm', 'grad_ffn_w_up', 'grad_ffn_conv_w', 'grad_ffn_conv_b', 'grad_ffn_w_down', 'grad_final_norm', 'delta_a_norm', 'delta_a_w_in', 'delta_a_lambda_re', 'delta_a_lambda_im', 'delta_a_log_step', 'delta_a_b_re', 'delta_a_b_im', 'delta_a_c_re', 'delta_a_c_im', 'delta_a_d', 'delta_a_w_glu', 'delta_kv_norm', 'delta_w_k', 'delta_w_v', 'delta_w_f', 'delta_b_f', 'delta_b_norm', 'delta_b_w_q', 'delta_b_w_o', 'delta_ffn_norm', 'delta_ffn_w_up', 'delta_ffn_conv_w', 'delta_ffn_conv_b', 'delta_ffn_w_down', 'delta_final_norm', 'new_m_a_norm', 'new_m_a_w_in', 'new_m_a_lambda_re', 'new_m_a_lambda_im', 'new_m_a_log_step', 'new_m_a_b_re', 'new_m_a_b_im', 'new_m_a_c_re', 'new_m_a_c_im', 'new_m_a_d', 'new_m_a_w_glu', 'new_m_kv_norm', 'new_m_w_k', 'new_m_w_v', 'new_m_w_f', 'new_m_b_f', 'new_m_b_norm', 'new_m_b_w_q', 'new_m_b_w_o', 'new_m_ffn_norm', 'new_m_ffn_w_up', 'new_m_ffn_conv_w', 'new_m_ffn_conv_b', 'new_m_ffn_w_down', 'new_m_final_norm', 'new_v_a_norm', 'new_v_a_w_in', 'new_v_a_lambda_re', 'new_v_a_lambda_im', 'new_v_a_log_step', 'new_v_a_b_re', 'new_v_a_b_im', 'new_v_a_c_re', 'new_v_a_c_im', 'new_v_a_d', 'new_v_a_w_glu', 'new_v_kv_norm', 'new_v_w_k', 'new_v_w_v', 'new_v_w_f', 'new_v_b_f', 'new_v_b_norm', 'new_v_b_w_q', 'new_v_b_w_o', 'new_v_ffn_norm', 'new_v_ffn_w_up', 'new_v_ffn_conv_w', 'new_v_ffn_conv_b', 'new_v_ffn_w_down', 'new_v_final_norm']
TWIN_LEAF_KINDS = {'loss': 'loss', 'grad_x': 'grad_x', 'grad_a_norm': 'grad_w', 'grad_a_w_in': 'grad_w', 'grad_a_lambda_re': 'grad_w', 'grad_a_lambda_im': 'grad_w', 'grad_a_log_step': 'grad_w', 'grad_a_b_re': 'grad_w', 'grad_a_b_im': 'grad_w', 'grad_a_c_re': 'grad_w', 'grad_a_c_im': 'grad_w', 'grad_a_d': 'grad_w', 'grad_a_w_glu': 'grad_w', 'grad_kv_norm': 'grad_w', 'grad_w_k': 'grad_w', 'grad_w_v': 'grad_w', 'grad_w_f': 'grad_w', 'grad_b_f': 'grad_w', 'grad_b_norm': 'grad_w', 'grad_b_w_q': 'grad_w', 'grad_b_w_o': 'grad_w', 'grad_ffn_norm': 'grad_w', 'grad_ffn_w_up': 'grad_w', 'grad_ffn_conv_w': 'grad_w', 'grad_ffn_conv_b': 'grad_w', 'grad_ffn_w_down': 'grad_w', 'grad_final_norm': 'grad_w', 'delta_a_norm': 'delta_w', 'delta_a_w_in': 'delta_w', 'delta_a_lambda_re': 'delta_w', 'delta_a_lambda_im': 'delta_w', 'delta_a_log_step': 'delta_w', 'delta_a_b_re': 'delta_w', 'delta_a_b_im': 'delta_w', 'delta_a_c_re': 'delta_w', 'delta_a_c_im': 'delta_w', 'delta_a_d': 'delta_w', 'delta_a_w_glu': 'delta_w', 'delta_kv_norm': 'delta_w', 'delta_w_k': 'delta_w', 'delta_w_v': 'delta_w', 'delta_w_f': 'delta_w', 'delta_b_f': 'delta_w', 'delta_b_norm': 'delta_w', 'delta_b_w_q': 'delta_w', 'delta_b_w_o': 'delta_w', 'delta_ffn_norm': 'delta_w', 'delta_ffn_w_up': 'delta_w', 'delta_ffn_conv_w': 'delta_w', 'delta_ffn_conv_b': 'delta_w', 'delta_ffn_w_down': 'delta_w', 'delta_final_norm': 'delta_w', 'new_m_a_norm': 'new_m', 'new_m_a_w_in': 'new_m', 'new_m_a_lambda_re': 'new_m', 'new_m_a_lambda_im': 'new_m', 'new_m_a_log_step': 'new_m', 'new_m_a_b_re': 'new_m', 'new_m_a_b_im': 'new_m', 'new_m_a_c_re': 'new_m', 'new_m_a_c_im': 'new_m', 'new_m_a_d': 'new_m', 'new_m_a_w_glu': 'new_m', 'new_m_kv_norm': 'new_m', 'new_m_w_k': 'new_m', 'new_m_w_v': 'new_m', 'new_m_w_f': 'new_m', 'new_m_b_f': 'new_m', 'new_m_b_norm': 'new_m', 'new_m_b_w_q': 'new_m', 'new_m_b_w_o': 'new_m', 'new_m_ffn_norm': 'new_m', 'new_m_ffn_w_up': 'new_m', 'new_m_ffn_conv_w': 'new_m', 'new_m_ffn_conv_b': 'new_m', 'new_m_ffn_w_down': 'new_m', 'new_m_final_norm': 'new_m', 'new_v_a_norm': 'new_v', 'new_v_a_w_in': 'new_v', 'new_v_a_lambda_re': 'new_v', 'new_v_a_lambda_im': 'new_v', 'new_v_a_log_step': 'new_v', 'new_v_a_b_re': 'new_v', 'new_v_a_b_im': 'new_v', 'new_v_a_c_re': 'new_v', 'new_v_a_c_im': 'new_v', 'new_v_a_d': 'new_v', 'new_v_a_w_glu': 'new_v', 'new_v_kv_norm': 'new_v', 'new_v_w_k': 'new_v', 'new_v_w_v': 'new_v', 'new_v_w_f': 'new_v', 'new_v_b_f': 'new_v', 'new_v_b_norm': 'new_v', 'new_v_b_w_q': 'new_v', 'new_v_b_w_o': 'new_v', 'new_v_ffn_norm': 'new_v', 'new_v_ffn_w_up': 'new_v', 'new_v_ffn_conv_w': 'new_v', 'new_v_ffn_conv_b': 'new_v', 'new_v_ffn_w_down': 'new_v', 'new_v_final_norm': 'new_v'}


def _forward(args):
    return _fwd_reference(*[args[k] for k in FWD_PARAMS])


def _output_shape():
    def fwd():
        inp = _fwd_setup_inputs(0)
        return _fwd_reference(*[inp[k] for k in FWD_PARAMS])
    out = _jax.eval_shape(fwd)
    return out.shape, out.dtype

N_MICROBATCH = 1
ADAM_LR = 0.001
ADAM_B1 = 0.9
ADAM_B2 = 0.999
ADAM_EPS = 1e-08
ADAM_WD = 0.01
ADAM_STEP = 10
PER_EXAMPLE_BATCH_AXIS = {'x': 0, 'loss_target': 0}
SHARED_INPUTS = []
_WEIGHT_DTYPES = {'a_norm': _jnp.float32, 'a_w_in': _jnp.float32, 'a_lambda_re': _jnp.float32, 'a_lambda_im': _jnp.float32, 'a_log_step': _jnp.float32, 'a_b_re': _jnp.float32, 'a_b_im': _jnp.float32, 'a_c_re': _jnp.float32, 'a_c_im': _jnp.float32, 'a_d': _jnp.float32, 'a_w_glu': _jnp.float32, 'kv_norm': _jnp.float32, 'w_k': _jnp.float32, 'w_v': _jnp.float32, 'w_f': _jnp.float32, 'b_f': _jnp.float32, 'b_norm': _jnp.float32, 'b_w_q': _jnp.float32, 'b_w_o': _jnp.float32, 'ffn_norm': _jnp.float32, 'ffn_w_up': _jnp.float32, 'ffn_conv_w': _jnp.float32, 'ffn_conv_b': _jnp.float32, 'ffn_w_down': _jnp.float32, 'final_norm': _jnp.float32}
MOMENT_SCALE = {'a_norm': 2.746942e-02, 'a_w_in': 2.779284e-02, 'a_lambda_re': 2.940065e-03, 'a_lambda_im': 3.021453e-03, 'a_log_step': 2.374682e+00, 'a_b_re': 1.809753e-03, 'a_b_im': 1.831798e-03, 'a_c_re': 3.622417e-03, 'a_c_im': 3.709116e-03, 'a_d': 5.803906e-02, 'a_w_glu': 1.954850e-02, 'kv_norm': 5.726656e-02, 'w_k': 3.317443e-02, 'w_v': 4.452617e-02, 'w_f': 1.078832e-01, 'b_f': 3.453930e-01, 'b_norm': 2.310011e-02, 'b_w_q': 2.339873e-02, 'b_w_o': 3.143042e-02, 'ffn_norm': 1.019595e-01, 'ffn_w_up': 4.287882e-02, 'ffn_conv_w': 4.350283e-02, 'ffn_conv_b': 4.166277e-02, 'ffn_w_down': 7.007459e-02, 'final_norm': 3.194839e+01}


def _to_microbatches(a, axis):
    t = _jnp.moveaxis(a, axis, 0)
    t = t.reshape((N_MICROBATCH, t.shape[0] // N_MICROBATCH) + t.shape[1:])
    return _jnp.moveaxis(t, 1, axis + 1)


def setup_inputs(seed: int = 0) -> dict:
    inp = _fwd_setup_inputs(seed)
    key = _jax.random.fold_in(_jax.random.key(seed), 7919)
    shape, _ = _output_shape()
    out = dict(inp)
    out["loss_target"] = _jax.random.normal(_jax.random.fold_in(key, 0), shape, _jnp.float32)
    for i, name in enumerate(TWIN_WEIGHTS):
        w = inp[name].astype(_jnp.float32)
        if MOMENT_SCALE is None:
            s = _jnp.sqrt(_jnp.mean(_jnp.square(w)) + 1e-30)
        else:
            s = MOMENT_SCALE[name]
        km, kv = _jax.random.split(_jax.random.fold_in(key, i + 1))
        out[name] = w
        out["m_" + name] = s * _jax.random.normal(km, w.shape, _jnp.float32)
        out["v_" + name] = (s * s) * _jax.random.uniform(kv, w.shape, _jnp.float32, 0.5, 1.5)
    if N_MICROBATCH > 1:
        for name, axis in PER_EXAMPLE_BATCH_AXIS.items():
            out[name] = _to_microbatches(out[name], axis)
    return {'x': out['x'], 'a_norm': out['a_norm'], 'a_w_in': out['a_w_in'], 'a_lambda_re': out['a_lambda_re'], 'a_lambda_im': out['a_lambda_im'], 'a_log_step': out['a_log_step'], 'a_b_re': out['a_b_re'], 'a_b_im': out['a_b_im'], 'a_c_re': out['a_c_re'], 'a_c_im': out['a_c_im'], 'a_d': out['a_d'], 'a_w_glu': out['a_w_glu'], 'kv_norm': out['kv_norm'], 'w_k': out['w_k'], 'w_v': out['w_v'], 'w_f': out['w_f'], 'b_f': out['b_f'], 'b_norm': out['b_norm'], 'b_w_q': out['b_w_q'], 'b_w_o': out['b_w_o'], 'ffn_norm': out['ffn_norm'], 'ffn_w_up': out['ffn_w_up'], 'ffn_conv_w': out['ffn_conv_w'], 'ffn_conv_b': out['ffn_conv_b'], 'ffn_w_down': out['ffn_w_down'], 'final_norm': out['final_norm'], 'loss_target': out['loss_target'], 'm_a_norm': out['m_a_norm'], 'm_a_w_in': out['m_a_w_in'], 'm_a_lambda_re': out['m_a_lambda_re'], 'm_a_lambda_im': out['m_a_lambda_im'], 'm_a_log_step': out['m_a_log_step'], 'm_a_b_re': out['m_a_b_re'], 'm_a_b_im': out['m_a_b_im'], 'm_a_c_re': out['m_a_c_re'], 'm_a_c_im': out['m_a_c_im'], 'm_a_d': out['m_a_d'], 'm_a_w_glu': out['m_a_w_glu'], 'm_kv_norm': out['m_kv_norm'], 'm_w_k': out['m_w_k'], 'm_w_v': out['m_w_v'], 'm_w_f': out['m_w_f'], 'm_b_f': out['m_b_f'], 'm_b_norm': out['m_b_norm'], 'm_b_w_q': out['m_b_w_q'], 'm_b_w_o': out['m_b_w_o'], 'm_ffn_norm': out['m_ffn_norm'], 'm_ffn_w_up': out['m_ffn_w_up'], 'm_ffn_conv_w': out['m_ffn_conv_w'], 'm_ffn_conv_b': out['m_ffn_conv_b'], 'm_ffn_w_down': out['m_ffn_w_down'], 'm_final_norm': out['m_final_norm'], 'v_a_norm': out['v_a_norm'], 'v_a_w_in': out['v_a_w_in'], 'v_a_lambda_re': out['v_a_lambda_re'], 'v_a_lambda_im': out['v_a_lambda_im'], 'v_a_log_step': out['v_a_log_step'], 'v_a_b_re': out['v_a_b_re'], 'v_a_b_im': out['v_a_b_im'], 'v_a_c_re': out['v_a_c_re'], 'v_a_c_im': out['v_a_c_im'], 'v_a_d': out['v_a_d'], 'v_a_w_glu': out['v_a_w_glu'], 'v_kv_norm': out['v_kv_norm'], 'v_w_k': out['v_w_k'], 'v_w_v': out['v_w_v'], 'v_w_f': out['v_w_f'], 'v_b_f': out['v_b_f'], 'v_b_norm': out['v_b_norm'], 'v_b_w_q': out['v_b_w_q'], 'v_b_w_o': out['v_b_w_o'], 'v_ffn_norm': out['v_ffn_norm'], 'v_ffn_w_up': out['v_ffn_w_up'], 'v_ffn_conv_w': out['v_ffn_conv_w'], 'v_ffn_conv_b': out['v_ffn_conv_b'], 'v_ffn_w_down': out['v_ffn_w_down'], 'v_final_norm': out['v_final_norm']}


def _loss(weights, diff, rest, loss_target):
    with _jax.named_scope("forward"):
        args = {**rest, TWIN_DIFF_INPUT: diff, **{k: w.astype(_WEIGHT_DTYPES[k]) for k, w in weights.items()}}
        y = _forward(args)
    with _jax.named_scope("loss_head"):
        err = _jnp.square(y.astype(_jnp.float32) - loss_target)
        return 0.5 * _jnp.sum(_jnp.mean(err, axis=-1)) if err.ndim else 0.5 * err


def _adamw(w, g, m, v):
    m = ADAM_B1 * m + (1.0 - ADAM_B1) * g
    v = ADAM_B2 * v + (1.0 - ADAM_B2) * _jnp.square(g)
    m_hat = m / (1.0 - ADAM_B1 ** ADAM_STEP)
    v_hat = v / (1.0 - ADAM_B2 ** ADAM_STEP)
    delta = -ADAM_LR * (m_hat / (_jnp.sqrt(v_hat) + ADAM_EPS) + ADAM_WD * w)
    return delta, m, v


def reference(x, a_norm, a_w_in, a_lambda_re, a_lambda_im, a_log_step, a_b_re, a_b_im, a_c_re, a_c_im, a_d, a_w_glu, kv_norm, w_k, w_v, w_f, b_f, b_norm, b_w_q, b_w_o, ffn_norm, ffn_w_up, ffn_conv_w, ffn_conv_b, ffn_w_down, final_norm, loss_target, m_a_norm, m_a_w_in, m_a_lambda_re, m_a_lambda_im, m_a_log_step, m_a_b_re, m_a_b_im, m_a_c_re, m_a_c_im, m_a_d, m_a_w_glu, m_kv_norm, m_w_k, m_w_v, m_w_f, m_b_f, m_b_norm, m_b_w_q, m_b_w_o, m_ffn_norm, m_ffn_w_up, m_ffn_conv_w, m_ffn_conv_b, m_ffn_w_down, m_final_norm, v_a_norm, v_a_w_in, v_a_lambda_re, v_a_lambda_im, v_a_log_step, v_a_b_re, v_a_b_im, v_a_c_re, v_a_c_im, v_a_d, v_a_w_glu, v_kv_norm, v_w_k, v_w_v, v_w_f, v_b_f, v_b_norm, v_b_w_q, v_b_w_o, v_ffn_norm, v_ffn_w_up, v_ffn_conv_w, v_ffn_conv_b, v_ffn_w_down, v_final_norm):
    given = dict(x=x, a_norm=a_norm, a_w_in=a_w_in, a_lambda_re=a_lambda_re, a_lambda_im=a_lambda_im, a_log_step=a_log_step, a_b_re=a_b_re, a_b_im=a_b_im, a_c_re=a_c_re, a_c_im=a_c_im, a_d=a_d, a_w_glu=a_w_glu, kv_norm=kv_norm, w_k=w_k, w_v=w_v, w_f=w_f, b_f=b_f, b_norm=b_norm, b_w_q=b_w_q, b_w_o=b_w_o, ffn_norm=ffn_norm, ffn_w_up=ffn_w_up, ffn_conv_w=ffn_conv_w, ffn_conv_b=ffn_conv_b, ffn_w_down=ffn_w_down, final_norm=final_norm, loss_target=loss_target, m_a_norm=m_a_norm, m_a_w_in=m_a_w_in, m_a_lambda_re=m_a_lambda_re, m_a_lambda_im=m_a_lambda_im, m_a_log_step=m_a_log_step, m_a_b_re=m_a_b_re, m_a_b_im=m_a_b_im, m_a_c_re=m_a_c_re, m_a_c_im=m_a_c_im, m_a_d=m_a_d, m_a_w_glu=m_a_w_glu, m_kv_norm=m_kv_norm, m_w_k=m_w_k, m_w_v=m_w_v, m_w_f=m_w_f, m_b_f=m_b_f, m_b_norm=m_b_norm, m_b_w_q=m_b_w_q, m_b_w_o=m_b_w_o, m_ffn_norm=m_ffn_norm, m_ffn_w_up=m_ffn_w_up, m_ffn_conv_w=m_ffn_conv_w, m_ffn_conv_b=m_ffn_conv_b, m_ffn_w_down=m_ffn_w_down, m_final_norm=m_final_norm, v_a_norm=v_a_norm, v_a_w_in=v_a_w_in, v_a_lambda_re=v_a_lambda_re, v_a_lambda_im=v_a_lambda_im, v_a_log_step=v_a_log_step, v_a_b_re=v_a_b_re, v_a_b_im=v_a_b_im, v_a_c_re=v_a_c_re, v_a_c_im=v_a_c_im, v_a_d=v_a_d, v_a_w_glu=v_a_w_glu, v_kv_norm=v_kv_norm, v_w_k=v_w_k, v_w_v=v_w_v, v_w_f=v_w_f, v_b_f=v_b_f, v_b_norm=v_b_norm, v_b_w_q=v_b_w_q, v_b_w_o=v_b_w_o, v_ffn_norm=v_ffn_norm, v_ffn_w_up=v_ffn_w_up, v_ffn_conv_w=v_ffn_conv_w, v_ffn_conv_b=v_ffn_conv_b, v_ffn_w_down=v_ffn_w_down, v_final_norm=v_final_norm)
    weights = {n: given[n] for n in TWIN_WEIGHTS}
    shared = {n: given[n] for n in SHARED_INPUTS}
    per_example = {n: given[n] for n in ['x']}
    grad_fn = _jax.value_and_grad(_loss, argnums=(0, 1))

    def one_microbatch(ex, loss_target):
        ex = dict(ex)
        diff = ex.pop(TWIN_DIFF_INPUT)
        return grad_fn(weights, diff, {**shared, **ex}, loss_target)

    if N_MICROBATCH == 1:
        loss, (grad_w, grad_x) = one_microbatch(per_example, given["loss_target"])
    else:
        def body(carry, xs):
            loss_sum, grad_sum = carry
            l_k, (gw_k, gx_k) = one_microbatch(xs[0], xs[1])
            with _jax.named_scope("update"):
                return (loss_sum + l_k, _jax.tree.map(_jnp.add, grad_sum, gw_k)), gx_k

        init = (_jnp.zeros((), _jnp.float32), _jax.tree.map(_jnp.zeros_like, weights))
        (loss, grad_w), grad_x = _jax.lax.scan(body, init, (per_example, given["loss_target"]))
    with _jax.named_scope("update"):
        delta_w, new_m, new_v = {}, {}, {}
        for n in TWIN_WEIGHTS:
            delta_w[n], new_m[n], new_v[n] = _adamw(weights[n], grad_w[n], given["m_" + n], given["v_" + n])
    return (loss, grad_x, *[grad_w[n] for n in TWIN_WEIGHTS], *[delta_w[n] for n in TWIN_WEIGHTS],
            *[new_m[n] for n in TWIN_WEIGHTS], *[new_v[n] for n in TWIN_WEIGHTS])
```

```python
import functools
import math

import jax
import jax.numpy as jnp
from jax import lax
from jax.experimental import pallas as pl
from jax.experimental.pallas import tpu as pltpu

F32 = jnp.float32
BF16 = jnp.bfloat16
HI = lax.Precision.HIGHEST
MESH_AXES = ("x", "y", "c")
N_DEV = 8
VMEM_LIMIT_BYTES = 48 * 1024 * 1024

NORM_EPS = 1e-6
S5_GROUP = 16
S5_STATE = 64
S5_CHUNK = 16
S5_TC = S5_CHUNK * S5_GROUP
S5_2P = 2 * S5_STATE
HEAD_DIM = 128
MASK_VALUE = -1e30
GELU_C = math.sqrt(2.0 / math.pi)
GELU_A = 0.044715

ADAM_LR = 0.001
ADAM_B1 = 0.9
ADAM_B2 = 0.999
ADAM_EPS = 1e-08
ADAM_WD = 0.01
ADAM_STEP = 10


def _params(sem=None):
    return pltpu.CompilerParams(dimension_semantics=sem, vmem_limit_bytes=VMEM_LIMIT_BYTES)


def _pick(n, cands):
    for c in cands:
        if n % c == 0:
            return c
    return n


def _gelu(x):
    return 0.5 * x * (1.0 + jnp.tanh(GELU_C * (x + GELU_A * x * x * x)))


def _gelu_grad(x):
    t = jnp.tanh(GELU_C * (x + GELU_A * x * x * x))
    return 0.5 * (1.0 + t) + 0.5 * x * (1.0 - t * t) * GELU_C * (1.0 + 3.0 * GELU_A * x * x)


def _my_index():
    return 4 * lax.axis_index("x") + 2 * lax.axis_index("y") + lax.axis_index("c")


def _peer(k):
    x, y, c = lax.axis_index("x"), lax.axis_index("y"), lax.axis_index("c")
    px = 1 - x if (k >> 2) & 1 else x
    py = 1 - y if (k >> 1) & 1 else y
    pc = 1 - c if k & 1 else c
    return (px, py, pc), 4 * px + 2 * py + pc


def _exchange(src_of, dst_of, send_sems, recv_sems, local_sem):
    me = _my_index()
    mine = pltpu.make_async_copy(src_of(me), dst_of(me), local_sem)
    mine.start()
    copies = []
    for k in range(1, N_DEV):
        peer, pidx = _peer(k)
        cp = pltpu.make_async_remote_copy(
            src_ref=src_of(pidx), dst_ref=dst_of(me), send_sem=send_sems.at[k - 1],
            recv_sem=recv_sems.at[k - 1], device_id=peer, device_id_type=pl.DeviceIdType.MESH)
        cp.start()
        copies.append(cp)
    for k in range(1, N_DEV):
        peer, pidx = _peer(k)
        copies[k - 1].wait_send()
        pltpu.make_async_remote_copy(
            src_ref=src_of(pidx), dst_ref=dst_of(pidx), send_sem=send_sems.at[k - 1],
            recv_sem=recv_sems.at[k - 1], device_id=peer, device_id_type=pl.DeviceIdType.MESH).wait_recv()
    mine.wait()


_COMM_SCRATCH = [pltpu.SemaphoreType.DMA((N_DEV - 1,)), pltpu.SemaphoreType.DMA((N_DEV - 1,)),
                 pltpu.SemaphoreType.DMA]
_HBM = pl.BlockSpec(memory_space=pltpu.HBM)


def all_gather(xs, axis, name):
    nd = xs.ndim
    n = xs.shape[axis]
    full = list(xs.shape)
    full[axis] = n * N_DEV

    def body(x_ref, out_ref, send_sems, recv_sems, local_sem):
        def dst_of(idx):
            sl = [slice(None)] * nd
            sl[axis] = pl.ds(idx * n, n)
            return out_ref.at[tuple(sl)]

        _exchange(lambda idx: x_ref, dst_of, send_sems, recv_sems, local_sem)

    return pl.pallas_call(
        body, name=name, out_shape=jax.ShapeDtypeStruct(tuple(full), xs.dtype),
        in_specs=[_HBM], out_specs=_HBM, scratch_shapes=_COMM_SCRATCH,
    )(xs)


def scatter_blocks(g, axis, name):
    nd = g.ndim
    n = g.shape[axis] // N_DEV
    blk = list(g.shape)
    blk[axis] = n

    def body(g_ref, out_ref, send_sems, recv_sems, local_sem):
        def src_of(idx):
            sl = [slice(None)] * nd
            sl[axis] = pl.ds(idx * n, n)
            return g_ref.at[tuple(sl)]

        _exchange(src_of, lambda idx: out_ref.at[idx], send_sems, recv_sems, local_sem)

    return pl.pallas_call(
        body, name=name, out_shape=jax.ShapeDtypeStruct((N_DEV, *blk), g.dtype),
        in_specs=[_HBM], out_specs=_HBM, scratch_shapes=_COMM_SCRATCH,
    )(g)


def matmul(a, b, *, ta=False, tb=False, out_dtype=F32, add=None, name):
    m = a.shape[1] if ta else a.shape[0]
    kk = a.shape[0] if ta else a.shape[1]
    n = b.shape[0] if tb else b.shape[1]
    tm = _pick(m, (1024, 512, 256, 128))
    tn = _pick(n, (1024, 512, 256, 128))
    tk = _pick(kk, (512, 256, 128))
    nk = kk // tk
    dims = (((0,) if ta else (1,), (1,) if tb else (0,)), ((), ()))

    def body(*refs):
        if add is None:
            a_ref, b_ref, o_ref, acc = refs
        else:
            a_ref, b_ref, add_ref, o_ref, acc = refs
        k = pl.program_id(2)

        @pl.when(k == 0)
        def _():
            acc[...] = jnp.zeros_like(acc)

        acc[...] += lax.dot_general(a_ref[...], b_ref[...], dims, preferred_element_type=F32)

        @pl.when(k == nk - 1)
        def _():
            r = acc[...]
            if add is not None:
                r = r + add_ref[...].astype(F32)
            o_ref[...] = r.astype(out_dtype)

    a_spec = (pl.BlockSpec((tk, tm), lambda i, j, k: (k, i)) if ta
              else pl.BlockSpec((tm, tk), lambda i, j, k: (i, k)))
    b_spec = (pl.BlockSpec((tn, tk), lambda i, j, k: (j, k)) if tb
              else pl.BlockSpec((tk, tn), lambda i, j, k: (k, j)))
    o_spec = pl.BlockSpec((tm, tn), lambda i, j, k: (i, j))
    ins = [a, b] + ([] if add is None else [add])
    in_specs = [a_spec, b_spec] + ([] if add is None else [o_spec])
    return pl.pallas_call(
        body, name=name, out_shape=jax.ShapeDtypeStruct((m, n), out_dtype),
        grid=(m // tm, n // tn, nk), in_specs=in_specs, out_specs=o_spec,
        scratch_shapes=[pltpu.VMEM((tm, tn), F32)],
        compiler_params=_params(("parallel", "parallel", "arbitrary")),
    )(*ins)


def rowwise(fn, ins, rows, outs, accs, *, tl, name):
    n_rows = ins[0].shape[0]
    tl = min(tl, n_rows)
    n_in, n_row, n_out, n_acc = len(ins), len(rows), len(outs), len(accs)

    def body(*refs):
        in_refs = refs[:n_in]
        row_refs = refs[n_in:n_in + n_row]
        out_refs = refs[n_in + n_row:n_in + n_row + n_out]
        acc_refs = refs[n_in + n_row + n_out:]
        res_outs, res_accs = fn(*[r[...] for r in in_refs], *[r[...] for r in row_refs])
        for r, v in zip(out_refs, res_outs):
            r[...] = v.astype(r.dtype)
        if n_acc:
            @pl.when(pl.program_id(0) == 0)
            def _():
                for r in acc_refs:
                    r[...] = jnp.zeros_like(r)

            for r, v in zip(acc_refs, res_accs):
                r[...] += v

    in_specs = [pl.BlockSpec((tl, a.shape[1]), lambda i: (i, 0)) for a in ins]
    in_specs += [pl.BlockSpec((1, a.shape[1]), lambda i: (0, 0)) for a in rows]
    out_specs = [pl.BlockSpec((tl, w), lambda i: (i, 0)) for w, _ in outs]
    out_specs += [pl.BlockSpec((1, w), lambda i: (0, 0)) for w in accs]
    out_shape = [jax.ShapeDtypeStruct((n_rows, w), dt) for w, dt in outs]
    out_shape += [jax.ShapeDtypeStruct((1, w), F32) for w in accs]
    res = pl.pallas_call(
        body, name=name, out_shape=out_shape, grid=(n_rows // tl,), in_specs=in_specs,
        out_specs=out_specs, compiler_params=_params(("arbitrary",)),
    )(*ins, *rows)
    return res


def _rsum(v):
    return jnp.sum(v, axis=0, keepdims=True)


def norm_fwd(x, g, name):
    def fn(xt, gt):
        r = lax.rsqrt(jnp.mean(xt * xt, axis=-1, keepdims=True) + NORM_EPS)
        return (xt * r * gt,), ()

    d = x.shape[1]
    return rowwise(fn, [x], [g], [(d, BF16)], [], tl=512, name=name)[0]


def norm_bwd(x, g, dh, dres, name):
    def fn(xt, dht, drt, gt):
        r = lax.rsqrt(jnp.mean(xt * xt, axis=-1, keepdims=True) + NORM_EPS)
        w = dht * gt
        dx = r * w - xt * (r * r * r) * jnp.mean(xt * w, axis=-1, keepdims=True)
        return (drt + dx,), (_rsum(dht * xt * r),)

    d = x.shape[1]
    dx, dg = rowwise(fn, [x, dh, dres], [g], [(d, F32)], [d], tl=256, name=name)
    return dx, dg


def loss_and_grad(x, g, target, name):
    d = x.shape[1]

    def fn(xt, tt, gt):
        r = lax.rsqrt(jnp.mean(xt * xt, axis=-1, keepdims=True) + NORM_EPS)
        y = xt * r * gt
        diff = y - tt
        dy = diff * (1.0 / d)
        w = dy * gt
        dx = r * w - xt * (r * r * r) * jnp.mean(xt * w, axis=-1, keepdims=True)
        return (dx,), (_rsum(dy * xt * r), _rsum(diff * diff))

    dx, dg, sq = rowwise(fn, [x, target], [g], [(d, F32)], [d, d], tl=256, name=name)
    return dx, dg, sq


def glu_fwd(z, x, name):
    d = x.shape[1]

    def fn(zt, xt):
        return (xt + zt[:, :d] * jax.nn.sigmoid(zt[:, d:]),), ()

    return rowwise(fn, [z, x], [], [(d, F32)], [], tl=256, name=name)[0]


def glu_bwd(z, dx, name):
    d = dx.shape[1]

    def fn(zt, dxt):
        val, gate = zt[:, :d], zt[:, d:]
        s = jax.nn.sigmoid(gate)
        return (jnp.concatenate([dxt * s, dxt * val * s * (1.0 - s)], axis=1),), ()

    return rowwise(fn, [z, dx], [], [(2 * d, BF16)], [], tl=256, name=name)[0]


def add_cast(a, b, name):
    def fn(at, bt):
        return (at + bt,), ()

    return rowwise(fn, [a, b], [], [(a.shape[1], BF16)], [], tl=512, name=name)[0]


def _conv_tiles(n_rows, f):
    return min(256, n_rows), _pick(f, (512, 256, 128))


def _shifted(gate, h6, h7):
    row = lax.broadcasted_iota(jnp.int32, gate.shape, 0)
    g1 = jnp.where(row == 0, h7, pltpu.roll(gate, 1, 0))
    g2 = jnp.where(row == 0, h6, jnp.where(row == 1, h7, pltpu.roll(gate, 2, 0)))
    return g1, g2


def convglu_fwd(up, conv_w, conv_b, name):
    n_rows, f2 = up.shape
    f = f2 // 2
    tl, tc = _conv_tiles(n_rows, f)
    nc = f // tc
    hb = tl // 8

    def body(gate_ref, val_ref, halo_ref, w_ref, b_ref, o_ref):
        l = pl.program_id(1)
        live = (l > 0).astype(F32)
        h6 = halo_ref[6:7, :] * live
        h7 = halo_ref[7:8, :] * live
        gate = gate_ref[...]
        g1, g2 = _shifted(gate, h6, h7)
        gc = w_ref[0:1, :] * g2 + w_ref[1:2, :] * g1 + w_ref[2:3, :] * gate + b_ref[...]
        o_ref[...] = (_gelu(gc) * val_ref[...]).astype(o_ref.dtype)

    return pl.pallas_call(
        body, name=name, out_shape=jax.ShapeDtypeStruct((n_rows, f), BF16),
        grid=(nc, n_rows // tl),
        in_specs=[pl.BlockSpec((tl, tc), lambda c, l: (l, c)),
                  pl.BlockSpec((tl, tc), lambda c, l: (l, c + nc)),
                  pl.BlockSpec((8, tc), lambda c, l: (jnp.maximum(l * hb - 1, 0), c)),
                  pl.BlockSpec((3, tc), lambda c, l: (0, c)),
                  pl.BlockSpec((1, tc), lambda c, l: (0, c))],
        out_specs=pl.BlockSpec((tl, tc), lambda c, l: (l, c)),
        compiler_params=_params(("parallel", "arbitrary")),
    )(up, up, up, conv_w, conv_b)


def convglu_bwd(up, da, conv_w, conv_b, name):
    n_rows, f2 = up.shape
    f = f2 // 2
    tl, tc = _conv_tiles(n_rows, f)
    nc = f // tc
    nl = n_rows // tl
    hb = tl // 8
    last_halo = n_rows // 8 - 1

    def body(gate_ref, val_ref, da_ref, gprev_ref, gnext_ref, vnext_ref, danext_ref, w_ref, b_ref,
             dup_ref, dw_ref, db_ref):
        cc = pl.program_id(0)
        l = pl.program_id(1)
        live_prev = (l > 0).astype(F32)
        live_next = (l < nl - 1).astype(F32)
        h6 = gprev_ref[6:7, :] * live_prev
        h7 = gprev_ref[7:8, :] * live_prev
        w0, w1, w2 = w_ref[0:1, :], w_ref[1:2, :], w_ref[2:3, :]

        @pl.when(cc >= nc)
        def _():
            gate = gate_ref[...]
            g1, g2 = _shifted(gate, h6, h7)
            gc = w0 * g2 + w1 * g1 + w2 * gate + b_ref[...]
            dup_ref[...] = (da_ref[...] * _gelu(gc)).astype(dup_ref.dtype)

        @pl.when(cc < nc)
        def _():
            @pl.when(l == 0)
            def _():
                dw_ref[...] = jnp.zeros_like(dw_ref)
                db_ref[...] = jnp.zeros_like(db_ref)

            gate = jnp.concatenate([gate_ref[...], gnext_ref[...]], axis=0)
            val = jnp.concatenate([val_ref[...], vnext_ref[...]], axis=0)
            dae = jnp.concatenate([da_ref[...], danext_ref[...] * live_next], axis=0)
            g1, g2 = _shifted(gate, h6, h7)
            gc = w0 * g2 + w1 * g1 + w2 * gate + b_ref[...]
            dgc = dae * val * _gelu_grad(gc)
            ext = tl + 8
            d1 = pltpu.roll(dgc, ext - 1, 0)
            d2 = pltpu.roll(dgc, ext - 2, 0)
            dgate = w2 * dgc + w1 * d1 + w0 * d2
            dup_ref[...] = dgate[:tl].astype(dup_ref.dtype)
            dg = dgc[:tl]
            db_ref[...] += _rsum(dg)
            dw_ref[0:1, :] += _rsum(dg * g2[:tl])
            dw_ref[1:2, :] += _rsum(dg * g1[:tl])
            dw_ref[2:3, :] += _rsum(dg * gate[:tl])

    def cmod(cc):
        return jnp.where(cc >= nc, cc - nc, cc)

    def cacc(cc):
        return jnp.minimum(cc, nc - 1)

    def prev_blk(l):
        return jnp.maximum(l * hb - 1, 0)

    def next_blk(l):
        return jnp.minimum((l + 1) * hb, last_halo)

    dup, dw, db = pl.pallas_call(
        body, name=name,
        out_shape=[jax.ShapeDtypeStruct((n_rows, f2), BF16), jax.ShapeDtypeStruct((3, f), F32),
                   jax.ShapeDtypeStruct((1, f), F32)],
        grid=(2 * nc, nl),
        in_specs=[pl.BlockSpec((tl, tc), lambda cc, l: (l, cmod(cc))),
                  pl.BlockSpec((tl, tc), lambda cc, l: (l, cmod(cc) + nc)),
                  pl.BlockSpec((tl, tc), lambda cc, l: (l, cmod(cc))),
                  pl.BlockSpec((8, tc), lambda cc, l: (prev_blk(l), cmod(cc))),
                  pl.BlockSpec((8, tc), lambda cc, l: (next_blk(l), cmod(cc))),
                  pl.BlockSpec((8, tc), lambda cc, l: (next_blk(l), cmod(cc) + nc)),
                  pl.BlockSpec((8, tc), lambda cc, l: (next_blk(l), cmod(cc))),
                  pl.BlockSpec((3, tc), lambda cc, l: (0, cmod(cc))),
                  pl.BlockSpec((1, tc), lambda cc, l: (0, cmod(cc)))],
        out_specs=[pl.BlockSpec((tl, tc), lambda cc, l: (l, cc)),
                   pl.BlockSpec((3, tc), lambda cc, l: (0, cacc(cc))),
                   pl.BlockSpec((1, tc), lambda cc, l: (0, cacc(cc)))],
        compiler_params=_params(("arbitrary", "arbitrary")),
    )(up, up, da, up, up, up, da, conv_w, conv_b)
    return dup, dw, db


def _log_sigmoid(z):
    return jnp.minimum(z, 0.0) - jnp.log(1.0 + jnp.exp(-jnp.abs(z)))


def fgate_fwd(zf, bf, n_heads, name):
    n_rows = zf.shape[0]
    tl = min(512, n_rows)

    def body(z_ref, b_ref, c_ref, carry):
        @pl.when(pl.program_id(0) == 0)
        def _():
            carry[...] = jnp.zeros_like(carry)

        lf = _log_sigmoid(z_ref[...] + b_ref[...])
        lft = lf.T
        r = lax.broadcasted_iota(jnp.int32, (tl, tl), 0)
        c = lax.broadcasted_iota(jnp.int32, (tl, tl), 1)
        tri = (r <= c).astype(F32)
        cum = jnp.dot(lft, tri, precision=HI, preferred_element_type=F32) + carry[...]
        c_ref[...] = cum[:n_heads]
        carry[...] += jnp.sum(lft, axis=1, keepdims=True)

    return pl.pallas_call(
        body, name=name, out_shape=jax.ShapeDtypeStruct((n_heads, n_rows), F32),
        grid=(n_rows // tl,),
        in_specs=[pl.BlockSpec((tl, 128), lambda i: (i, 0)), pl.BlockSpec((1, 128), lambda i: (0, 0))],
        out_specs=pl.BlockSpec((n_heads, tl), lambda i: (0, i)),
        scratch_shapes=[pltpu.VMEM((128, 1), F32)],
        compiler_params=_params(("arbitrary",)),
    )(zf, bf)


def fgate_bwd(dc_a, dc_b, zf, bf, name):
    n_heads, n_rows = dc_a.shape
    tl = min(512, n_rows)
    nl = n_rows // tl

    def body(da_ref, db_ref, z_ref, b_ref, dz_ref, dbias_ref, carry):
        @pl.when(pl.program_id(0) == 0)
        def _():
            carry[...] = jnp.zeros_like(carry)
            dbias_ref[...] = jnp.zeros_like(dbias_ref)

        dc = da_ref[...] + db_ref[...]
        dcp = jnp.concatenate([dc, jnp.zeros((128 - n_heads, tl), F32)], axis=0)
        r = lax.broadcasted_iota(jnp.int32, (tl, tl), 0)
        c = lax.broadcasted_iota(jnp.int32, (tl, tl), 1)
        later = (c >= r).astype(F32)
        local = lax.dot_general(later, dcp, (((1,), (1,)), ((), ())), precision=HI,
                                preferred_element_type=F32)
        dlf = local + carry[...]
        carry[...] += local[0:1, :]
        z = z_ref[...] + b_ref[...]
        dz = dlf * jax.nn.sigmoid(-z)
        dz_ref[...] = dz.astype(dz_ref.dtype)
        dbias_ref[...] += _rsum(dz)

    return pl.pallas_call(
        body, name=name,
        out_shape=[jax.ShapeDtypeStruct((n_rows, 128), BF16), jax.ShapeDtypeStruct((1, 128), F32)],
        grid=(nl,),
        in_specs=[pl.BlockSpec((n_heads, tl), lambda i: (0, nl - 1 - i)),
                  pl.BlockSpec((n_heads, tl), lambda i: (0, nl - 1 - i)),
                  pl.BlockSpec((tl, 128), lambda i: (nl - 1 - i, 0)),
                  pl.BlockSpec((1, 128), lambda i: (0, 0))],
        out_specs=[pl.BlockSpec((tl, 128), lambda i: (nl - 1 - i, 0)),
                   pl.BlockSpec((1, 128), lambda i: (0, 0))],
        scratch_shapes=[pltpu.VMEM((1, 128), F32)],
        compiler_params=_params(("arbitrary",)),
    )(dc_a, dc_b, zf, bf)


def _pick_row(blk, h):
    row = lax.broadcasted_iota(jnp.int32, blk.shape, 0)
    return jnp.sum(jnp.where(row == h, blk, 0.0), axis=0, keepdims=True)


def _scores(q, k, cq_blk, ck_blk, h, i, j, tq, scale):
    s = lax.dot_general(q, k, (((1,), (1,)), ((), ())), preferred_element_type=F32) * scale
    rq = lax.broadcasted_iota(jnp.int32, cq_blk.shape, 0)
    cq = lax.broadcasted_iota(jnp.int32, cq_blk.shape, 1)
    c_ref = jnp.sum(jnp.where((rq == h) & (cq == 0), cq_blk, 0.0), keepdims=True)
    s = s + (c_ref - _pick_row(ck_blk, h))
    rows = i * tq + lax.broadcasted_iota(jnp.int32, s.shape, 0)
    cols = j * tq + lax.broadcasted_iota(jnp.int32, s.shape, 1)
    return jnp.where(cols <= rows, s, MASK_VALUE)


def flash_fwd(q, k, v, ct, name):
    n_rows, d = q.shape
    n_heads = d // HEAD_DIM
    tq = min(512, n_rows)
    nq = n_rows // tq
    scale = HEAD_DIM ** -0.5

    def body(q_ref, k_ref, v_ref, cq_ref, ck_ref, o_ref, o32_ref, lse_ref, m_sc, l_sc, acc_sc):
        i, h, j = pl.program_id(0), pl.program_id(1), pl.program_id(2)

        @pl.when(j == 0)
        def _():
            m_sc[...] = jnp.full_like(m_sc, MASK_VALUE)
            l_sc[...] = jnp.zeros_like(l_sc)
            acc_sc[...] = jnp.zeros_like(acc_sc)

        @pl.when(j <= i)
        def _():
            s = _scores(q_ref[...], k_ref[...], cq_ref[...], ck_ref[...], h, i, j, tq, scale)
            m_prev = m_sc[...]
            m_new = jnp.maximum(m_prev, jnp.max(s, axis=1, keepdims=True))
            p = jnp.exp(s - m_new)
            alpha = jnp.exp(m_prev - m_new)
            l_sc[...] = alpha * l_sc[...] + jnp.sum(p, axis=1, keepdims=True)
            p_hi = p.astype(BF16)
            p_lo = (p - p_hi.astype(F32)).astype(BF16)
            vv = v_ref[...]
            pv = (jnp.dot(p_hi, vv, preferred_element_type=F32)
                  + jnp.dot(p_lo, vv, preferred_element_type=F32))
            acc_sc[...] = alpha * acc_sc[...] + pv
            m_sc[...] = m_new

        @pl.when(j == i)
        def _():
            o = acc_sc[...] / l_sc[...]
            o_ref[...] = o.astype(o_ref.dtype)
            o32_ref[...] = o
            lse_ref[0] = m_sc[...] + jnp.log(l_sc[...])

    qspec = pl.BlockSpec((tq, HEAD_DIM), lambda i, h, j: (i, h))
    kspec = pl.BlockSpec((tq, HEAD_DIM), lambda i, h, j: (jnp.minimum(j, i), h))
    return pl.pallas_call(
        body, name=name,
        out_shape=[jax.ShapeDtypeStruct((n_rows, d), BF16), jax.ShapeDtypeStruct((n_rows, d), F32),
                   jax.ShapeDtypeStruct((n_heads, n_rows, 1), F32)],
        grid=(nq, n_heads, nq),
        in_specs=[qspec, kspec, kspec,
                  pl.BlockSpec((n_heads, tq), lambda i, h, j: (0, i)),
                  pl.BlockSpec((n_heads, tq), lambda i, h, j: (0, jnp.minimum(j, i)))],
        out_specs=[qspec, qspec, pl.BlockSpec((1, tq, 1), lambda i, h, j: (h, i, 0))],
        scratch_shapes=[pltpu.VMEM((tq, 1), F32), pltpu.VMEM((tq, 1), F32), pltpu.VMEM((tq, HEAD_DIM), F32)],
        compiler_params=_params(("parallel", "parallel", "arbitrary")),
    )(q, k, v, ct, ct)


def attn_delta(do, o, name):
    n_rows, d = o.shape
    n_heads = d // HEAD_DIM
    tq = min(512, n_rows)

    def body(do_ref, o_ref, out_ref):
        out_ref[0] = jnp.sum(do_ref[...].astype(F32) * o_ref[...].astype(F32), axis=1, keepdims=True)

    spec = pl.BlockSpec((tq, HEAD_DIM), lambda i, h: (i, h))
    return pl.pallas_call(
        body, name=name, out_shape=jax.ShapeDtypeStruct((n_heads, n_rows, 1), F32),
        grid=(n_rows // tq, n_heads), in_specs=[spec, spec],
        out_specs=pl.BlockSpec((1, tq, 1), lambda i, h: (h, i, 0)),
        compiler_params=_params(("parallel", "parallel")),
    )(do, o)


def flash_bwd_dq(q, k, v, do, ct, lse, delta, name):
    n_rows, d = q.shape
    n_heads = d // HEAD_DIM
    tq = min(512, n_rows)
    nq = n_rows // tq
    scale = HEAD_DIM ** -0.5

    def body(q_ref, k_ref, v_ref, do_ref, cq_ref, ck_ref, lse_ref, dl_ref, dq_ref, acc_sc):
        i, h, j = pl.program_id(0), pl.program_id(1), pl.program_id(2)

        @pl.when(j == 0)
        def _():
            acc_sc[...] = jnp.zeros_like(acc_sc)

        @pl.when(j <= i)
        def _():
            s = _scores(q_ref[...], k_ref[...], cq_ref[...], ck_ref[...], h, i, j, tq, scale)
            p = jnp.exp(s - lse_ref[0])
            dp = lax.dot_general(do_ref[...], v_ref[...], (((1,), (1,)), ((), ())),
                                 preferred_element_type=F32)
            ds = p * (dp - dl_ref[0])
            acc_sc[...] += jnp.dot(ds.astype(BF16), k_ref[...], preferred_element_type=F32)

        @pl.when(j == i)
        def _():
            dq_ref[...] = (acc_sc[...] * scale).astype(dq_ref.dtype)

    qspec = pl.BlockSpec((tq, HEAD_DIM), lambda i, h, j: (i, h))
    kspec = pl.BlockSpec((tq, HEAD_DIM), lambda i, h, j: (jnp.minimum(j, i), h))
    vec = pl.BlockSpec((1, tq, 1), lambda i, h, j: (h, i, 0))
    return pl.pallas_call(
        body, name=name, out_shape=jax.ShapeDtypeStruct((n_rows, d), BF16),
        grid=(nq, n_heads, nq),
        in_specs=[qspec, kspec, kspec, qspec,
                  pl.BlockSpec((n_heads, tq), lambda i, h, j: (0, i)),
                  pl.BlockSpec((n_heads, tq), lambda i, h, j: (0, jnp.minimum(j, i))),
                  vec, vec],
        out_specs=qspec,
        scratch_shapes=[pltpu.VMEM((tq, HEAD_DIM), F32)],
        compiler_params=_params(("parallel", "parallel", "arbitrary")),
    )(q, k, v, do, ct, ct, lse, delta)


def flash_bwd_dkv(q, k, v, do, ct, lse, delta, name):
    n_rows, d = q.shape
    n_heads = d // HEAD_DIM
    tq = min(512, n_rows)
    nq = n_rows // tq
    scale = HEAD_DIM ** -0.5

    def body(q_ref, k_ref, v_ref, do_ref, cq_ref, ck_ref, lse_ref, dl_ref, dk_ref, dv_ref, dc_ref,
             dk_sc, dv_sc):
        j, h, i = pl.program_id(0), pl.program_id(1), pl.program_id(2)

        @pl.when((h == 0) & (i == 0))
        def _():
            dc_ref[...] = jnp.zeros_like(dc_ref)

        @pl.when(i == 0)
        def _():
            dk_sc[...] = jnp.zeros_like(dk_sc)
            dv_sc[...] = jnp.zeros_like(dv_sc)

        @pl.when(i >= j)
        def _():
            s = _scores(q_ref[...], k_ref[...], cq_ref[...], ck_ref[...], h, i, j, tq, scale)
            p = jnp.exp(s - lse_ref[0])
            do = do_ref[...]
            dp = lax.dot_general(do, v_ref[...], (((1,), (1,)), ((), ())), preferred_element_type=F32)
            ds = p * (dp - dl_ref[0])
            dv_sc[...] += lax.dot_general(p.astype(BF16), do, (((0,), (0,)), ((), ())),
                                          preferred_element_type=F32)
            dk_sc[...] += lax.dot_general(ds.astype(BF16), q_ref[...], (((0,), (0,)), ((), ())),
                                          preferred_element_type=F32)
            col = -jnp.sum(ds, axis=0, keepdims=True)
            row = lax.broadcasted_iota(jnp.int32, dc_ref.shape, 0)
            dc_ref[...] += jnp.where(row == h, col, 0.0)

        @pl.when(i == nq - 1)
        def _():
            dk_ref[...] = dk_sc[...] * scale
            dv_ref[...] = dv_sc[...]

    kspec = pl.BlockSpec((tq, HEAD_DIM), lambda j, h, i: (j, h))
    qspec = pl.BlockSpec((tq, HEAD_DIM), lambda j, h, i: (jnp.maximum(i, j), h))
    vec = pl.BlockSpec((1, tq, 1), lambda j, h, i: (h, jnp.maximum(i, j), 0))
    return pl.pallas_call(
        body, name=name,
        out_shape=[jax.ShapeDtypeStruct((n_rows, d), F32), jax.ShapeDtypeStruct((n_rows, d), F32),
                   jax.ShapeDtypeStruct((n_heads, n_rows), F32)],
        grid=(nq, n_heads, nq),
        in_specs=[qspec, kspec, kspec, qspec,
                  pl.BlockSpec((n_heads, tq), lambda j, h, i: (0, jnp.maximum(i, j))),
                  pl.BlockSpec((n_heads, tq), lambda j, h, i: (0, j)),
                  vec, vec],
        out_specs=[kspec, kspec, pl.BlockSpec((n_heads, tq), lambda j, h, i: (0, j))],
        scratch_shapes=[pltpu.VMEM((tq, HEAD_DIM), F32), pltpu.VMEM((tq, HEAD_DIM), F32)],
        compiler_params=_params(("arbitrary", "arbitrary", "arbitrary")),
    )(q, k, v, do, ct, ct, lse, delta)


S5_PARAM_SHAPES = [(1, S5_2P)] * 3 + [(S5_2P, 1)] * 3 + [(S5_GROUP, S5_2P)] * 4 + [(S5_2P, S5_TC)] * 4 + [(S5_TC, 1)]


def _tile_rows(a, reps):
    return jnp.concatenate([a] * reps, axis=0)


def _s5_build(lrl, lil, lsl, lrc, lic, lsc, cri, cir, bt2, bt2s, brit, birt, ctt, ctts):
    first = lax.broadcasted_iota(jnp.int32, (1, S5_2P), 1) < S5_STATE
    sgn = jnp.where(first, -1.0, 1.0)
    dt_l = jnp.exp(lsl)
    a_l, b_l = lrl * dt_l, lil * dt_l
    mag1 = jnp.exp(a_l)
    lbr, lbi = mag1 * jnp.cos(b_l), mag1 * jnp.sin(b_l)
    n = lrl * lrl + lil * lil
    kr = ((lbr - 1.0) * lrl + lbi * lil) / n
    ki = (lbi * lrl - (lbr - 1.0) * lil) / n
    blk = lax.shift_right_logical(lax.broadcasted_iota(jnp.int32, (S5_TC, 1), 0), 4).astype(F32)
    e_end = (S5_CHUNK - 1.0) - blk
    mag_e = jnp.exp(e_end * a_l)
    pr, pi = mag_e * jnp.cos(e_end * b_l), mag_e * jnp.sin(e_end * b_l)
    cr, ci = pr * kr - pi * ki, pr * ki + pi * kr
    w_end = cr * _tile_rows(bt2, S5_CHUNK) + (sgn * ci) * _tile_rows(bt2s, S5_CHUNK)
    mag_j = jnp.exp(blk * a_l)
    qr, qi = mag_j * jnp.cos(blk * b_l), mag_j * jnp.sin(blk * b_l)
    xmat = qr * _tile_rows(cri, S5_CHUNK) + (sgn * qi) * _tile_rows(cir, S5_CHUNK)
    dt_c = jnp.exp(lsc)
    a_c, b_c = lrc * dt_c, lic * dt_c
    m1c = jnp.exp(a_c)
    lbrc, lbic = m1c * jnp.cos(b_c), m1c * jnp.sin(b_c)
    n_c = lrc * lrc + lic * lic
    krc = ((lbrc - 1.0) * lrc + lbic * lic) / n_c
    kic = (lbic * lrc - (lbrc - 1.0) * lic) / n_c
    top = lax.broadcasted_iota(jnp.int32, (S5_2P, 1), 0) < S5_STATE
    rhs = jnp.where(top, krc, -krc) * brit + (-kic) * birt
    z = jnp.dot(xmat, rhs, precision=HI, preferred_element_type=F32)
    e_out = (lax.shift_right_logical(lax.broadcasted_iota(jnp.int32, (1, S5_TC), 1), 4) + 1).astype(F32)
    mag_o = jnp.exp(a_c * e_out)
    o_r, o_i = mag_o * jnp.cos(b_c * e_out), mag_o * jnp.sin(b_c * e_out)
    w_out = jnp.where(top, o_r, -o_r) * ctt + (-o_i) * ctts
    m16 = jnp.exp(float(S5_CHUNK) * a_l)
    a1 = m16 * jnp.cos(float(S5_CHUNK) * b_l)
    a2 = sgn * (m16 * jnp.sin(float(S5_CHUNK) * b_l))
    return z, w_end, w_out, a1, a2


def _lane_block_bit(b):
    blk = lax.shift_right_logical(lax.broadcasted_iota(jnp.int32, (S5_TC, S5_TC), 1), 4)
    return (lax.shift_right_logical(blk, b) & 1) == 1


def _toeplitz(z, dcol):
    for b in range(4):
        s = S5_GROUP << b
        moved = jnp.concatenate([jnp.zeros((s, S5_TC), F32), z[:S5_TC - s]], axis=0)
        z = jnp.where(_lane_block_bit(b), moved, z)
    r = lax.broadcasted_iota(jnp.int32, (S5_TC, S5_TC), 0)
    c = lax.broadcasted_iota(jnp.int32, (S5_TC, S5_TC), 1)
    return z + jnp.where(r == c, dcol, 0.0)


def _toeplitz_t(dt):
    r = lax.broadcasted_iota(jnp.int32, (S5_TC, S5_TC), 0)
    c = lax.broadcasted_iota(jnp.int32, (S5_TC, S5_TC), 1)
    ddcol = jnp.sum(jnp.where(r == c, dt, 0.0), axis=1, keepdims=True)
    for b in range(4):
        s = S5_GROUP << b
        moved = jnp.concatenate([dt[s:], jnp.zeros((s, S5_TC), F32)], axis=0)
        dt = jnp.where(_lane_block_bit(b), moved, dt)
    return dt, ddcol


def _s5_scan(e_ref, sin_ref, a1s, a2s, nk, gb):
    def step(k, carry):
        new = []
        for g in range(gb):
            s = carry[g]
            sin_ref[g, pl.ds(k, 1), :] = s
            new.append(a1s[g] * s + a2s[g] * pltpu.roll(s, S5_STATE, 1) + e_ref[g, pl.ds(k, 1), :])
        return tuple(new)

    lax.fori_loop(0, nk, step, tuple(jnp.zeros((1, S5_2P), F32) for _ in range(gb)))


def _s5_scan_t(dsin_ref, de_ref, a1s, a2s, nk, gb):
    def step(t, carry):
        k = nk - 1 - t
        new = []
        for g in range(gb):
            gk = carry[g]
            de_ref[g, pl.ds(k, 1), :] = gk
            new.append(dsin_ref[g, pl.ds(k, 1), :] + a1s[g] * gk - a2s[g] * pltpu.roll(gk, S5_STATE, 1))
        return tuple(new)

    lax.fori_loop(0, nk, step, tuple(jnp.zeros((1, S5_2P), F32) for _ in range(gb)))


def _dot_hi(a, b, dims):
    return lax.dot_general(a, b, (dims, ((), ())), precision=HI, preferred_element_type=F32)


_NN = ((1,), (0,))
_NT = ((1,), (1,))
_TN = ((0,), (0,))


def _s5_specs(gb):
    return [pl.BlockSpec((gb, *s), lambda i: (i, 0, 0)) for s in S5_PARAM_SHAPES]


def s5_fwd(u, prm, name):
    n_groups, nk, _ = u.shape
    gb = min(4, n_groups)
    n_prm = len(S5_PARAM_SHAPES)

    def body(*refs):
        u_ref = refs[0]
        p_refs = refs[1:1 + n_prm]
        o_ref = refs[1 + n_prm]
        e_sc, sin_sc, toep_sc, wout_sc = refs[2 + n_prm:]
        a1s, a2s = [], []
        for g in range(gb):
            z, w_end, w_out, a1, a2 = _s5_build(*[r[g] for r in p_refs[:-1]])
            toep_sc[g] = _toeplitz(z, p_refs[-1][g])
            wout_sc[g] = w_out
            a1s.append(a1)
            a2s.append(a2)
            e_sc[g] = _dot_hi(u_ref[g], w_end, _NN)
        _s5_scan(e_sc, sin_sc, a1s, a2s, nk, gb)
        for g in range(gb):
            y = _dot_hi(u_ref[g], toep_sc[g], _NT) + _dot_hi(sin_sc[g], wout_sc[g], _NN)
            o_ref[g] = _gelu(y).astype(o_ref.dtype)

    blk = pl.BlockSpec((gb, nk, S5_TC), lambda i: (i, 0, 0))
    return pl.pallas_call(
        body, name=name, out_shape=jax.ShapeDtypeStruct(u.shape, BF16), grid=(n_groups // gb,),
        in_specs=[blk] + _s5_specs(gb), out_specs=blk,
        scratch_shapes=[pltpu.VMEM((gb, nk, S5_2P), F32), pltpu.VMEM((gb, nk, S5_2P), F32),
                        pltpu.VMEM((gb, S5_TC, S5_TC), F32), pltpu.VMEM((gb, S5_2P, S5_TC), F32)],
        compiler_params=_params(("arbitrary",)),
    )(u, *prm)


def s5_bwd(u, dyg, prm, name):
    n_groups, nk, _ = u.shape
    gb = min(2, n_groups)
    n_prm = len(S5_PARAM_SHAPES)

    def body(*refs):
        u_ref, dyg_ref = refs[0], refs[1]
        p_refs = refs[2:2 + n_prm]
        du_ref = refs[2 + n_prm]
        dp_refs = refs[3 + n_prm:3 + 2 * n_prm]
        e_sc, sin_sc, dsin_sc, dy_sc, toep_sc, wend_sc, wout_sc = refs[3 + 2 * n_prm:]
        a1s, a2s = [], []
        for g in range(gb):
            z, w_end, w_out, a1, a2 = _s5_build(*[r[g] for r in p_refs[:-1]])
            toep_sc[g] = _toeplitz(z, p_refs[-1][g])
            wend_sc[g] = w_end
            wout_sc[g] = w_out
            a1s.append(a1)
            a2s.append(a2)
            e_sc[g] = _dot_hi(u_ref[g], w_end, _NN)
        _s5_scan(e_sc, sin_sc, a1s, a2s, nk, gb)
        for g in range(gb):
            y = _dot_hi(u_ref[g], toep_sc[g], _NT) + _dot_hi(sin_sc[g], wout_sc[g], _NN)
            dy = dyg_ref[g] * _gelu_grad(y)
            dy_sc[g] = dy
            dsin_sc[g] = _dot_hi(dy, wout_sc[g], _NT)
        _s5_scan_t(dsin_sc, e_sc, a1s, a2s, nk, gb)
        for g in range(gb):
            ug, dy, de, sin = u_ref[g], dy_sc[g], e_sc[g], sin_sc[g]
            du_ref[g] = _dot_hi(dy, toep_sc[g], _NN) + _dot_hi(de, wend_sc[g], _NT)
            dz, ddcol = _toeplitz_t(_dot_hi(dy, ug, _TN))
            da1 = jnp.sum(de * sin, axis=0, keepdims=True)
            da2 = jnp.sum(de * pltpu.roll(sin, S5_STATE, 1), axis=0, keepdims=True)
            _, vjp = jax.vjp(_s5_build, *[r[g] for r in p_refs[:-1]])
            grads = vjp((dz, _dot_hi(ug, de, _TN), _dot_hi(sin, dy, _TN), da1, da2))
            for r, v in zip(dp_refs[:-1], grads):
                r[g] = v
            dp_refs[-1][g] = ddcol

    blk = pl.BlockSpec((gb, nk, S5_TC), lambda i: (i, 0, 0))
    res = pl.pallas_call(
        body, name=name,
        out_shape=[jax.ShapeDtypeStruct(u.shape, F32)]
        + [jax.ShapeDtypeStruct((n_groups, *s), F32) for s in S5_PARAM_SHAPES],
        grid=(n_groups // gb,),
        in_specs=[blk, blk] + _s5_specs(gb), out_specs=[blk] + _s5_specs(gb),
        scratch_shapes=[pltpu.VMEM((gb, nk, S5_2P), F32), pltpu.VMEM((gb, nk, S5_2P), F32),
                        pltpu.VMEM((gb, nk, S5_2P), F32), pltpu.VMEM((gb, nk, S5_TC), F32),
                        pltpu.VMEM((gb, S5_TC, S5_TC), F32), pltpu.VMEM((gb, S5_TC, S5_2P), F32),
                        pltpu.VMEM((gb, S5_2P, S5_TC), F32)],
        compiler_params=_params(("arbitrary",)),
    )(u, dyg, *prm)
    return res[0], res[1:]


def s5_prepare(lam_re, lam_im, log_step, b_re, b_im, c_re, c_im, d):
    g = lam_re.shape[0]
    lane = lambda a: a.reshape(g, 1, -1)
    col = lambda a: a.reshape(g, -1, 1)
    ls = jnp.broadcast_to(log_step[:, None], (g, S5_2P))
    brt = jnp.swapaxes(b_re, 1, 2)
    bit = jnp.swapaxes(b_im, 1, 2)
    crt = jnp.swapaxes(c_re, 1, 2)
    cit = jnp.swapaxes(c_im, 1, 2)
    cat = jnp.concatenate
    return [
        lane(cat([lam_re, lam_re], 1)), lane(cat([lam_im, lam_im], 1)), lane(ls),
        col(cat([lam_re, lam_re], 1)), col(cat([lam_im, lam_im], 1)), col(ls),
        cat([c_re, c_im], 2), cat([c_im, c_re], 2),
        cat([brt, bit], 2), cat([bit, brt], 2),
        jnp.tile(cat([b_re, b_im], 1), (1, 1, S5_CHUNK)), jnp.tile(cat([b_im, b_re], 1), (1, 1, S5_CHUNK)),
        jnp.tile(cat([crt, cit], 1), (1, 1, S5_CHUNK)), jnp.tile(cat([cit, crt], 1), (1, 1, S5_CHUNK)),
        col(jnp.tile(d, (1, S5_CHUNK))),
    ]


def s5_param_grads(prm, grads):
    _, vjp = jax.vjp(s5_prepare, *prm)
    return vjp(list(grads))


def s5_to_groups(a):
    n_rows, d = a.shape
    g, nk = d // S5_GROUP, n_rows // S5_CHUNK
    return a.reshape(nk, S5_CHUNK, g, S5_GROUP).transpose(2, 0, 1, 3).reshape(g, nk, S5_TC)


def s5_from_groups(a):
    g, nk, _ = a.shape
    return a.reshape(g, nk, S5_CHUNK, S5_GROUP).transpose(1, 2, 0, 3).reshape(nk * S5_CHUNK, g * S5_GROUP)


def _adamw_math(w, g, m, v):
    m = ADAM_B1 * m + (1.0 - ADAM_B1) * g
    v = ADAM_B2 * v + (1.0 - ADAM_B2) * (g * g)
    m_hat = m / (1.0 - ADAM_B1 ** ADAM_STEP)
    v_hat = v / (1.0 - ADAM_B2 ** ADAM_STEP)
    delta = -ADAM_LR * (m_hat / (jnp.sqrt(v_hat) + ADAM_EPS) + ADAM_WD * w)
    return delta, m, v


def adamw_sum8(parts, w, m, v, name):
    r, c = w.shape
    tr = _pick(r, (128, 64, 32, 16, 8))

    def body(p_ref, w_ref, m_ref, v_ref, g_out, d_out, m_out, v_out):
        g = p_ref[0].astype(F32)
        for s in range(1, N_DEV):
            g = g + p_ref[s].astype(F32)
        delta, m_new, v_new = _adamw_math(w_ref[...], g, m_ref[...], v_ref[...])
        g_out[...] = g
        d_out[...] = delta
        m_out[...] = m_new
        v_out[...] = v_new

    spec = pl.BlockSpec((tr, c), lambda i: (i, 0))
    return pl.pallas_call(
        body, name=name, out_shape=[jax.ShapeDtypeStruct((r, c), F32)] * 4, grid=(r // tr,),
        in_specs=[pl.BlockSpec((N_DEV, tr, c), lambda i: (0, i, 0)), spec, spec, spec],
        out_specs=[spec] * 4, compiler_params=_params(("parallel",)),
    )(parts, w, m, v)


def sum8(parts, name):
    r, c = parts.shape[1:]
    tr = _pick(r, (256, 128, 64, 32, 16, 8))

    def body(p_ref, o_ref):
        g = p_ref[0]
        for s in range(1, N_DEV):
            g = g + p_ref[s]
        o_ref[...] = g

    return pl.pallas_call(
        body, name=name, out_shape=jax.ShapeDtypeStruct((r, c), F32), grid=(r // tr,),
        in_specs=[pl.BlockSpec((N_DEV, tr, c), lambda i: (0, i, 0))],
        out_specs=pl.BlockSpec((tr, c), lambda i: (i, 0)), compiler_params=_params(("parallel",)),
    )(parts)


def adamw_plain(g, w, m, v, name):
    r, c = w.shape
    tr = _pick(r, (256, 128, 64, 32, 16, 8))

    def body(g_ref, w_ref, m_ref, v_ref, d_out, m_out, v_out):
        delta, m_new, v_new = _adamw_math(w_ref[...], g_ref[...], m_ref[...], v_ref[...])
        d_out[...] = delta
        m_out[...] = m_new
        v_out[...] = v_new

    spec = pl.BlockSpec((tr, c), lambda i: (i, 0))
    return pl.pallas_call(
        body, name=name, out_shape=[jax.ShapeDtypeStruct((r, c), F32)] * 3, grid=(r // tr,),
        in_specs=[spec] * 4, out_specs=[spec] * 3, compiler_params=_params(("parallel",)),
    )(g, w, m, v)


def _pack(arrays):
    pieces = []
    for a in arrays:
        flat = a.reshape(-1).astype(F32)
        pad = (-flat.shape[0]) % 1024
        pieces.append(jnp.pad(flat, (0, pad)))
    return jnp.concatenate(pieces).reshape(-1, 128)


def _unpack(buf, shapes):
    flat = buf.reshape(-1)
    out, off = [], 0
    for s in shapes:
        n = math.prod(s)
        out.append(flat[off:off + n].reshape(s))
        off += n + ((-n) % 1024)
    return out


WEIGHT_NAMES = ['a_norm', 'a_w_in', 'a_lambda_re', 'a_lambda_im', 'a_log_step', 'a_b_re', 'a_b_im', 'a_c_re',
                'a_c_im', 'a_d', 'a_w_glu', 'kv_norm', 'w_k', 'w_v', 'w_f', 'b_f', 'b_norm', 'b_w_q', 'b_w_o',
                'ffn_norm', 'ffn_w_up', 'ffn_conv_w', 'ffn_conv_b', 'ffn_w_down', 'final_norm']
BIG = {'a_w_in': 0, 'a_w_glu': 1, 'w_k': 0, 'w_v': 0, 'b_w_q': 0, 'b_w_o': 0, 'ffn_w_up': 1, 'ffn_w_down': 0}
SMALL_SHARDED = {'a_norm': 1, 'w_f': 0, 'ffn_conv_w': 2}


def _ffn_fwd(x, gain, w_up, conv_w, conv_b, w_down, tag):
    h = norm_fwd(x, gain, f"norm_fwd_{tag}")
    up = matmul(h, w_up, name=f"mm_up_{tag}")
    a = convglu_fwd(up, conv_w, conv_b, f"convglu_fwd_{tag}")
    x_out = matmul(a, w_down, add=x, name=f"mm_down_{tag}")
    return x_out, (x, h, up, a)


def _ffn_bwd(dx, saved, gain, w_up, conv_w, conv_b, w_down, tag):
    x, h, up, a = saved
    dxb = dx.astype(BF16)
    dw_down = matmul(a, dxb, ta=True, out_dtype=BF16, name=f"mm_dwdown_{tag}")
    da = matmul(dxb, w_down, tb=True, name=f"mm_da_{tag}")
    dup, dconv_w, dconv_b = convglu_bwd(up, da, conv_w, conv_b, f"convglu_bwd_{tag}")
    dw_up = matmul(h, dup, ta=True, out_dtype=BF16, name=f"mm_dwup_{tag}")
    dh = matmul(dup, w_up, tb=True, name=f"mm_dh_{tag}")
    dx_in, dgain = norm_bwd(x, gain, dh, dx, f"norm_bwd_{tag}")
    return dx_in, dict(norm=dgain, w_up=dw_up, conv_w=dconv_w, conv_b=dconv_b, w_down=dw_down)


def kernel(x, a_norm, a_w_in, a_lambda_re, a_lambda_im, a_log_step, a_b_re, a_b_im, a_c_re, a_c_im, a_d, a_w_glu, kv_norm, w_k, w_v, w_f, b_f, b_norm, b_w_q, b_w_o, ffn_norm, ffn_w_up, ffn_conv_w, ffn_conv_b, ffn_w_down, final_norm, loss_target, m_a_norm, m_a_w_in, m_a_lambda_re, m_a_lambda_im, m_a_log_step, m_a_b_re, m_a_b_im, m_a_c_re, m_a_c_im, m_a_d, m_a_w_glu, m_kv_norm, m_w_k, m_w_v, m_w_f, m_b_f, m_b_norm, m_b_w_q, m_b_w_o, m_ffn_norm, m_ffn_w_up, m_ffn_conv_w, m_ffn_conv_b, m_ffn_w_down, m_final_norm, v_a_norm, v_a_w_in, v_a_lambda_re, v_a_lambda_im, v_a_log_step, v_a_b_re, v_a_b_im, v_a_c_re, v_a_c_im, v_a_d, v_a_w_glu, v_kv_norm, v_w_k, v_w_v, v_w_f, v_b_f, v_b_norm, v_b_w_q, v_b_w_o, v_ffn_norm, v_ffn_w_up, v_ffn_conv_w, v_ffn_conv_b, v_ffn_w_down, v_final_norm):
    args = locals()
    w = {n: args[n] for n in WEIGHT_NAMES}
    mom = {n: args["m_" + n] for n in WEIGHT_NAMES}
    var = {n: args["v_" + n] for n in WEIGHT_NAMES}
    x0 = x[0]
    target = loss_target[0]
    n_rows, d = x0.shape
    n_heads = d // HEAD_DIM
    n_a = a_w_in.shape[0]
    n_b = b_w_q.shape[0]
    me = _my_index()

    def gathered(name, layer=None):
        ws = w[name] if layer is None else w[name][layer]
        tag = name if layer is None else f"{name}{layer}"
        return all_gather(ws.astype(BF16), BIG[name], f"ag_{tag}")

    small_local = [w[n] for n in SMALL_SHARDED]
    small_full = all_gather(_pack(small_local)[None], 0, "ag_small")
    per_dev = [_unpack(small_full[s], [a.shape for a in small_local]) for s in range(N_DEV)]
    full_small = {n: jnp.concatenate([per_dev[s][i] for s in range(N_DEV)], axis=SMALL_SHARDED[n])
                  for i, n in enumerate(SMALL_SHARDED)}
    a_norm_f, w_f_f, conv_w_f = full_small['a_norm'], full_small['w_f'], full_small['ffn_conv_w']
    w_f_pad = jnp.pad(w_f_f, ((0, 0), (0, 128 - n_heads))).astype(BF16)
    b_f_pad = jnp.pad(b_f, (0, 128 - n_heads)).reshape(1, 128)

    grads = {}
    big_grads = {}

    xs = x0
    saved_a, saved_b, saved_ffn = [], [], []
    weights_a, weights_b, weights_ffn = [], [], []
    for i in range(n_a):
        w_in = gathered('a_w_in', i)
        w_glu = gathered('a_w_glu', i)
        prm = s5_prepare(a_lambda_re[i], a_lambda_im[i], a_log_step[i], a_b_re[i], a_b_im[i],
                         a_c_re[i], a_c_im[i], a_d[i])
        gain = a_norm_f[i].reshape(1, d)
        h = norm_fwd(xs, gain, f"norm_fwd_a{i}")
        u = s5_to_groups(matmul(h, w_in, name=f"mm_win_{i}"))
        yg = s5_from_groups(s5_fwd(u, prm, f"s5_fwd_{i}"))
        z = matmul(yg, w_glu, name=f"mm_wglu_{i}")
        x_mid = glu_fwd(z, xs, f"glu_fwd_{i}")
        saved_a.append((xs, h, u, yg, z))
        weights_a.append((w_in, w_glu, prm, gain))
        fw = (ffn_norm[i].reshape(1, d), gathered('ffn_w_up', i), conv_w_f[i], ffn_conv_b[i].reshape(1, -1),
              gathered('ffn_w_down', i))
        xs, sv = _ffn_fwd(x_mid, *fw, f"f{i}")
        saved_ffn.append(sv)
        weights_ffn.append(fw)

    x_kv = xs
    kv_gain = kv_norm.reshape(1, d)
    wk_f, wv_f = gathered('w_k'), gathered('w_v')
    h_kv = norm_fwd(x_kv, kv_gain, "norm_fwd_kv")
    k = matmul(h_kv, wk_f, out_dtype=BF16, name="mm_wk")
    v = matmul(h_kv, wv_f, out_dtype=BF16, name="mm_wv")
    zf = matmul(h_kv, w_f_pad, name="mm_wf")
    ct = fgate_fwd(zf, b_f_pad, n_heads, "fgate_fwd")

    for j in range(n_b):
        li = n_a + j
        w_q, w_o = gathered('b_w_q', j), gathered('b_w_o', j)
        gain = b_norm[j].reshape(1, d)
        h = norm_fwd(xs, gain, f"norm_fwd_b{j}")
        q = matmul(h, w_q, out_dtype=BF16, name=f"mm_wq_{j}")
        o, o32, lse = flash_fwd(q, k, v, ct, f"flash_fwd_{j}")
        x_mid = matmul(o, w_o, add=xs, name=f"mm_wo_{j}")
        saved_b.append((xs, h, q, o, o32, lse))
        weights_b.append((w_q, w_o, gain))
        fw = (ffn_norm[li].reshape(1, d), gathered('ffn_w_up', li), conv_w_f[li], ffn_conv_b[li].reshape(1, -1),
              gathered('ffn_w_down', li))
        xs, sv = _ffn_fwd(x_mid, *fw, f"f{li}")
        saved_ffn.append(sv)
        weights_ffn.append(fw)

    dx, d_final, sq = loss_and_grad(xs, final_norm.reshape(1, d), target, "loss")
    loss = lax.psum(0.5 * jnp.sum(sq) / d, MESH_AXES)
    grads['final_norm'] = d_final.reshape(d)

    ffn_g = [None] * (n_a + n_b)
    b_g = [None] * n_b
    dk_tot = dv_tot = dc_tot = None
    for j in reversed(range(n_b)):
        li = n_a + j
        dx, ffn_g[li] = _ffn_bwd(dx, saved_ffn[li], *weights_ffn[li], f"f{li}")
        xs_in, h, q, o, o32, lse = saved_b[j]
        w_q, w_o, gain = weights_b[j]
        dxb = dx.astype(BF16)
        dw_o = matmul(o, dxb, ta=True, out_dtype=BF16, name=f"mm_dwo_{j}")
        do = matmul(dxb, w_o, tb=True, out_dtype=BF16, name=f"mm_do_{j}")
        delta = attn_delta(do, o32, f"attn_delta_{j}")
        dq = flash_bwd_dq(q, k, v, do, ct, lse, delta, f"flash_dq_{j}")
        dk_j, dv_j, dc_j = flash_bwd_dkv(q, k, v, do, ct, lse, delta, f"flash_dkv_{j}")
        if dk_tot is None:
            dk_tot, dv_tot, dc_tot = [dk_j], [dv_j], [dc_j]
        else:
            dk_tot.append(dk_j), dv_tot.append(dv_j), dc_tot.append(dc_j)
        dw_q = matmul(h, dq, ta=True, out_dtype=BF16, name=f"mm_dwq_{j}")
        dh = matmul(dq, w_q, tb=True, name=f"mm_dhq_{j}")
        dx, dgain = norm_bwd(xs_in, gain, dh, dx, f"norm_bwd_b{j}")
        b_g[j] = dict(norm=dgain, w_q=dw_q, w_o=dw_o)

    def total(parts, tag):
        if len(parts) == 1:
            return parts[0].astype(BF16)
        acc = parts[0]
        for n_, p_ in enumerate(parts[1:-1]):
            acc = acc + p_
        return add_cast(acc, parts[-1], f"add_{tag}")

    dk_b, dv_b = total(dk_tot, "dk"), total(dv_tot, "dv")
    dc_a = dc_tot[0]
    dc_b = dc_tot[1] if len(dc_tot) > 1 else jnp.zeros_like(dc_a)
    for extra in dc_tot[2:]:
        dc_b = dc_b + extra
    dzf, db_f = fgate_bwd(dc_a, dc_b, zf, b_f_pad, "fgate_bwd")
    big_grads[('w_k', None)] = matmul(h_kv, dk_b, ta=True, out_dtype=BF16, name="mm_dwk")
    big_grads[('w_v', None)] = matmul(h_kv, dv_b, ta=True, out_dtype=BF16, name="mm_dwv")
    dw_f = matmul(h_kv, dzf, ta=True, name="mm_dwf")[:, :n_heads]
    dh_kv = matmul(dk_b, wk_f, tb=True, name="mm_dhk")
    dh_kv = matmul(dv_b, wv_f, tb=True, add=dh_kv, name="mm_dhv")
    dh_kv = matmul(dzf, w_f_pad, tb=True, add=dh_kv, name="mm_dhf")
    dx, d_kv_gain = norm_bwd(x_kv, kv_gain, dh_kv, dx, "norm_bwd_kv")
    grads['kv_norm'] = d_kv_gain.reshape(d)
    grads['b_f'] = db_f[0, :n_heads]

    a_g = [None] * n_a
    for i in reversed(range(n_a)):
        dx, ffn_g[i] = _ffn_bwd(dx, saved_ffn[i], *weights_ffn[i], f"f{i}")
        xs_in, h, u, yg, z = saved_a[i]
        w_in, w_glu, prm, gain = weights_a[i]
        dz = glu_bwd(z, dx, f"glu_bwd_{i}")
        dw_glu = matmul(yg, dz, ta=True, out_dtype=BF16, name=f"mm_dwglu_{i}")
        dyg = s5_to_groups(matmul(dz, w_glu, tb=True, name=f"mm_dyg_{i}"))
        du_g, dprm = s5_bwd(u, dyg, prm, f"s5_bwd_{i}")
        du = s5_from_groups(du_g).astype(BF16)
        dw_in = matmul(h, du, ta=True, out_dtype=BF16, name=f"mm_dwin_{i}")
        dh = matmul(du, w_in, tb=True, name=f"mm_dhin_{i}")
        dx, dgain = norm_bwd(xs_in, gain, dh, dx, f"norm_bwd_a{i}")
        a_g[i] = dict(norm=dgain, w_in=dw_in, w_glu=dw_glu, prm=s5_param_grads(
            (a_lambda_re[i], a_lambda_im[i], a_log_step[i], a_b_re[i], a_b_im[i], a_c_re[i], a_c_im[i],
             a_d[i]), dprm))
    grad_x = dx[None]

    for i in range(n_a):
        big_grads[('a_w_in', i)] = a_g[i]['w_in']
        big_grads[('a_w_glu', i)] = a_g[i]['w_glu']
    for j in range(n_b):
        big_grads[('b_w_q', j)] = b_g[j]['w_q']
        big_grads[('b_w_o', j)] = b_g[j]['w_o']
    for li in range(n_a + n_b):
        big_grads[('ffn_w_up', li)] = ffn_g[li]['w_up']
        big_grads[('ffn_w_down', li)] = ffn_g[li]['w_down']
    stack = lambda parts: jnp.stack(parts, axis=0)
    grads['a_norm'] = stack([a_g[i]['norm'].reshape(d) for i in range(n_a)])
    for pi, pname in enumerate(['a_lambda_re', 'a_lambda_im', 'a_log_step', 'a_b_re', 'a_b_im', 'a_c_re',
                                'a_c_im', 'a_d']):
        grads[pname] = stack([a_g[i]['prm'][pi] for i in range(n_a)])
    grads['w_f'] = dw_f
    grads['b_norm'] = stack([b_g[j]['norm'].reshape(d) for j in range(n_b)])
    grads['ffn_norm'] = stack([g_['norm'].reshape(d) for g_ in ffn_g])
    grads['ffn_conv_w'] = stack([g_['conv_w'] for g_ in ffn_g])
    grads['ffn_conv_b'] = stack([g_['conv_b'].reshape(-1) for g_ in ffn_g])

    small_names = [n for n in WEIGHT_NAMES if n not in BIG]
    small_shapes = [grads[n].shape for n in small_names]
    g_parts = all_gather(_pack([grads[n] for n in small_names])[None], 0, "ag_grads_small")
    g_sum = _unpack(sum8(g_parts, "sum_grads_small"), small_shapes)
    g_full = dict(zip(small_names, g_sum))
    g_local = {}
    for n in small_names:
        if n in SMALL_SHARDED:
            ax = SMALL_SHARDED[n]
            size = w[n].shape[ax]
            g_local[n] = lax.dynamic_slice_in_dim(g_full[n], me * size, size, axis=ax)
        else:
            g_local[n] = g_full[n]
    local_shapes = [w[n].shape for n in small_names]
    d_s, m_s, v_s = adamw_plain(_pack([g_local[n] for n in small_names]), _pack([w[n] for n in small_names]),
                                _pack([mom[n] for n in small_names]), _pack([var[n] for n in small_names]),
                                "adamw_small")
    out_g, out_d, out_m, out_v = dict(g_local), {}, {}, {}
    for n, dd, mm, vv in zip(small_names, _unpack(d_s, local_shapes), _unpack(m_s, local_shapes),
                             _unpack(v_s, local_shapes)):
        out_d[n], out_m[n], out_v[n] = dd, mm, vv

    for name, ax in BIG.items():
        layered = w[name].ndim == 3
        layers = range(w[name].shape[0]) if layered else [None]
        res = []
        for layer in layers:
            tag = name if layer is None else f"{name}{layer}"
            parts = scatter_blocks(big_grads[(name, layer)], ax, f"rs_{tag}")
            pick = (lambda a: a) if layer is None else (lambda a: a[layer])
            res.append(adamw_sum8(parts, pick(w[name]), pick(mom[name]), pick(var[name]), f"adamw_{tag}"))
        for dst, idx in ((out_g, 0), (out_d, 1), (out_m, 2), (out_v, 3)):
            dst[name] = stack([r[idx] for r in res]) if layered else res[0][idx]

    return (loss, grad_x, *[out_g[n] for n in WEIGHT_NAMES], *[out_d[n] for n in WEIGHT_NAMES],
            *[out_m[n] for n in WEIGHT_NAMES], *[out_v[n] for n in WEIGHT_NAMES])
```

```python
import functools
import math

import jax
import jax.numpy as jnp
from jax import lax
from jax.experimental import pallas as pl
from jax.experimental.pallas import tpu as pltpu

F32 = jnp.float32
BF16 = jnp.bfloat16
HI = lax.Precision.HIGHEST
MESH_AXES = ("x", "y", "c")
N_DEV = 8
VMEM_LIMIT_BYTES = 48 * 1024 * 1024

NORM_EPS = 1e-6
S5_GROUP = 16
S5_STATE = 64
S5_CHUNK = 16
S5_TC = S5_CHUNK * S5_GROUP
S5_2P = 2 * S5_STATE
HEAD_DIM = 128
MASK_VALUE = -1e30
GELU_C = math.sqrt(2.0 / math.pi)
GELU_A = 0.044715

ADAM_LR = 0.001
ADAM_B1 = 0.9
ADAM_B2 = 0.999
ADAM_EPS = 1e-08
ADAM_WD = 0.01
ADAM_STEP = 10


def _params(sem=None):
    return pltpu.CompilerParams(dimension_semantics=sem, vmem_limit_bytes=VMEM_LIMIT_BYTES)


def _pick(n, cands):
    for c in cands:
        if n % c == 0:
            return c
    return n


def _gelu(x):
    return 0.5 * x * (1.0 + jnp.tanh(GELU_C * (x + GELU_A * x * x * x)))


def _gelu_grad(x):
    t = jnp.tanh(GELU_C * (x + GELU_A * x * x * x))
    return 0.5 * (1.0 + t) + 0.5 * x * (1.0 - t * t) * GELU_C * (1.0 + 3.0 * GELU_A * x * x)


def _my_index():
    return 4 * lax.axis_index("x") + 2 * lax.axis_index("y") + lax.axis_index("c")


def _peer(k):
    x, y, c = lax.axis_index("x"), lax.axis_index("y"), lax.axis_index("c")
    px = 1 - x if (k >> 2) & 1 else x
    py = 1 - y if (k >> 1) & 1 else y
    pc = 1 - c if k & 1 else c
    return (px, py, pc), 4 * px + 2 * py + pc


def _exchange(src_of, dst_of, send_sems, recv_sems, local_sem):
    me = _my_index()
    mine = pltpu.make_async_copy(src_of(me), dst_of(me), local_sem)
    mine.start()
    copies = []
    for k in range(1, N_DEV):
        peer, pidx = _peer(k)
        cp = pltpu.make_async_remote_copy(
            src_ref=src_of(pidx), dst_ref=dst_of(me), send_sem=send_sems.at[k - 1],
            recv_sem=recv_sems.at[k - 1], device_id=peer, device_id_type=pl.DeviceIdType.MESH)
        cp.start()
        copies.append(cp)
    for k in range(1, N_DEV):
        peer, pidx = _peer(k)
        copies[k - 1].wait_send()
        pltpu.make_async_remote_copy(
            src_ref=src_of(pidx), dst_ref=dst_of(pidx), send_sem=send_sems.at[k - 1],
            recv_sem=recv_sems.at[k - 1], device_id=peer, device_id_type=pl.DeviceIdType.MESH).wait_recv()
    mine.wait()


_COMM_SCRATCH = [pltpu.SemaphoreType.DMA((N_DEV - 1,)), pltpu.SemaphoreType.DMA((N_DEV - 1,)),
                 pltpu.SemaphoreType.DMA]
_HBM = pl.BlockSpec(memory_space=pltpu.HBM)


def all_gather(xs, axis, name):
    nd = xs.ndim
    n = xs.shape[axis]
    full = list(xs.shape)
    full[axis] = n * N_DEV

    def body(x_ref, out_ref, send_sems, recv_sems, local_sem):
        x, y, c = lax.axis_index("x"), lax.axis_index("y"), lax.axis_index("c")
        me, sibling = (x, y, c), (x, y, 1 - c)
        chips = [(1 - x, y), (x, 1 - y), (1 - x, 1 - y)]

        def rows(px, py, pc):
            sl = [slice(None)] * nd
            sl[axis] = pl.ds((4 * px + 2 * py + pc) * n, n)
            return out_ref.at[tuple(sl)]

        def copy(k, block, to, src=None):
            return pltpu.make_async_remote_copy(
                src_ref=rows(*block) if src is None else src, dst_ref=rows(*block),
                send_sem=send_sems.at[k], recv_sem=recv_sems.at[k], device_id=to,
                device_id_type=pl.DeviceIdType.MESH)

        mine = pltpu.make_async_copy(x_ref, rows(*me), local_sem)
        mine.start()
        first = [copy(0, me, sibling, src=x_ref)]
        first += [copy(1 + j, me, (*chip, c), src=x_ref) for j, chip in enumerate(chips)]
        for cp in first:
            cp.start()
        passed = [copy(4 + j, (*chip, c), sibling) for j, chip in enumerate(chips)]
        for j, chip in enumerate(chips):
            copy(1 + j, (*chip, c), me).wait_recv()
            passed[j].start()
        copy(0, sibling, me).wait_recv()
        for j, chip in enumerate(chips):
            copy(4 + j, (*chip, 1 - c), me).wait_recv()
        for cp in first + passed:
            cp.wait_send()
        mine.wait()

    return pl.pallas_call(
        body, name=name, out_shape=jax.ShapeDtypeStruct(tuple(full), xs.dtype),
        in_specs=[_HBM], out_specs=_HBM, scratch_shapes=_COMM_SCRATCH,
    )(xs)


def scatter_blocks(g, axis, name):
    nd = g.ndim
    n = g.shape[axis] // N_DEV
    blk = list(g.shape)
    blk[axis] = n

    def body(g_ref, out_ref, send_sems, recv_sems, local_sem):
        def src_of(idx):
            sl = [slice(None)] * nd
            sl[axis] = pl.ds(idx * n, n)
            return g_ref.at[tuple(sl)]

        _exchange(src_of, lambda idx: out_ref.at[idx], send_sems, recv_sems, local_sem)

    return pl.pallas_call(
        body, name=name, out_shape=jax.ShapeDtypeStruct((N_DEV, *blk), g.dtype),
        in_specs=[_HBM], out_specs=_HBM, scratch_shapes=_COMM_SCRATCH,
    )(g)


def matmul(a, b, *, ta=False, tb=False, out_dtype=F32, add=None, name):
    m = a.shape[1] if ta else a.shape[0]
    kk = a.shape[0] if ta else a.shape[1]
    n = b.shape[0] if tb else b.shape[1]
    tm = _pick(m, (1024, 512, 256, 128))
    tn = _pick(n, (1024, 512, 256, 128))
    tk = _pick(kk, (512, 256, 128))
    nk = kk // tk
    dims = (((0,) if ta else (1,), (1,) if tb else (0,)), ((), ()))

    def body(*refs):
        if add is None:
            a_ref, b_ref, o_ref, acc = refs
        else:
            a_ref, b_ref, add_ref, o_ref, acc = refs
        k = pl.program_id(2)

        @pl.when(k == 0)
        def _():
            acc[...] = jnp.zeros_like(acc)

        acc[...] += lax.dot_general(a_ref[...], b_ref[...], dims, preferred_element_type=F32)

        @pl.when(k == nk - 1)
        def _():
            r = acc[...]
            if add is not None:
                r = r + add_ref[...].astype(F32)
            o_ref[...] = r.astype(out_dtype)

    a_spec = (pl.BlockSpec((tk, tm), lambda i, j, k: (k, i)) if ta
              else pl.BlockSpec((tm, tk), lambda i, j, k: (i, k)))
    b_spec = (pl.BlockSpec((tn, tk), lambda i, j, k: (j, k)) if tb
              else pl.BlockSpec((tk, tn), lambda i, j, k: (k, j)))
    o_spec = pl.BlockSpec((tm, tn), lambda i, j, k: (i, j))
    ins = [a, b] + ([] if add is None else [add])
    in_specs = [a_spec, b_spec] + ([] if add is None else [o_spec])
    return pl.pallas_call(
        body, name=name, out_shape=jax.ShapeDtypeStruct((m, n), out_dtype),
        grid=(m // tm, n // tn, nk), in_specs=in_specs, out_specs=o_spec,
        scratch_shapes=[pltpu.VMEM((tm, tn), F32)],
        compiler_params=_params(("parallel", "parallel", "arbitrary")),
    )(*ins)


def rowwise(fn, ins, rows, outs, accs, *, tl, name):
    n_rows = ins[0].shape[0]
    tl = min(tl, n_rows)
    n_in, n_row, n_out, n_acc = len(ins), len(rows), len(outs), len(accs)

    def body(*refs):
        in_refs = refs[:n_in]
        row_refs = refs[n_in:n_in + n_row]
        out_refs = refs[n_in + n_row:n_in + n_row + n_out]
        acc_refs = refs[n_in + n_row + n_out:]
        res_outs, res_accs = fn(*[r[...] for r in in_refs], *[r[...] for r in row_refs])
        for r, v in zip(out_refs, res_outs):
            r[...] = v.astype(r.dtype)
        if n_acc:
            @pl.when(pl.program_id(0) == 0)
            def _():
                for r in acc_refs:
                    r[...] = jnp.zeros_like(r)

            for r, v in zip(acc_refs, res_accs):
                r[...] += v

    in_specs = [pl.BlockSpec((tl, a.shape[1]), lambda i: (i, 0)) for a in ins]
    in_specs += [pl.BlockSpec((1, a.shape[1]), lambda i: (0, 0)) for a in rows]
    out_specs = [pl.BlockSpec((tl, w), lambda i: (i, 0)) for w, _ in outs]
    out_specs += [pl.BlockSpec((1, w), lambda i: (0, 0)) for w in accs]
    out_shape = [jax.ShapeDtypeStruct((n_rows, w), dt) for w, dt in outs]
    out_shape += [jax.ShapeDtypeStruct((1, w), F32) for w in accs]
    res = pl.pallas_call(
        body, name=name, out_shape=out_shape, grid=(n_rows // tl,), in_specs=in_specs,
        out_specs=out_specs, compiler_params=_params(("arbitrary",)),
    )(*ins, *rows)
    return res


def _rsum(v):
    return jnp.sum(v, axis=0, keepdims=True)


def norm_fwd(x, g, name):
    def fn(xt, gt):
        r = lax.rsqrt(jnp.mean(xt * xt, axis=-1, keepdims=True) + NORM_EPS)
        return (xt * r * gt,), ()

    d = x.shape[1]
    return rowwise(fn, [x], [g], [(d, BF16)], [], tl=512, name=name)[0]


def norm_bwd(x, g, dh, dres, name):
    def fn(xt, dht, drt, gt):
        r = lax.rsqrt(jnp.mean(xt * xt, axis=-1, keepdims=True) + NORM_EPS)
        w = dht * gt
        dx = drt + (r * w - xt * (r * r * r) * jnp.mean(xt * w, axis=-1, keepdims=True))
        return (dx, dx), (_rsum(dht * xt * r),)

    d = x.shape[1]
    dx, dxb, dg = rowwise(fn, [x, dh, dres], [g], [(d, F32), (d, BF16)], [d], tl=256, name=name)
    return dx, dxb, dg


def loss_and_grad(x, g, target, name):
    d = x.shape[1]

    def fn(xt, tt, gt):
        r = lax.rsqrt(jnp.mean(xt * xt, axis=-1, keepdims=True) + NORM_EPS)
        y = xt * r * gt
        diff = y - tt
        dy = diff * (1.0 / d)
        w = dy * gt
        dx = r * w - xt * (r * r * r) * jnp.mean(xt * w, axis=-1, keepdims=True)
        return (dx, dx), (_rsum(dy * xt * r), _rsum(diff * diff))

    dx, dxb, dg, sq = rowwise(fn, [x, target], [g], [(d, F32), (d, BF16)], [d, d], tl=256, name=name)
    return dx, dxb, dg, sq


def glu_fwd(z, x, name):
    d = x.shape[1]

    def fn(zt, xt):
        return (xt + zt[:, :d] * jax.nn.sigmoid(zt[:, d:]),), ()

    return rowwise(fn, [z, x], [], [(d, F32)], [], tl=256, name=name)[0]


def glu_bwd(z, dx, name):
    d = dx.shape[1]

    def fn(zt, dxt):
        val, gate = zt[:, :d], zt[:, d:]
        s = jax.nn.sigmoid(gate)
        return (jnp.concatenate([dxt * s, dxt * val * s * (1.0 - s)], axis=1),), ()

    return rowwise(fn, [z, dx], [], [(2 * d, BF16)], [], tl=256, name=name)[0]


def add_cast(a, b, name):
    def fn(at, bt):
        return (at + bt,), ()

    return rowwise(fn, [a, b], [], [(a.shape[1], BF16)], [], tl=512, name=name)[0]


def _conv_tiles(n_rows, f):
    return min(256, n_rows), _pick(f, (512, 256, 128))


def _shifted(gate, h6, h7):
    row = lax.broadcasted_iota(jnp.int32, gate.shape, 0)
    g1 = jnp.where(row == 0, h7, pltpu.roll(gate, 1, 0))
    g2 = jnp.where(row == 0, h6, jnp.where(row == 1, h7, pltpu.roll(gate, 2, 0)))
    return g1, g2


def convglu_fwd(up, conv_w, conv_b, name):
    n_rows, f2 = up.shape
    f = f2 // 2
    tl, tc = _conv_tiles(n_rows, f)
    nc = f // tc
    hb = tl // 8

    def body(gate_ref, val_ref, halo_ref, w_ref, b_ref, o_ref):
        l = pl.program_id(1)
        live = (l > 0).astype(F32)
        h6 = halo_ref[6:7, :] * live
        h7 = halo_ref[7:8, :] * live
        gate = gate_ref[...]
        g1, g2 = _shifted(gate, h6, h7)
        gc = w_ref[0:1, :] * g2 + w_ref[1:2, :] * g1 + w_ref[2:3, :] * gate + b_ref[...]
        o_ref[...] = (_gelu(gc) * val_ref[...]).astype(o_ref.dtype)

    return pl.pallas_call(
        body, name=name, out_shape=jax.ShapeDtypeStruct((n_rows, f), BF16),
        grid=(nc, n_rows // tl),
        in_specs=[pl.BlockSpec((tl, tc), lambda c, l: (l, c)),
                  pl.BlockSpec((tl, tc), lambda c, l: (l, c + nc)),
                  pl.BlockSpec((8, tc), lambda c, l: (jnp.maximum(l * hb - 1, 0), c)),
                  pl.BlockSpec((3, tc), lambda c, l: (0, c)),
                  pl.BlockSpec((1, tc), lambda c, l: (0, c))],
        out_specs=pl.BlockSpec((tl, tc), lambda c, l: (l, c)),
        compiler_params=_params(("parallel", "arbitrary")),
    )(up, up, up, conv_w, conv_b)


def convglu_bwd(up, da, conv_w, conv_b, name):
    n_rows, f2 = up.shape
    f = f2 // 2
    tl, tc = _conv_tiles(n_rows, f)
    nc = f // tc
    nl = n_rows // tl
    hb = tl // 8
    last_halo = n_rows // 8 - 1

    def body(gate_ref, val_ref, da_ref, gprev_ref, gnext_ref, vnext_ref, danext_ref, w_ref, b_ref,
             dup_ref, dw_ref, db_ref):
        cc = pl.program_id(0)
        l = pl.program_id(1)
        live_prev = (l > 0).astype(F32)
        live_next = (l < nl - 1).astype(F32)
        h6 = gprev_ref[6:7, :] * live_prev
        h7 = gprev_ref[7:8, :] * live_prev
        w0, w1, w2 = w_ref[0:1, :], w_ref[1:2, :], w_ref[2:3, :]

        @pl.when(cc >= nc)
        def _():
            gate = gate_ref[...]
            g1, g2 = _shifted(gate, h6, h7)
            gc = w0 * g2 + w1 * g1 + w2 * gate + b_ref[...]
            dup_ref[...] = (da_ref[...] * _gelu(gc)).astype(dup_ref.dtype)

        @pl.when(cc < nc)
        def _():
            @pl.when(l == 0)
            def _():
                dw_ref[...] = jnp.zeros_like(dw_ref)
                db_ref[...] = jnp.zeros_like(db_ref)

            gate = jnp.concatenate([gate_ref[...], gnext_ref[...]], axis=0)
            val = jnp.concatenate([val_ref[...], vnext_ref[...]], axis=0)
            dae = jnp.concatenate([da_ref[...], danext_ref[...] * live_next], axis=0)
            g1, g2 = _shifted(gate, h6, h7)
            gc = w0 * g2 + w1 * g1 + w2 * gate + b_ref[...]
            dgc = dae * val * _gelu_grad(gc)
            ext = tl + 8
            d1 = pltpu.roll(dgc, ext - 1, 0)
            d2 = pltpu.roll(dgc, ext - 2, 0)
            dgate = w2 * dgc + w1 * d1 + w0 * d2
            dup_ref[...] = dgate[:tl].astype(dup_ref.dtype)
            dg = dgc[:tl]
            db_ref[...] += _rsum(dg)
            dw_ref[0:1, :] += _rsum(dg * g2[:tl])
            dw_ref[1:2, :] += _rsum(dg * g1[:tl])
            dw_ref[2:3, :] += _rsum(dg * gate[:tl])

    def cmod(cc):
        return jnp.where(cc >= nc, cc - nc, cc)

    def cacc(cc):
        return jnp.minimum(cc, nc - 1)

    def prev_blk(l):
        return jnp.maximum(l * hb - 1, 0)

    def next_blk(l):
        return jnp.minimum((l + 1) * hb, last_halo)

    dup, dw, db = pl.pallas_call(
        body, name=name,
        out_shape=[jax.ShapeDtypeStruct((n_rows, f2), BF16), jax.ShapeDtypeStruct((3, f), F32),
                   jax.ShapeDtypeStruct((1, f), F32)],
        grid=(2 * nc, nl),
        in_specs=[pl.BlockSpec((tl, tc), lambda cc, l: (l, cmod(cc))),
                  pl.BlockSpec((tl, tc), lambda cc, l: (l, cmod(cc) + nc)),
                  pl.BlockSpec((tl, tc), lambda cc, l: (l, cmod(cc))),
                  pl.BlockSpec((8, tc), lambda cc, l: (prev_blk(l), cmod(cc))),
                  pl.BlockSpec((8, tc), lambda cc, l: (next_blk(l), cmod(cc))),
                  pl.BlockSpec((8, tc), lambda cc, l: (next_blk(l), cmod(cc) + nc)),
                  pl.BlockSpec((8, tc), lambda cc, l: (next_blk(l), cmod(cc))),
                  pl.BlockSpec((3, tc), lambda cc, l: (0, cmod(cc))),
                  pl.BlockSpec((1, tc), lambda cc, l: (0, cmod(cc)))],
        out_specs=[pl.BlockSpec((tl, tc), lambda cc, l: (l, cc)),
                   pl.BlockSpec((3, tc), lambda cc, l: (0, cacc(cc))),
                   pl.BlockSpec((1, tc), lambda cc, l: (0, cacc(cc)))],
        compiler_params=_params(("arbitrary", "arbitrary")),
    )(up, up, da, up, up, up, da, conv_w, conv_b)
    return dup, dw, db


def _log_sigmoid(z):
    return jnp.minimum(z, 0.0) - jnp.log(1.0 + jnp.exp(-jnp.abs(z)))


def fgate_fwd(zf, bf, n_heads, name):
    n_rows = zf.shape[0]
    tl = min(512, n_rows)

    def body(z_ref, b_ref, c_ref, carry):
        @pl.when(pl.program_id(0) == 0)
        def _():
            carry[...] = jnp.zeros_like(carry)

        lf = _log_sigmoid(z_ref[...] + b_ref[...])
        lft = lf.T
        r = lax.broadcasted_iota(jnp.int32, (tl, tl), 0)
        c = lax.broadcasted_iota(jnp.int32, (tl, tl), 1)
        tri = (r <= c).astype(F32)
        cum = jnp.dot(lft, tri, precision=HI, preferred_element_type=F32) + carry[...]
        c_ref[...] = cum[:n_heads]
        carry[...] += jnp.sum(lft, axis=1, keepdims=True)

    return pl.pallas_call(
        body, name=name, out_shape=jax.ShapeDtypeStruct((n_heads, n_rows), F32),
        grid=(n_rows // tl,),
        in_specs=[pl.BlockSpec((tl, 128), lambda i: (i, 0)), pl.BlockSpec((1, 128), lambda i: (0, 0))],
        out_specs=pl.BlockSpec((n_heads, tl), lambda i: (0, i)),
        scratch_shapes=[pltpu.VMEM((128, 1), F32)],
        compiler_params=_params(("arbitrary",)),
    )(zf, bf)


def fgate_bwd(dc_a, dc_b, zf, bf, name):
    n_heads, n_rows = dc_a.shape
    tl = min(512, n_rows)
    nl = n_rows // tl

    def body(da_ref, db_ref, z_ref, b_ref, dz_ref, dbias_ref, carry):
        @pl.when(pl.program_id(0) == 0)
        def _():
            carry[...] = jnp.zeros_like(carry)
            dbias_ref[...] = jnp.zeros_like(dbias_ref)

        dc = da_ref[...] + db_ref[...]
        dcp = jnp.concatenate([dc, jnp.zeros((128 - n_heads, tl), F32)], axis=0)
        r = lax.broadcasted_iota(jnp.int32, (tl, tl), 0)
        c = lax.broadcasted_iota(jnp.int32, (tl, tl), 1)
        later = (c >= r).astype(F32)
        local = lax.dot_general(later, dcp, (((1,), (1,)), ((), ())), precision=HI,
                                preferred_element_type=F32)
        dlf = local + carry[...]
        carry[...] += local[0:1, :]
        z = z_ref[...] + b_ref[...]
        dz = dlf * jax.nn.sigmoid(-z)
        dz_ref[...] = dz.astype(dz_ref.dtype)
        dbias_ref[...] += _rsum(dz)

    return pl.pallas_call(
        body, name=name,
        out_shape=[jax.ShapeDtypeStruct((n_rows, 128), BF16), jax.ShapeDtypeStruct((1, 128), F32)],
        grid=(nl,),
        in_specs=[pl.BlockSpec((n_heads, tl), lambda i: (0, nl - 1 - i)),
                  pl.BlockSpec((n_heads, tl), lambda i: (0, nl - 1 - i)),
                  pl.BlockSpec((tl, 128), lambda i: (nl - 1 - i, 0)),
                  pl.BlockSpec((1, 128), lambda i: (0, 0))],
        out_specs=[pl.BlockSpec((tl, 128), lambda i: (nl - 1 - i, 0)),
                   pl.BlockSpec((1, 128), lambda i: (0, 0))],
        scratch_shapes=[pltpu.VMEM((1, 128), F32)],
        compiler_params=_params(("arbitrary",)),
    )(dc_a, dc_b, zf, bf)


def _pick_row(blk, h):
    row = lax.broadcasted_iota(jnp.int32, blk.shape, 0)
    return jnp.sum(jnp.where(row == h, blk, 0.0), axis=0, keepdims=True)


def _pick_first(blk, h):
    r = lax.broadcasted_iota(jnp.int32, blk.shape, 0)
    c = lax.broadcasted_iota(jnp.int32, blk.shape, 1)
    return jnp.sum(jnp.where((r == h) & (c == 0), blk, 0.0), keepdims=True)


def _causal(s, keys_on_rows=False):
    r = lax.broadcasted_iota(jnp.int32, s.shape, 0)
    c = lax.broadcasted_iota(jnp.int32, s.shape, 1)
    return jnp.where((r <= c) if keys_on_rows else (c <= r), s, MASK_VALUE)


def _attn_tile(n_rows):
    return min(512, n_rows)


def attn_layouts(a):
    n_heads, n_rows = a.shape
    tq = _attn_tile(n_rows)
    return a.reshape(n_heads, n_rows // tq, tq).transpose(1, 0, 2), a.reshape(n_heads, n_rows, 1)


def flash_fwd(q, k, v, c_rows, name):
    n_rows, d = q.shape
    n_heads = d // HEAD_DIM
    tq = _attn_tile(n_rows)
    nq = n_rows // tq
    scale = HEAD_DIM ** -0.5

    def body(q_ref, k_ref, v_ref, c_ref, o_ref, o32_ref, lse_ref):
        h, i = pl.program_id(0), pl.program_id(1)
        qv = q_ref[...]
        c_q = _pick_first(c_ref[i], h)

        def block(j, carry, diagonal):
            m_prev, l_prev, acc = carry
            rows = pl.ds(pl.multiple_of(j * tq, tq), tq)
            s = lax.dot_general(qv, k_ref[rows, :], (((1,), (1,)), ((), ())),
                                preferred_element_type=F32) * scale
            s = s + (c_q - _pick_row(c_ref[j], h))
            if diagonal:
                s = _causal(s)
            m_new = jnp.maximum(m_prev, jnp.max(s, axis=1, keepdims=True))
            p = jnp.exp(s - m_new)
            alpha = jnp.exp(m_prev - m_new)
            p_hi = p.astype(BF16)
            p_lo = (p - p_hi.astype(F32)).astype(BF16)
            vv = v_ref[rows, :]
            pv = (jnp.dot(p_hi, vv, preferred_element_type=F32)
                  + jnp.dot(p_lo, vv, preferred_element_type=F32))
            return m_new, alpha * l_prev + jnp.sum(p, axis=1, keepdims=True), alpha * acc + pv

        init = (jnp.full((tq, 1), MASK_VALUE, F32), jnp.zeros((tq, 1), F32), jnp.zeros((tq, HEAD_DIM), F32))
        carry = lax.fori_loop(0, i, lambda j, c: block(j, c, False), init)
        m_fin, l_fin, acc = block(i, carry, True)
        o = acc / l_fin
        o_ref[...] = o.astype(o_ref.dtype)
        o32_ref[...] = o
        lse_ref[0] = m_fin + jnp.log(l_fin)

    qspec = pl.BlockSpec((tq, HEAD_DIM), lambda h, i: (i, h))
    kspec = pl.BlockSpec((n_rows, HEAD_DIM), lambda h, i: (0, h))
    return pl.pallas_call(
        body, name=name,
        out_shape=[jax.ShapeDtypeStruct((n_rows, d), BF16), jax.ShapeDtypeStruct((n_rows, d), F32),
                   jax.ShapeDtypeStruct((n_heads, n_rows, 1), F32)],
        grid=(n_heads, nq),
        in_specs=[qspec, kspec, kspec, pl.BlockSpec((nq, n_heads, tq), lambda h, i: (0, 0, 0))],
        out_specs=[qspec, qspec, pl.BlockSpec((1, tq, 1), lambda h, i: (h, i, 0))],
        compiler_params=_params(("parallel", "arbitrary")),
    )(q, k, v, c_rows)


def attn_delta(do, o, name):
    n_rows, d = o.shape
    n_heads = d // HEAD_DIM
    tq = min(512, n_rows)

    def body(do_ref, o_ref, out_ref):
        out_ref[0] = jnp.sum(do_ref[...].astype(F32) * o_ref[...].astype(F32), axis=1, keepdims=True)

    spec = pl.BlockSpec((tq, HEAD_DIM), lambda i, h: (i, h))
    return pl.pallas_call(
        body, name=name, out_shape=jax.ShapeDtypeStruct((n_heads, n_rows, 1), F32),
        grid=(n_rows // tq, n_heads), in_specs=[spec, spec],
        out_specs=pl.BlockSpec((1, tq, 1), lambda i, h: (h, i, 0)),
        compiler_params=_params(("parallel", "parallel")),
    )(do, o)


def flash_bwd_dq(q, k, v, do, c_rows, lse, delta, name):
    n_rows, d = q.shape
    n_heads = d // HEAD_DIM
    tq = _attn_tile(n_rows)
    nq = n_rows // tq
    scale = HEAD_DIM ** -0.5

    def body(q_ref, k_ref, v_ref, do_ref, c_ref, lse_ref, dl_ref, dq_ref):
        h, i = pl.program_id(0), pl.program_id(1)
        qv, dov = q_ref[...], do_ref[...]
        lse_q, dl_q = lse_ref[0], dl_ref[0]
        c_q = _pick_first(c_ref[i], h)

        def block(j, acc, diagonal):
            rows = pl.ds(pl.multiple_of(j * tq, tq), tq)
            kk = k_ref[rows, :]
            s = lax.dot_general(qv, kk, (((1,), (1,)), ((), ())), preferred_element_type=F32) * scale
            s = s + (c_q - _pick_row(c_ref[j], h))
            if diagonal:
                s = _causal(s)
            p = jnp.exp(s - lse_q)
            dp = lax.dot_general(dov, v_ref[rows, :], (((1,), (1,)), ((), ())), preferred_element_type=F32)
            ds = p * (dp - dl_q)
            return acc + jnp.dot(ds.astype(BF16), kk, preferred_element_type=F32)

        acc = lax.fori_loop(0, i, lambda j, a: block(j, a, False), jnp.zeros((tq, HEAD_DIM), F32))
        acc = block(i, acc, True)
        dq_ref[...] = (acc * scale).astype(dq_ref.dtype)

    qspec = pl.BlockSpec((tq, HEAD_DIM), lambda h, i: (i, h))
    kspec = pl.BlockSpec((n_rows, HEAD_DIM), lambda h, i: (0, h))
    vec = pl.BlockSpec((1, tq, 1), lambda h, i: (h, i, 0))
    return pl.pallas_call(
        body, name=name, out_shape=jax.ShapeDtypeStruct((n_rows, d), BF16),
        grid=(n_heads, nq),
        in_specs=[qspec, kspec, kspec, qspec, pl.BlockSpec((nq, n_heads, tq), lambda h, i: (0, 0, 0)), vec, vec],
        out_specs=qspec,
        compiler_params=_params(("parallel", "arbitrary")),
    )(q, k, v, do, c_rows, lse, delta)


def flash_bwd_dkv(q, k, v, do, c_rows, c_cols, lse_rows, delta_rows, name):
    n_rows, d = q.shape
    n_heads = d // HEAD_DIM
    tq = _attn_tile(n_rows)
    nq = n_rows // tq
    scale = HEAD_DIM ** -0.5

    def body(q_ref, k_ref, v_ref, do_ref, c_ref, ck_ref, lse_ref, dl_ref, dk_ref, dv_ref, dc_ref):
        h, j = pl.program_id(0), pl.program_id(1)
        kk, vv = k_ref[...], v_ref[...]
        c_k = ck_ref[0]

        def block(i, carry, diagonal):
            dk, dv, dc = carry
            rows = pl.ds(pl.multiple_of(i * tq, tq), tq)
            qi, doi = q_ref[rows, :], do_ref[rows, :]
            st = lax.dot_general(kk, qi, (((1,), (1,)), ((), ())), preferred_element_type=F32) * scale
            st = st + (_pick_first(c_ref[i], h) - c_k)
            if diagonal:
                st = _causal(st, keys_on_rows=True)
            pt = jnp.exp(st - _pick_row(lse_ref[i], h))
            dpt = lax.dot_general(vv, doi, (((1,), (1,)), ((), ())), preferred_element_type=F32)
            dst = pt * (dpt - _pick_row(dl_ref[i], h))
            dv = dv + jnp.dot(pt.astype(BF16), doi, preferred_element_type=F32)
            dk = dk + jnp.dot(dst.astype(BF16), qi, preferred_element_type=F32)
            return dk, dv, dc - jnp.sum(dst, axis=1, keepdims=True)

        zero = jnp.zeros((tq, HEAD_DIM), F32)
        carry = block(j, (zero, zero, jnp.zeros((tq, 1), F32)), True)
        dk, dv, dc = lax.fori_loop(j + 1, nq, lambda i, c: block(i, c, False), carry)
        dk_ref[...] = dk * scale
        dv_ref[...] = dv
        dc_ref[0] = dc

    kspec = pl.BlockSpec((tq, HEAD_DIM), lambda h, j: (j, h))
    qspec = pl.BlockSpec((n_rows, HEAD_DIM), lambda h, j: (0, h))
    rows3 = pl.BlockSpec((nq, n_heads, tq), lambda h, j: (0, 0, 0))
    col = pl.BlockSpec((1, tq, 1), lambda h, j: (h, j, 0))
    return pl.pallas_call(
        body, name=name,
        out_shape=[jax.ShapeDtypeStruct((n_rows, d), F32), jax.ShapeDtypeStruct((n_rows, d), F32),
                   jax.ShapeDtypeStruct((n_heads, n_rows, 1), F32)],
        grid=(n_heads, nq),
        in_specs=[qspec, kspec, kspec, qspec, rows3, col, rows3, rows3],
        out_specs=[kspec, kspec, col],
        compiler_params=_params(("parallel", "arbitrary")),
    )(q, k, v, do, c_rows, c_cols, lse_rows, delta_rows)


S5_PARAM_SHAPES = [(1, S5_2P)] * 3 + [(S5_2P, 1)] * 3 + [(S5_GROUP, S5_2P)] * 4 + [(S5_2P, S5_TC)] * 4 + [(S5_TC, 1)]


def _tile_rows(a, reps):
    return jnp.concatenate([a] * reps, axis=0)


def _s5_build(lrl, lil, lsl, lrc, lic, lsc, cri, cir, bt2, bt2s, brit, birt, ctt, ctts):
    first = lax.broadcasted_iota(jnp.int32, (1, S5_2P), 1) < S5_STATE
    sgn = jnp.where(first, -1.0, 1.0)
    dt_l = jnp.exp(lsl)
    a_l, b_l = lrl * dt_l, lil * dt_l
    mag1 = jnp.exp(a_l)
    lbr, lbi = mag1 * jnp.cos(b_l), mag1 * jnp.sin(b_l)
    n = lrl * lrl + lil * lil
    kr = ((lbr - 1.0) * lrl + lbi * lil) / n
    ki = (lbi * lrl - (lbr - 1.0) * lil) / n
    blk = lax.shift_right_logical(lax.broadcasted_iota(jnp.int32, (S5_TC, 1), 0), 4).astype(F32)
    e_end = (S5_CHUNK - 1.0) - blk
    mag_e = jnp.exp(e_end * a_l)
    pr, pi = mag_e * jnp.cos(e_end * b_l), mag_e * jnp.sin(e_end * b_l)
    cr, ci = pr * kr - pi * ki, pr * ki + pi * kr
    w_end = cr * _tile_rows(bt2, S5_CHUNK) + (sgn * ci) * _tile_rows(bt2s, S5_CHUNK)
    mag_j = jnp.exp(blk * a_l)
    qr, qi = mag_j * jnp.cos(blk * b_l), mag_j * jnp.sin(blk * b_l)
    xmat = qr * _tile_rows(cri, S5_CHUNK) + (sgn * qi) * _tile_rows(cir, S5_CHUNK)
    dt_c = jnp.exp(lsc)
    a_c, b_c = lrc * dt_c, lic * dt_c
    m1c = jnp.exp(a_c)
    lbrc, lbic = m1c * jnp.cos(b_c), m1c * jnp.sin(b_c)
    n_c = lrc * lrc + lic * lic
    krc = ((lbrc - 1.0) * lrc + lbic * lic) / n_c
    kic = (lbic * lrc - (lbrc - 1.0) * lic) / n_c
    top = lax.broadcasted_iota(jnp.int32, (S5_2P, 1), 0) < S5_STATE
    rhs = jnp.where(top, krc, -krc) * brit + (-kic) * birt
    z = jnp.dot(xmat, rhs, precision=HI, preferred_element_type=F32)
    e_out = (lax.shift_right_logical(lax.broadcasted_iota(jnp.int32, (1, S5_TC), 1), 4) + 1).astype(F32)
    mag_o = jnp.exp(a_c * e_out)
    o_r, o_i = mag_o * jnp.cos(b_c * e_out), mag_o * jnp.sin(b_c * e_out)
    w_out = jnp.where(top, o_r, -o_r) * ctt + (-o_i) * ctts
    m16 = jnp.exp(float(S5_CHUNK) * a_l)
    a1 = m16 * jnp.cos(float(S5_CHUNK) * b_l)
    a2 = sgn * (m16 * jnp.sin(float(S5_CHUNK) * b_l))
    return z, w_end, w_out, a1, a2


def _lane_block_bit(b):
    blk = lax.shift_right_logical(lax.broadcasted_iota(jnp.int32, (S5_TC, S5_TC), 1), 4)
    return (lax.shift_right_logical(blk, b) & 1) == 1


def _toeplitz(z, dcol):
    for b in range(4):
        s = S5_GROUP << b
        moved = jnp.concatenate([jnp.zeros((s, S5_TC), F32), z[:S5_TC - s]], axis=0)
        z = jnp.where(_lane_block_bit(b), moved, z)
    r = lax.broadcasted_iota(jnp.int32, (S5_TC, S5_TC), 0)
    c = lax.broadcasted_iota(jnp.int32, (S5_TC, S5_TC), 1)
    return z + jnp.where(r == c, dcol, 0.0)


def _toeplitz_t(dt):
    r = lax.broadcasted_iota(jnp.int32, (S5_TC, S5_TC), 0)
    c = lax.broadcasted_iota(jnp.int32, (S5_TC, S5_TC), 1)
    ddcol = jnp.sum(jnp.where(r == c, dt, 0.0), axis=1, keepdims=True)
    for b in range(4):
        s = S5_GROUP << b
        moved = jnp.concatenate([dt[s:], jnp.zeros((s, S5_TC), F32)], axis=0)
        dt = jnp.where(_lane_block_bit(b), moved, dt)
    return dt, ddcol


def _s5_powers(a1, a2, n):
    first = lax.broadcasted_iota(jnp.int32, a1.shape, 1) < S5_STATE
    ar, ai = a1, jnp.where(first, -a2, a2)
    out = []
    d = 1
    while d < n:
        out.append((ar, jnp.where(first, -ai, ai)))
        ar, ai = ar * ar - ai * ai, 2.0 * ar * ai
        d *= 2
    return out


def _s5_prefix(e, pows, reverse):
    n = e.shape[0]
    row = lax.broadcasted_iota(jnp.int32, e.shape, 0)
    x = e
    for b, (p1, p2) in enumerate(pows):
        d = 1 << b
        if reverse:
            sh = jnp.where(row < n - d, pltpu.roll(x, n - d, 0), 0.0)
            x = x + p1 * sh - p2 * pltpu.roll(sh, S5_STATE, 1)
        else:
            sh = jnp.where(row >= d, pltpu.roll(x, d, 0), 0.0)
            x = x + p1 * sh + p2 * pltpu.roll(sh, S5_STATE, 1)
    return x


def _s5_scan(e, pows):
    n = e.shape[0]
    row = lax.broadcasted_iota(jnp.int32, e.shape, 0)
    return jnp.where(row >= 1, pltpu.roll(_s5_prefix(e, pows, False), 1, 0), 0.0)


def _s5_scan_t(dsin, pows):
    n = dsin.shape[0]
    row = lax.broadcasted_iota(jnp.int32, dsin.shape, 0)
    return jnp.where(row < n - 1, pltpu.roll(_s5_prefix(dsin, pows, True), n - 1, 0), 0.0)


def _dot_hi(a, b, dims):
    return lax.dot_general(a, b, (dims, ((), ())), precision=HI, preferred_element_type=F32)


_NN = ((1,), (0,))
_NT = ((1,), (1,))
_TN = ((0,), (0,))


def _s5_specs(gb):
    return [pl.BlockSpec((gb, *s), lambda i: (i, 0, 0)) for s in S5_PARAM_SHAPES]


def s5_fwd(u, prm, name):
    n_groups, nk, _ = u.shape
    gb = min(4, n_groups)
    n_prm = len(S5_PARAM_SHAPES)

    def body(*refs):
        u_ref = refs[0]
        p_refs = refs[1:1 + n_prm]
        o_ref = refs[1 + n_prm]
        sin_sc, toep_sc = refs[2 + n_prm:]
        for g in range(gb):
            z, w_end, w_out, a1, a2 = _s5_build(*[r[g] for r in p_refs[:-1]])
            toep_sc[...] = _toeplitz(z, p_refs[-1][g])
            ug = u_ref[g]
            sin_sc[...] = _s5_scan(_dot_hi(ug, w_end, _NN), _s5_powers(a1, a2, nk))
            y = _dot_hi(ug, toep_sc[...], _NT) + _dot_hi(sin_sc[...], w_out, _NN)
            o_ref[g] = _gelu(y).astype(o_ref.dtype)

    blk = pl.BlockSpec((gb, nk, S5_TC), lambda i: (i, 0, 0))
    return pl.pallas_call(
        body, name=name, out_shape=jax.ShapeDtypeStruct(u.shape, BF16), grid=(n_groups // gb,),
        in_specs=[blk] + _s5_specs(gb), out_specs=blk,
        scratch_shapes=[pltpu.VMEM((nk, S5_2P), F32), pltpu.VMEM((S5_TC, S5_TC), F32)],
        compiler_params=_params(("parallel",)),
    )(u, *prm)


def s5_bwd(u, dyg, prm, name):
    n_groups, nk, _ = u.shape
    gb = min(4, n_groups)
    n_prm = len(S5_PARAM_SHAPES)

    def body(*refs):
        u_ref, dyg_ref = refs[0], refs[1]
        p_refs = refs[2:2 + n_prm]
        du_ref = refs[2 + n_prm]
        dp_refs = refs[3 + n_prm:3 + 2 * n_prm]
        sin_sc, de_sc, dy_sc, toep_sc = refs[3 + 2 * n_prm:]
        for g in range(gb):
            outs, vjp = jax.vjp(_s5_build, *[r[g] for r in p_refs[:-1]])
            z, w_end, w_out, a1, a2 = outs
            toep_sc[...] = _toeplitz(z, p_refs[-1][g])
            pows = _s5_powers(a1, a2, nk)
            ug = u_ref[g]
            sin_sc[...] = _s5_scan(_dot_hi(ug, w_end, _NN), pows)
            y = _dot_hi(ug, toep_sc[...], _NT) + _dot_hi(sin_sc[...], w_out, _NN)
            dy_sc[...] = dyg_ref[g] * _gelu_grad(y)
            dy = dy_sc[...]
            de_sc[...] = _s5_scan_t(_dot_hi(dy, w_out, _NT), pows)
            de, sin = de_sc[...], sin_sc[...]
            du_ref[g] = _dot_hi(dy, toep_sc[...], _NN) + _dot_hi(de, w_end, _NT)
            dz, ddcol = _toeplitz_t(_dot_hi(dy, ug, _TN))
            da1 = jnp.sum(de * sin, axis=0, keepdims=True)
            da2 = jnp.sum(de * pltpu.roll(sin, S5_STATE, 1), axis=0, keepdims=True)
            grads = vjp((dz, _dot_hi(ug, de, _TN), _dot_hi(sin, dy, _TN), da1, da2))
            for r, v in zip(dp_refs[:-1], grads):
                r[g] = v
            dp_refs[-1][g] = ddcol

    blk = pl.BlockSpec((gb, nk, S5_TC), lambda i: (i, 0, 0))
    res = pl.pallas_call(
        body, name=name,
        out_shape=[jax.ShapeDtypeStruct(u.shape, F32)]
        + [jax.ShapeDtypeStruct((n_groups, *s), F32) for s in S5_PARAM_SHAPES],
        grid=(n_groups // gb,),
        in_specs=[blk, blk] + _s5_specs(gb), out_specs=[blk] + _s5_specs(gb),
        scratch_shapes=[pltpu.VMEM((nk, S5_2P), F32), pltpu.VMEM((nk, S5_2P), F32),
                        pltpu.VMEM((nk, S5_TC), F32), pltpu.VMEM((S5_TC, S5_TC), F32)],
        compiler_params=_params(("parallel",)),
    )(u, dyg, *prm)
    return res[0], res[1:]


def s5_prepare(lam_re, lam_im, log_step, b_re, b_im, c_re, c_im, d):
    g = lam_re.shape[0]
    lane = lambda a: a.reshape(g, 1, -1)
    col = lambda a: a.reshape(g, -1, 1)
    ls = jnp.broadcast_to(log_step[:, None], (g, S5_2P))
    brt = jnp.swapaxes(b_re, 1, 2)
    bit = jnp.swapaxes(b_im, 1, 2)
    crt = jnp.swapaxes(c_re, 1, 2)
    cit = jnp.swapaxes(c_im, 1, 2)
    cat = jnp.concatenate
    return [
        lane(cat([lam_re, lam_re], 1)), lane(cat([lam_im, lam_im], 1)), lane(ls),
        col(cat([lam_re, lam_re], 1)), col(cat([lam_im, lam_im], 1)), col(ls),
        cat([c_re, c_im], 2), cat([c_im, c_re], 2),
        cat([brt, bit], 2), cat([bit, brt], 2),
        jnp.tile(cat([b_re, b_im], 1), (1, 1, S5_CHUNK)), jnp.tile(cat([b_im, b_re], 1), (1, 1, S5_CHUNK)),
        jnp.tile(cat([crt, cit], 1), (1, 1, S5_CHUNK)), jnp.tile(cat([cit, crt], 1), (1, 1, S5_CHUNK)),
        col(jnp.tile(d, (1, S5_CHUNK))),
    ]


def s5_param_grads(prm, grads):
    _, vjp = jax.vjp(s5_prepare, *prm)
    return vjp(list(grads))


def s5_to_groups(a):
    n_rows, d = a.shape
    g, nk = d // S5_GROUP, n_rows // S5_CHUNK
    return a.reshape(nk, S5_CHUNK, g, S5_GROUP).transpose(2, 0, 1, 3).reshape(g, nk, S5_TC)


def s5_from_groups(a):
    g, nk, _ = a.shape
    return a.reshape(g, nk, S5_CHUNK, S5_GROUP).transpose(1, 2, 0, 3).reshape(nk * S5_CHUNK, g * S5_GROUP)


def _adamw_math(w, g, m, v):
    m = ADAM_B1 * m + (1.0 - ADAM_B1) * g
    v = ADAM_B2 * v + (1.0 - ADAM_B2) * (g * g)
    m_hat = m / (1.0 - ADAM_B1 ** ADAM_STEP)
    v_hat = v / (1.0 - ADAM_B2 ** ADAM_STEP)
    delta = -ADAM_LR * (m_hat / (jnp.sqrt(v_hat) + ADAM_EPS) + ADAM_WD * w)
    return delta, m, v


def adamw_sum8(parts, w, m, v, name):
    r, c = w.shape
    tr = _pick(r, (128, 64, 32, 16, 8))

    def body(p_ref, w_ref, m_ref, v_ref, g_out, d_out, m_out, v_out):
        g = p_ref[0].astype(F32)
        for s in range(1, N_DEV):
            g = g + p_ref[s].astype(F32)
        delta, m_new, v_new = _adamw_math(w_ref[...], g, m_ref[...], v_ref[...])
        g_out[...] = g
        d_out[...] = delta
        m_out[...] = m_new
        v_out[...] = v_new

    spec = pl.BlockSpec((tr, c), lambda i: (i, 0))
    return pl.pallas_call(
        body, name=name, out_shape=[jax.ShapeDtypeStruct((r, c), F32)] * 4, grid=(r // tr,),
        in_specs=[pl.BlockSpec((N_DEV, tr, c), lambda i: (0, i, 0)), spec, spec, spec],
        out_specs=[spec] * 4, compiler_params=_params(("parallel",)),
    )(parts, w, m, v)


def sum8(parts, name):
    r, c = parts.shape[1:]
    tr = _pick(r, (256, 128, 64, 32, 16, 8))

    def body(p_ref, o_ref):
        g = p_ref[0]
        for s in range(1, N_DEV):
            g = g + p_ref[s]
        o_ref[...] = g

    return pl.pallas_call(
        body, name=name, out_shape=jax.ShapeDtypeStruct((r, c), F32), grid=(r // tr,),
        in_specs=[pl.BlockSpec((N_DEV, tr, c), lambda i: (0, i, 0))],
        out_specs=pl.BlockSpec((tr, c), lambda i: (i, 0)), compiler_params=_params(("parallel",)),
    )(parts)


def adamw_plain(g, w, m, v, name):
    r, c = w.shape
    tr = _pick(r, (256, 128, 64, 32, 16, 8))

    def body(g_ref, w_ref, m_ref, v_ref, d_out, m_out, v_out):
        delta, m_new, v_new = _adamw_math(w_ref[...], g_ref[...], m_ref[...], v_ref[...])
        d_out[...] = delta
        m_out[...] = m_new
        v_out[...] = v_new

    spec = pl.BlockSpec((tr, c), lambda i: (i, 0))
    return pl.pallas_call(
        body, name=name, out_shape=[jax.ShapeDtypeStruct((r, c), F32)] * 3, grid=(r // tr,),
        in_specs=[spec] * 4, out_specs=[spec] * 3, compiler_params=_params(("parallel",)),
    )(g, w, m, v)


def _pack(arrays):
    pieces = []
    for a in arrays:
        flat = a.reshape(-1).astype(F32)
        pad = (-flat.shape[0]) % 1024
        pieces.append(jnp.pad(flat, (0, pad)))
    return jnp.concatenate(pieces).reshape(-1, 128)


def _unpack(buf, shapes):
    flat = buf.reshape(-1)
    out, off = [], 0
    for s in shapes:
        n = math.prod(s)
        out.append(flat[off:off + n].reshape(s))
        off += n + ((-n) % 1024)
    return out


WEIGHT_NAMES = ['a_norm', 'a_w_in', 'a_lambda_re', 'a_lambda_im', 'a_log_step', 'a_b_re', 'a_b_im', 'a_c_re',
                'a_c_im', 'a_d', 'a_w_glu', 'kv_norm', 'w_k', 'w_v', 'w_f', 'b_f', 'b_norm', 'b_w_q', 'b_w_o',
                'ffn_norm', 'ffn_w_up', 'ffn_conv_w', 'ffn_conv_b', 'ffn_w_down', 'final_norm']
BIG = {'a_w_in': 0, 'a_w_glu': 1, 'w_k': 0, 'w_v': 0, 'b_w_q': 0, 'b_w_o': 0, 'ffn_w_up': 1, 'ffn_w_down': 0}
SMALL_SHARDED = {'a_norm': 1, 'w_f': 0, 'ffn_conv_w': 2}


def _ffn_fwd(x, gain, w_up, conv_w, conv_b, w_down, tag):
    h = norm_fwd(x, gain, f"norm_fwd_{tag}")
    up = matmul(h, w_up, name=f"mm_up_{tag}")
    a = convglu_fwd(up, conv_w, conv_b, f"convglu_fwd_{tag}")
    x_out = matmul(a, w_down, add=x, name=f"mm_down_{tag}")
    return x_out, (x, h, up, a)


def _ffn_bwd(dx, dxb, saved, gain, w_up, conv_w, conv_b, w_down, tag):
    x, h, up, a = saved
    dw_down = matmul(a, dxb, ta=True, out_dtype=BF16, name=f"mm_dwdown_{tag}")
    da = matmul(dxb, w_down, tb=True, name=f"mm_da_{tag}")
    dup, dconv_w, dconv_b = convglu_bwd(up, da, conv_w, conv_b, f"convglu_bwd_{tag}")
    dw_up = matmul(h, dup, ta=True, out_dtype=BF16, name=f"mm_dwup_{tag}")
    dh = matmul(dup, w_up, tb=True, name=f"mm_dh_{tag}")
    dx_in, dxb_in, dgain = norm_bwd(x, gain, dh, dx, f"norm_bwd_{tag}")
    return dx_in, dxb_in, dict(norm=dgain, w_up=dw_up, conv_w=dconv_w, conv_b=dconv_b, w_down=dw_down)


def kernel(x, a_norm, a_w_in, a_lambda_re, a_lambda_im, a_log_step, a_b_re, a_b_im, a_c_re, a_c_im, a_d, a_w_glu, kv_norm, w_k, w_v, w_f, b_f, b_norm, b_w_q, b_w_o, ffn_norm, ffn_w_up, ffn_conv_w, ffn_conv_b, ffn_w_down, final_norm, loss_target, m_a_norm, m_a_w_in, m_a_lambda_re, m_a_lambda_im, m_a_log_step, m_a_b_re, m_a_b_im, m_a_c_re, m_a_c_im, m_a_d, m_a_w_glu, m_kv_norm, m_w_k, m_w_v, m_w_f, m_b_f, m_b_norm, m_b_w_q, m_b_w_o, m_ffn_norm, m_ffn_w_up, m_ffn_conv_w, m_ffn_conv_b, m_ffn_w_down, m_final_norm, v_a_norm, v_a_w_in, v_a_lambda_re, v_a_lambda_im, v_a_log_step, v_a_b_re, v_a_b_im, v_a_c_re, v_a_c_im, v_a_d, v_a_w_glu, v_kv_norm, v_w_k, v_w_v, v_w_f, v_b_f, v_b_norm, v_b_w_q, v_b_w_o, v_ffn_norm, v_ffn_w_up, v_ffn_conv_w, v_ffn_conv_b, v_ffn_w_down, v_final_norm):
    args = locals()
    w = {n: args[n] for n in WEIGHT_NAMES}
    mom = {n: args["m_" + n] for n in WEIGHT_NAMES}
    var = {n: args["v_" + n] for n in WEIGHT_NAMES}
    x0 = x[0]
    target = loss_target[0]
    n_rows, d = x0.shape
    n_heads = d // HEAD_DIM
    n_a = a_w_in.shape[0]
    n_b = b_w_q.shape[0]
    me = _my_index()

    def gathered(name, layer=None):
        ws = w[name] if layer is None else w[name][layer]
        tag = name if layer is None else f"{name}{layer}"
        return all_gather(ws.astype(BF16), BIG[name], f"ag_{tag}")

    small_local = [w[n] for n in SMALL_SHARDED]
    small_full = all_gather(_pack(small_local)[None], 0, "ag_small")
    per_dev = [_unpack(small_full[s], [a.shape for a in small_local]) for s in range(N_DEV)]
    full_small = {n: jnp.concatenate([per_dev[s][i] for s in range(N_DEV)], axis=SMALL_SHARDED[n])
                  for i, n in enumerate(SMALL_SHARDED)}
    a_norm_f, w_f_f, conv_w_f = full_small['a_norm'], full_small['w_f'], full_small['ffn_conv_w']
    w_f_pad = jnp.pad(w_f_f, ((0, 0), (0, 128 - n_heads))).astype(BF16)
    b_f_pad = jnp.pad(b_f, (0, 128 - n_heads)).reshape(1, 128)

    grads = {}
    big_grads = {}

    xs = x0
    saved_a, saved_b, saved_ffn = [], [], []
    weights_a, weights_b, weights_ffn = [], [], []
    for i in range(n_a):
        w_in = gathered('a_w_in', i)
        w_glu = gathered('a_w_glu', i)
        prm = s5_prepare(a_lambda_re[i], a_lambda_im[i], a_log_step[i], a_b_re[i], a_b_im[i],
                         a_c_re[i], a_c_im[i], a_d[i])
        gain = a_norm_f[i].reshape(1, d)
        h = norm_fwd(xs, gain, f"norm_fwd_a{i}")
        u = s5_to_groups(matmul(h, w_in, name=f"mm_win_{i}"))
        yg = s5_from_groups(s5_fwd(u, prm, f"s5_fwd_{i}"))
        z = matmul(yg, w_glu, name=f"mm_wglu_{i}")
        x_mid = glu_fwd(z, xs, f"glu_fwd_{i}")
        saved_a.append((xs, h, u, yg, z))
        weights_a.append((w_in, w_glu, prm, gain))
        fw = (ffn_norm[i].reshape(1, d), gathered('ffn_w_up', i), conv_w_f[i], ffn_conv_b[i].reshape(1, -1),
              gathered('ffn_w_down', i))
        xs, sv = _ffn_fwd(x_mid, *fw, f"f{i}")
        saved_ffn.append(sv)
        weights_ffn.append(fw)

    x_kv = xs
    kv_gain = kv_norm.reshape(1, d)
    wk_f, wv_f = gathered('w_k'), gathered('w_v')
    h_kv = norm_fwd(x_kv, kv_gain, "norm_fwd_kv")
    k = matmul(h_kv, wk_f, out_dtype=BF16, name="mm_wk")
    v = matmul(h_kv, wv_f, out_dtype=BF16, name="mm_wv")
    zf = matmul(h_kv, w_f_pad, name="mm_wf")
    ct = fgate_fwd(zf, b_f_pad, n_heads, "fgate_fwd")
    c_rows, c_cols = attn_layouts(ct)

    for j in range(n_b):
        li = n_a + j
        w_q, w_o = gathered('b_w_q', j), gathered('b_w_o', j)
        gain = b_norm[j].reshape(1, d)
        h = norm_fwd(xs, gain, f"norm_fwd_b{j}")
        q = matmul(h, w_q, out_dtype=BF16, name=f"mm_wq_{j}")
        o, o32, lse = flash_fwd(q, k, v, c_rows, f"flash_fwd_{j}")
        x_mid = matmul(o, w_o, add=xs, name=f"mm_wo_{j}")
        saved_b.append((xs, h, q, o, o32, lse))
        weights_b.append((w_q, w_o, gain))
        fw = (ffn_norm[li].reshape(1, d), gathered('ffn_w_up', li), conv_w_f[li], ffn_conv_b[li].reshape(1, -1),
              gathered('ffn_w_down', li))
        xs, sv = _ffn_fwd(x_mid, *fw, f"f{li}")
        saved_ffn.append(sv)
        weights_ffn.append(fw)

    dx, dxb, d_final, sq = loss_and_grad(xs, final_norm.reshape(1, d), target, "loss")
    loss = lax.psum(0.5 * jnp.sum(sq) / d, MESH_AXES)
    grads['final_norm'] = d_final.reshape(d)

    ffn_g = [None] * (n_a + n_b)
    b_g = [None] * n_b
    dk_tot = dv_tot = dc_tot = None
    for j in reversed(range(n_b)):
        li = n_a + j
        dx, dxb, ffn_g[li] = _ffn_bwd(dx, dxb, saved_ffn[li], *weights_ffn[li], f"f{li}")
        xs_in, h, q, o, o32, lse = saved_b[j]
        w_q, w_o, gain = weights_b[j]
        dw_o = matmul(o, dxb, ta=True, out_dtype=BF16, name=f"mm_dwo_{j}")
        do = matmul(dxb, w_o, tb=True, out_dtype=BF16, name=f"mm_do_{j}")
        delta = attn_delta(do, o32, f"attn_delta_{j}")
        dq = flash_bwd_dq(q, k, v, do, c_rows, lse, delta, f"flash_dq_{j}")
        lse_rows = attn_layouts(lse.reshape(n_heads, n_rows))[0]
        delta_rows = attn_layouts(delta.reshape(n_heads, n_rows))[0]
        dk_j, dv_j, dc_j = flash_bwd_dkv(q, k, v, do, c_rows, c_cols, lse_rows, delta_rows, f"flash_dkv_{j}")
        dc_j = dc_j.reshape(n_heads, n_rows)
        if dk_tot is None:
            dk_tot, dv_tot, dc_tot = [dk_j], [dv_j], [dc_j]
        else:
            dk_tot.append(dk_j), dv_tot.append(dv_j), dc_tot.append(dc_j)
        dw_q = matmul(h, dq, ta=True, out_dtype=BF16, name=f"mm_dwq_{j}")
        dh = matmul(dq, w_q, tb=True, name=f"mm_dhq_{j}")
        dx, dxb, dgain = norm_bwd(xs_in, gain, dh, dx, f"norm_bwd_b{j}")
        b_g[j] = dict(norm=dgain, w_q=dw_q, w_o=dw_o)

    def total(parts, tag):
        if len(parts) == 1:
            return parts[0].astype(BF16)
        acc = parts[0]
        for n_, p_ in enumerate(parts[1:-1]):
            acc = acc + p_
        return add_cast(acc, parts[-1], f"add_{tag}")

    dk_b, dv_b = total(dk_tot, "dk"), total(dv_tot, "dv")
    dc_a = dc_tot[0]
    dc_b = dc_tot[1] if len(dc_tot) > 1 else jnp.zeros_like(dc_a)
    for extra in dc_tot[2:]:
        dc_b = dc_b + extra
    dzf, db_f = fgate_bwd(dc_a, dc_b, zf, b_f_pad, "fgate_bwd")
    big_grads[('w_k', None)] = matmul(h_kv, dk_b, ta=True, out_dtype=BF16, name="mm_dwk")
    big_grads[('w_v', None)] = matmul(h_kv, dv_b, ta=True, out_dtype=BF16, name="mm_dwv")
    dw_f = matmul(h_kv, dzf, ta=True, name="mm_dwf")[:, :n_heads]
    dh_kv = matmul(dk_b, wk_f, tb=True, name="mm_dhk")
    dh_kv = matmul(dv_b, wv_f, tb=True, add=dh_kv, name="mm_dhv")
    dh_kv = matmul(dzf, w_f_pad, tb=True, add=dh_kv, name="mm_dhf")
    dx, dxb, d_kv_gain = norm_bwd(x_kv, kv_gain, dh_kv, dx, "norm_bwd_kv")
    grads['kv_norm'] = d_kv_gain.reshape(d)
    grads['b_f'] = db_f[0, :n_heads]

    a_g = [None] * n_a
    for i in reversed(range(n_a)):
        dx, dxb, ffn_g[i] = _ffn_bwd(dx, dxb, saved_ffn[i], *weights_ffn[i], f"f{i}")
        xs_in, h, u, yg, z = saved_a[i]
        w_in, w_glu, prm, gain = weights_a[i]
        dz = glu_bwd(z, dx, f"glu_bwd_{i}")
        dw_glu = matmul(yg, dz, ta=True, out_dtype=BF16, name=f"mm_dwglu_{i}")
        dyg = s5_to_groups(matmul(dz, w_glu, tb=True, name=f"mm_dyg_{i}"))
        du_g, dprm = s5_bwd(u, dyg, prm, f"s5_bwd_{i}")
        du = s5_from_groups(du_g).astype(BF16)
        dw_in = matmul(h, du, ta=True, out_dtype=BF16, name=f"mm_dwin_{i}")
        dh = matmul(du, w_in, tb=True, name=f"mm_dhin_{i}")
        dx, dxb, dgain = norm_bwd(xs_in, gain, dh, dx, f"norm_bwd_a{i}")
        a_g[i] = dict(norm=dgain, w_in=dw_in, w_glu=dw_glu, prm=s5_param_grads(
            (a_lambda_re[i], a_lambda_im[i], a_log_step[i], a_b_re[i], a_b_im[i], a_c_re[i], a_c_im[i],
             a_d[i]), dprm))
    grad_x = dx[None]

    for i in range(n_a):
        big_grads[('a_w_in', i)] = a_g[i]['w_in']
        big_grads[('a_w_glu', i)] = a_g[i]['w_glu']
    for j in range(n_b):
        big_grads[('b_w_q', j)] = b_g[j]['w_q']
        big_grads[('b_w_o', j)] = b_g[j]['w_o']
    for li in range(n_a + n_b):
        big_grads[('ffn_w_up', li)] = ffn_g[li]['w_up']
        big_grads[('ffn_w_down', li)] = ffn_g[li]['w_down']
    stack = lambda parts: jnp.stack(parts, axis=0)
    grads['a_norm'] = stack([a_g[i]['norm'].reshape(d) for i in range(n_a)])
    for pi, pname in enumerate(['a_lambda_re', 'a_lambda_im', 'a_log_step', 'a_b_re', 'a_b_im', 'a_c_re',
                                'a_c_im', 'a_d']):
        grads[pname] = stack([a_g[i]['prm'][pi] for i in range(n_a)])
    grads['w_f'] = dw_f
    grads['b_norm'] = stack([b_g[j]['norm'].reshape(d) for j in range(n_b)])
    grads['ffn_norm'] = stack([g_['norm'].reshape(d) for g_ in ffn_g])
    grads['ffn_conv_w'] = stack([g_['conv_w'] for g_ in ffn_g])
    grads['ffn_conv_b'] = stack([g_['conv_b'].reshape(-1) for g_ in ffn_g])

    small_names = [n for n in WEIGHT_NAMES if n not in BIG]
    small_shapes = [grads[n].shape for n in small_names]
    g_parts = all_gather(_pack([grads[n] for n in small_names])[None], 0, "ag_grads_small")
    g_sum = _unpack(sum8(g_parts, "sum_grads_small"), small_shapes)
    g_full = dict(zip(small_names, g_sum))
    g_local = {}
    for n in small_names:
        if n in SMALL_SHARDED:
            ax = SMALL_SHARDED[n]
            size = w[n].shape[ax]
            g_local[n] = lax.dynamic_slice_in_dim(g_full[n], me * size, size, axis=ax)
        else:
            g_local[n] = g_full[n]
    local_shapes = [w[n].shape for n in small_names]
    d_s, m_s, v_s = adamw_plain(_pack([g_local[n] for n in small_names]), _pack([w[n] for n in small_names]),
                                _pack([mom[n] for n in small_names]), _pack([var[n] for n in small_names]),
                                "adamw_small")
    out_g, out_d, out_m, out_v = dict(g_local), {}, {}, {}
    for n, dd, mm, vv in zip(small_names, _unpack(d_s, local_shapes), _unpack(m_s, local_shapes),
                             _unpack(v_s, local_shapes)):
        out_d[n], out_m[n], out_v[n] = dd, mm, vv

    for name, ax in BIG.items():
        layered = w[name].ndim == 3
        layers = range(w[name].shape[0]) if layered else [None]
        res = []
        for layer in layers:
            tag = name if layer is None else f"{name}{layer}"
            parts = scatter_blocks(big_grads[(name, layer)], ax, f"rs_{tag}")
            pick = (lambda a: a) if layer is None else (lambda a: a[layer])
            res.append(adamw_sum8(parts, pick(w[name]), pick(mom[name]), pick(var[name]), f"adamw_{tag}"))
        for dst, idx in ((out_g, 0), (out_d, 1), (out_m, 2), (out_v, 3)):
            dst[name] = stack([r[idx] for r in res]) if layered else res[0][idx]

    return (loss, grad_x, *[out_g[n] for n in WEIGHT_NAMES], *[out_d[n] for n in WEIGHT_NAMES],
            *[out_m[n] for n in WEIGHT_NAMES], *[out_v[n] for n in WEIGHT_NAMES])
```

```python
import functools
import math

import jax
import jax.numpy as jnp
from jax import lax
from jax.experimental import pallas as pl
from jax.experimental.pallas import tpu as pltpu

F32 = jnp.float32
BF16 = jnp.bfloat16
HI = lax.Precision.HIGHEST
MESH_AXES = ("x", "y", "c")
N_DEV = 8
VMEM_LIMIT_BYTES = 48 * 1024 * 1024

NORM_EPS = 1e-6
S5_GROUP = 16
S5_STATE = 64
S5_CHUNK = 16
S5_TC = S5_CHUNK * S5_GROUP
S5_2P = 2 * S5_STATE
HEAD_DIM = 128
MASK_VALUE = -1e30
GELU_C = math.sqrt(2.0 / math.pi)
GELU_A = 0.044715

ADAM_LR = 0.001
ADAM_B1 = 0.9
ADAM_B2 = 0.999
ADAM_EPS = 1e-08
ADAM_WD = 0.01
ADAM_STEP = 10


def _params(sem=None):
    return pltpu.CompilerParams(dimension_semantics=sem, vmem_limit_bytes=VMEM_LIMIT_BYTES)


def _pick(n, cands):
    for c in cands:
        if n % c == 0:
            return c
    return n


def _gelu(x):
    return 0.5 * x * (1.0 + jnp.tanh(GELU_C * (x + GELU_A * x * x * x)))


def _gelu_grad(x):
    t = jnp.tanh(GELU_C * (x + GELU_A * x * x * x))
    return 0.5 * (1.0 + t) + 0.5 * x * (1.0 - t * t) * GELU_C * (1.0 + 3.0 * GELU_A * x * x)


def _my_index():
    return 4 * lax.axis_index("x") + 2 * lax.axis_index("y") + lax.axis_index("c")


_COMM_SCRATCH = [pltpu.SemaphoreType.DMA((N_DEV - 1,)), pltpu.SemaphoreType.DMA((N_DEV - 1,)),
                 pltpu.SemaphoreType.DMA]
_HBM = pl.BlockSpec(memory_space=pltpu.HBM)


def all_gather(xs, axis, name):
    nd = xs.ndim
    n = xs.shape[axis]
    full = list(xs.shape)
    full[axis] = n * N_DEV

    def body(x_ref, out_ref, send_sems, recv_sems, local_sem):
        x, y, c = lax.axis_index("x"), lax.axis_index("y"), lax.axis_index("c")
        me, sibling = (x, y, c), (x, y, 1 - c)
        chips = [(1 - x, y), (x, 1 - y), (1 - x, 1 - y)]

        def rows(px, py, pc):
            sl = [slice(None)] * nd
            sl[axis] = pl.ds((4 * px + 2 * py + pc) * n, n)
            return out_ref.at[tuple(sl)]

        def copy(k, block, to, src=None):
            return pltpu.make_async_remote_copy(
                src_ref=rows(*block) if src is None else src, dst_ref=rows(*block),
                send_sem=send_sems.at[k], recv_sem=recv_sems.at[k], device_id=to,
                device_id_type=pl.DeviceIdType.MESH)

        mine = pltpu.make_async_copy(x_ref, rows(*me), local_sem)
        mine.start()
        first = [copy(0, me, sibling, src=x_ref)]
        first += [copy(1 + j, me, (*chip, c), src=x_ref) for j, chip in enumerate(chips)]
        for cp in first:
            cp.start()
        passed = [copy(4 + j, (*chip, c), sibling) for j, chip in enumerate(chips)]
        for j, chip in enumerate(chips):
            copy(1 + j, (*chip, c), me).wait_recv()
            passed[j].start()
        copy(0, sibling, me).wait_recv()
        for j, chip in enumerate(chips):
            copy(4 + j, (*chip, 1 - c), me).wait_recv()
        for cp in first + passed:
            cp.wait_send()
        mine.wait()

    return pl.pallas_call(
        body, name=name, out_shape=jax.ShapeDtypeStruct(tuple(full), xs.dtype),
        in_specs=[_HBM], out_specs=_HBM, scratch_shapes=_COMM_SCRATCH,
    )(xs)


N_CHIPS = 4


def sibling_exchange(g, axis, name):
    nd = g.ndim
    n = g.shape[axis] // N_DEV
    blk = list(g.shape)
    blk[axis] = n

    def body(g_ref, mine_ref, theirs_ref, send_sems, recv_sems, local_sems):
        x, y, c = lax.axis_index("x"), lax.axis_index("y"), lax.axis_index("c")

        def block(idx):
            sl = [slice(None)] * nd
            sl[axis] = pl.ds(idx * n, n)
            return g_ref.at[tuple(sl)]

        def remote(q):
            return pltpu.make_async_remote_copy(
                src_ref=block(2 * q + 1 - c), dst_ref=theirs_ref.at[q], send_sem=send_sems.at[q],
                recv_sem=recv_sems.at[q], device_id=(x, y, 1 - c), device_id_type=pl.DeviceIdType.MESH)

        local = [pltpu.make_async_copy(block(2 * q + c), mine_ref.at[q], local_sems.at[q])
                 for q in range(N_CHIPS)]
        sends = [remote(q) for q in range(N_CHIPS)]
        for cp in local + sends:
            cp.start()
        for cp in sends:
            cp.wait_send()
            cp.wait_recv()
        for cp in local:
            cp.wait()

    shape = jax.ShapeDtypeStruct((N_CHIPS, *blk), g.dtype)
    return pl.pallas_call(
        body, name=name, out_shape=[shape, shape], in_specs=[_HBM], out_specs=[_HBM, _HBM],
        scratch_shapes=[pltpu.SemaphoreType.DMA((N_CHIPS,))] * 3,
    )(g)


def owner_exchange(part, name):
    def body(p_ref, out_ref, send_sems, recv_sems, local_sem):
        x, y, c = lax.axis_index("x"), lax.axis_index("y"), lax.axis_index("c")
        my_chip = 2 * x + y
        mine = pltpu.make_async_copy(p_ref.at[my_chip], out_ref.at[my_chip], local_sem)
        mine.start()

        def other(k):
            px = 1 - x if (k >> 1) & 1 else x
            py = 1 - y if k & 1 else y
            return (px, py, c), 2 * px + py

        def copy(k, src_slot, dst_slot):
            return pltpu.make_async_remote_copy(
                src_ref=p_ref.at[src_slot], dst_ref=out_ref.at[dst_slot], send_sem=send_sems.at[k - 1],
                recv_sem=recv_sems.at[k - 1], device_id=other(k)[0], device_id_type=pl.DeviceIdType.MESH)

        sends = [copy(k, other(k)[1], my_chip) for k in range(1, N_CHIPS)]
        for cp in sends:
            cp.start()
        for k in range(1, N_CHIPS):
            sends[k - 1].wait_send()
            copy(k, other(k)[1], other(k)[1]).wait_recv()
        mine.wait()

    return pl.pallas_call(
        body, name=name, out_shape=jax.ShapeDtypeStruct(part.shape, part.dtype),
        in_specs=[_HBM], out_specs=_HBM,
        scratch_shapes=[pltpu.SemaphoreType.DMA((N_CHIPS - 1,)), pltpu.SemaphoreType.DMA((N_CHIPS - 1,)),
                        pltpu.SemaphoreType.DMA],
    )(part)


def chip_sum(mine, theirs, name):
    shape = mine.shape
    cols = shape[-1]

    def fn(a, b):
        return (a.astype(F32) + b.astype(F32),), ()

    out = rowwise(fn, [mine.reshape(-1, cols), theirs.reshape(-1, cols)], [], [(cols, BF16)], [], tl=512,
                  name=name)[0]
    return out.reshape(shape)


def matmul(a, b, *, ta=False, tb=False, out_dtype=F32, add=None, name):
    m = a.shape[1] if ta else a.shape[0]
    kk = a.shape[0] if ta else a.shape[1]
    n = b.shape[0] if tb else b.shape[1]
    tm = _pick(m, (1024, 512, 256, 128))
    tn = _pick(n, (1024, 512, 256, 128))
    tk = _pick(kk, (2048, 1408, 1024, 512, 256, 128))
    nk = kk // tk
    dims = (((0,) if ta else (1,), (1,) if tb else (0,)), ((), ()))

    def body(*refs):
        a_ref, b_ref = refs[0], refs[1]
        add_ref = None if add is None else refs[2]
        o_ref = refs[2 if add is None else 3]

        def finish(r):
            if add is not None:
                r = r + add_ref[...].astype(F32)
            o_ref[...] = r.astype(out_dtype)

        part = lax.dot_general(a_ref[...], b_ref[...], dims, preferred_element_type=F32)
        if nk == 1:
            finish(part)
            return
        acc = refs[-1]
        k = pl.program_id(2)

        @pl.when(k == 0)
        def _():
            acc[...] = part

        @pl.when(k > 0)
        def _():
            acc[...] += part

        @pl.when(k == nk - 1)
        def _():
            finish(acc[...])

    a_spec = (pl.BlockSpec((tk, tm), lambda i, j, k: (k, i)) if ta
              else pl.BlockSpec((tm, tk), lambda i, j, k: (i, k)))
    b_spec = (pl.BlockSpec((tn, tk), lambda i, j, k: (j, k)) if tb
              else pl.BlockSpec((tk, tn), lambda i, j, k: (k, j)))
    o_spec = pl.BlockSpec((tm, tn), lambda i, j, k: (i, j))
    ins = [a, b] + ([] if add is None else [add])
    in_specs = [a_spec, b_spec] + ([] if add is None else [o_spec])
    return pl.pallas_call(
        body, name=name, out_shape=jax.ShapeDtypeStruct((m, n), out_dtype),
        grid=(m // tm, n // tn, nk), in_specs=in_specs, out_specs=o_spec,
        scratch_shapes=[] if nk == 1 else [pltpu.VMEM((tm, tn), F32)],
        compiler_params=_params(("parallel", "parallel", "arbitrary")),
    )(*ins)


def rowwise(fn, ins, rows, outs, accs, *, tl, name):
    n_rows = ins[0].shape[0]
    tl = _pick(n_rows, tuple(tl >> s for s in range(tl.bit_length() - 3)))
    n_in, n_row, n_out, n_acc = len(ins), len(rows), len(outs), len(accs)

    def body(*refs):
        in_refs = refs[:n_in]
        row_refs = refs[n_in:n_in + n_row]
        out_refs = refs[n_in + n_row:n_in + n_row + n_out]
        acc_refs = refs[n_in + n_row + n_out:]
        res_outs, res_accs = fn(*[r[...] for r in in_refs], *[r[...] for r in row_refs])
        for r, v in zip(out_refs, res_outs):
            r[...] = v.astype(r.dtype)
        if n_acc:
            @pl.when(pl.program_id(0) == 0)
            def _():
                for r in acc_refs:
                    r[...] = jnp.zeros_like(r)

            for r, v in zip(acc_refs, res_accs):
                r[...] += v

    in_specs = [pl.BlockSpec((tl, a.shape[1]), lambda i: (i, 0)) for a in ins]
    in_specs += [pl.BlockSpec((1, a.shape[1]), lambda i: (0, 0)) for a in rows]
    out_specs = [pl.BlockSpec((tl, w), lambda i: (i, 0)) for w, _ in outs]
    out_specs += [pl.BlockSpec((1, w), lambda i: (0, 0)) for w in accs]
    out_shape = [jax.ShapeDtypeStruct((n_rows, w), dt) for w, dt in outs]
    out_shape += [jax.ShapeDtypeStruct((1, w), F32) for w in accs]
    res = pl.pallas_call(
        body, name=name, out_shape=out_shape, grid=(n_rows // tl,), in_specs=in_specs,
        out_specs=out_specs, compiler_params=_params(("arbitrary",)),
    )(*ins, *rows)
    return res


def _rsum(v):
    return jnp.sum(v, axis=0, keepdims=True)


def norm_fwd(x, g, name):
    def fn(xt, gt):
        r = lax.rsqrt(jnp.mean(xt * xt, axis=-1, keepdims=True) + NORM_EPS)
        return (xt * r * gt,), ()

    d = x.shape[1]
    return rowwise(fn, [x], [g], [(d, BF16)], [], tl=512, name=name)[0]


def norm_bwd(x, g, dh, dres, name):
    def fn(xt, dht, drt, gt):
        r = lax.rsqrt(jnp.mean(xt * xt, axis=-1, keepdims=True) + NORM_EPS)
        w = dht * gt
        dx = drt + (r * w - xt * (r * r * r) * jnp.mean(xt * w, axis=-1, keepdims=True))
        return (dx, dx), (_rsum(dht * xt * r),)

    d = x.shape[1]
    dx, dxb, dg = rowwise(fn, [x, dh, dres], [g], [(d, F32), (d, BF16)], [d], tl=256, name=name)
    return dx, dxb, dg


def loss_and_grad(x, g, target, name):
    d = x.shape[1]

    def fn(xt, tt, gt):
        r = lax.rsqrt(jnp.mean(xt * xt, axis=-1, keepdims=True) + NORM_EPS)
        y = xt * r * gt
        diff = y - tt
        dy = diff * (1.0 / d)
        w = dy * gt
        dx = r * w - xt * (r * r * r) * jnp.mean(xt * w, axis=-1, keepdims=True)
        return (dx, dx), (_rsum(dy * xt * r), _rsum(diff * diff))

    dx, dxb, dg, sq = rowwise(fn, [x, target], [g], [(d, F32), (d, BF16)], [d, d], tl=256, name=name)
    return dx, dxb, dg, sq


def glu_fwd(z, x, name):
    d = x.shape[1]

    def fn(zt, xt):
        return (xt + zt[:, :d] * jax.nn.sigmoid(zt[:, d:]),), ()

    return rowwise(fn, [z, x], [], [(d, F32)], [], tl=256, name=name)[0]


def glu_bwd(z, dx, name):
    d = dx.shape[1]

    def fn(zt, dxt):
        val, gate = zt[:, :d], zt[:, d:]
        s = jax.nn.sigmoid(gate)
        return (jnp.concatenate([dxt * s, dxt * val * s * (1.0 - s)], axis=1),), ()

    return rowwise(fn, [z, dx], [], [(2 * d, BF16)], [], tl=256, name=name)[0]


def add_cast(a, b, name):
    def fn(at, bt):
        return (at + bt,), ()

    return rowwise(fn, [a, b], [], [(a.shape[1], BF16)], [], tl=512, name=name)[0]


def _conv_tiles(n_rows, f):
    return min(256, n_rows), _pick(f, (512, 256, 128))


def _shifted(gate, h6, h7):
    row = lax.broadcasted_iota(jnp.int32, gate.shape, 0)
    g1 = jnp.where(row == 0, h7, pltpu.roll(gate, 1, 0))
    g2 = jnp.where(row == 0, h6, jnp.where(row == 1, h7, pltpu.roll(gate, 2, 0)))
    return g1, g2


def convglu_fwd(up, conv_w, conv_b, name):
    n_rows, f2 = up.shape
    f = f2 // 2
    tl, tc = _conv_tiles(n_rows, f)
    nc = f // tc
    hb = tl // 8

    def body(gate_ref, val_ref, halo_ref, w_ref, b_ref, o_ref):
        l = pl.program_id(1)
        live = (l > 0).astype(F32)
        h6 = halo_ref[6:7, :] * live
        h7 = halo_ref[7:8, :] * live
        gate = gate_ref[...]
        g1, g2 = _shifted(gate, h6, h7)
        gc = w_ref[0:1, :] * g2 + w_ref[1:2, :] * g1 + w_ref[2:3, :] * gate + b_ref[...]
        o_ref[...] = (_gelu(gc) * val_ref[...]).astype(o_ref.dtype)

    return pl.pallas_call(
        body, name=name, out_shape=jax.ShapeDtypeStruct((n_rows, f), BF16),
        grid=(nc, n_rows // tl),
        in_specs=[pl.BlockSpec((tl, tc), lambda c, l: (l, c)),
                  pl.BlockSpec((tl, tc), lambda c, l: (l, c + nc)),
                  pl.BlockSpec((8, tc), lambda c, l: (jnp.maximum(l * hb - 1, 0), c)),
                  pl.BlockSpec((3, tc), lambda c, l: (0, c)),
                  pl.BlockSpec((1, tc), lambda c, l: (0, c))],
        out_specs=pl.BlockSpec((tl, tc), lambda c, l: (l, c)),
        compiler_params=_params(("parallel", "arbitrary")),
    )(up, up, up, conv_w, conv_b)


def convglu_bwd(up, da, conv_w, conv_b, name):
    n_rows, f2 = up.shape
    f = f2 // 2
    tl = min(128, n_rows)
    tc = _pick(f, (512, 256, 128))
    nl = n_rows // tl
    hb = tl // 8
    last_halo = n_rows // 8 - 1
    ext = tl + 8

    def body(gate_ref, val_ref, da_ref, gprev_ref, gnext_ref, vnext_ref, danext_ref, w_ref, b_ref,
             dup_ref, dw_ref, db_ref):
        l = pl.program_id(0)
        live_prev = (l > 0).astype(F32)
        live_next = (l < nl - 1).astype(F32)

        @pl.when(l == 0)
        def _():
            dw_ref[...] = jnp.zeros_like(dw_ref)
            db_ref[...] = jnp.zeros_like(db_ref)

        for c0 in range(0, f, tc):
            cols = slice(c0, c0 + tc)
            h6 = gprev_ref[6:7, cols] * live_prev
            h7 = gprev_ref[7:8, cols] * live_prev
            w0, w1, w2 = w_ref[0:1, cols], w_ref[1:2, cols], w_ref[2:3, cols]
            gate = jnp.concatenate([gate_ref[:, cols], gnext_ref[:, cols]], axis=0)
            val = jnp.concatenate([val_ref[:, cols], vnext_ref[:, cols]], axis=0)
            dae = jnp.concatenate([da_ref[:, cols].astype(F32),
                                   danext_ref[:, cols].astype(F32) * live_next], axis=0)
            g1, g2 = _shifted(gate, h6, h7)
            gc = w0 * g2 + w1 * g1 + w2 * gate + b_ref[:, cols]
            t = jnp.tanh(GELU_C * (gc + GELU_A * gc * gc * gc))
            gelu = 0.5 * gc * (1.0 + t)
            gelu_grad = 0.5 * (1.0 + t) + 0.5 * gc * (1.0 - t * t) * GELU_C * (1.0 + 3.0 * GELU_A * gc * gc)
            dgc = dae * val * gelu_grad
            dgate = w2 * dgc + w1 * pltpu.roll(dgc, ext - 1, 0) + w0 * pltpu.roll(dgc, ext - 2, 0)
            dup_ref[:, cols] = dgate[:tl].astype(dup_ref.dtype)
            dup_ref[:, c0 + f:c0 + f + tc] = (dae * gelu)[:tl].astype(dup_ref.dtype)
            dg = dgc[:tl]
            db_ref[:, cols] += _rsum(dg)
            dw_ref[0:1, cols] += _rsum(dg * g2[:tl])
            dw_ref[1:2, cols] += _rsum(dg * g1[:tl])
            dw_ref[2:3, cols] += _rsum(dg * gate[:tl])

    def prev_blk(l):
        return jnp.maximum(l * hb - 1, 0)

    def next_blk(l):
        return jnp.minimum((l + 1) * hb, last_halo)

    dup, dw, db = pl.pallas_call(
        body, name=name,
        out_shape=[jax.ShapeDtypeStruct((n_rows, f2), BF16), jax.ShapeDtypeStruct((3, f), F32),
                   jax.ShapeDtypeStruct((1, f), F32)],
        grid=(nl,),
        in_specs=[pl.BlockSpec((tl, f), lambda l: (l, 0)),
                  pl.BlockSpec((tl, f), lambda l: (l, 1)),
                  pl.BlockSpec((tl, f), lambda l: (l, 0)),
                  pl.BlockSpec((8, f), lambda l: (prev_blk(l), 0)),
                  pl.BlockSpec((8, f), lambda l: (next_blk(l), 0)),
                  pl.BlockSpec((8, f), lambda l: (next_blk(l), 1)),
                  pl.BlockSpec((8, f), lambda l: (next_blk(l), 0)),
                  pl.BlockSpec((3, f), lambda l: (0, 0)),
                  pl.BlockSpec((1, f), lambda l: (0, 0))],
        out_specs=[pl.BlockSpec((tl, f2), lambda l: (l, 0)),
                   pl.BlockSpec((3, f), lambda l: (0, 0)),
                   pl.BlockSpec((1, f), lambda l: (0, 0))],
        compiler_params=_params(("arbitrary",)),
    )(up, up, da, up, up, up, da, conv_w, conv_b)
    return dup, dw, db


def _log_sigmoid(z):
    return jnp.minimum(z, 0.0) - jnp.log(1.0 + jnp.exp(-jnp.abs(z)))


def fgate_fwd(zf, bf, n_heads, name):
    n_rows = zf.shape[0]
    tl = min(512, n_rows)

    def body(z_ref, b_ref, c_ref, carry):
        @pl.when(pl.program_id(0) == 0)
        def _():
            carry[...] = jnp.zeros_like(carry)

        lf = _log_sigmoid(z_ref[...] + b_ref[...])
        lft = lf.T
        r = lax.broadcasted_iota(jnp.int32, (tl, tl), 0)
        c = lax.broadcasted_iota(jnp.int32, (tl, tl), 1)
        tri = (r <= c).astype(F32)
        cum = jnp.dot(lft, tri, precision=HI, preferred_element_type=F32) + carry[...]
        c_ref[...] = cum[:n_heads]
        carry[...] += jnp.sum(lft, axis=1, keepdims=True)

    return pl.pallas_call(
        body, name=name, out_shape=jax.ShapeDtypeStruct((n_heads, n_rows), F32),
        grid=(n_rows // tl,),
        in_specs=[pl.BlockSpec((tl, 128), lambda i: (i, 0)), pl.BlockSpec((1, 128), lambda i: (0, 0))],
        out_specs=pl.BlockSpec((n_heads, tl), lambda i: (0, i)),
        scratch_shapes=[pltpu.VMEM((128, 1), F32)],
        compiler_params=_params(("arbitrary",)),
    )(zf, bf)


def fgate_bwd(dc_a, dc_b, zf, bf, name):
    n_heads, n_rows = dc_a.shape
    tl = min(512, n_rows)
    nl = n_rows // tl

    def body(da_ref, db_ref, z_ref, b_ref, dz_ref, dbias_ref, carry):
        @pl.when(pl.program_id(0) == 0)
        def _():
            carry[...] = jnp.zeros_like(carry)
            dbias_ref[...] = jnp.zeros_like(dbias_ref)

        dc = da_ref[...] + db_ref[...]
        dcp = jnp.concatenate([dc, jnp.zeros((128 - n_heads, tl), F32)], axis=0)
        r = lax.broadcasted_iota(jnp.int32, (tl, tl), 0)
        c = lax.broadcasted_iota(jnp.int32, (tl, tl), 1)
        later = (c >= r).astype(F32)
        local = lax.dot_general(later, dcp, (((1,), (1,)), ((), ())), precision=HI,
                                preferred_element_type=F32)
        dlf = local + carry[...]
        carry[...] += local[0:1, :]
        z = z_ref[...] + b_ref[...]
        dz = dlf * jax.nn.sigmoid(-z)
        dz_ref[...] = dz.astype(dz_ref.dtype)
        dbias_ref[...] += _rsum(dz)

    return pl.pallas_call(
        body, name=name,
        out_shape=[jax.ShapeDtypeStruct((n_rows, 128), BF16), jax.ShapeDtypeStruct((1, 128), F32)],
        grid=(nl,),
        in_specs=[pl.BlockSpec((n_heads, tl), lambda i: (0, nl - 1 - i)),
                  pl.BlockSpec((n_heads, tl), lambda i: (0, nl - 1 - i)),
                  pl.BlockSpec((tl, 128), lambda i: (nl - 1 - i, 0)),
                  pl.BlockSpec((1, 128), lambda i: (0, 0))],
        out_specs=[pl.BlockSpec((tl, 128), lambda i: (nl - 1 - i, 0)),
                   pl.BlockSpec((1, 128), lambda i: (0, 0))],
        scratch_shapes=[pltpu.VMEM((1, 128), F32)],
        compiler_params=_params(("arbitrary",)),
    )(dc_a, dc_b, zf, bf)


def _pick_row(blk, h):
    row = lax.broadcasted_iota(jnp.int32, blk.shape, 0)
    return jnp.sum(jnp.where(row == h, blk, 0.0), axis=0, keepdims=True)


def _pick_first(blk, h):
    r = lax.broadcasted_iota(jnp.int32, blk.shape, 0)
    c = lax.broadcasted_iota(jnp.int32, blk.shape, 1)
    return jnp.sum(jnp.where((r == h) & (c == 0), blk, 0.0), keepdims=True)


def _causal(s, keys_on_rows=False):
    r = lax.broadcasted_iota(jnp.int32, s.shape, 0)
    c = lax.broadcasted_iota(jnp.int32, s.shape, 1)
    return jnp.where((r <= c) if keys_on_rows else (c <= r), s, MASK_VALUE)


def _attn_tile(n_rows):
    return min(512, n_rows)


def attn_layouts(a):
    n_heads, n_rows = a.shape
    tq = _attn_tile(n_rows)
    return a.reshape(n_heads, n_rows // tq, tq).transpose(1, 0, 2), a.reshape(n_heads, n_rows, 1)


def flash_fwd(q, k, v, c_rows, name):
    n_rows, d = q.shape
    n_heads = d // HEAD_DIM
    tq = _attn_tile(n_rows)
    nq = n_rows // tq
    scale = HEAD_DIM ** -0.5

    def body(q_ref, k_ref, v_ref, c_ref, o_ref, o32_ref, lse_ref):
        h, i = pl.program_id(0), pl.program_id(1)
        qv = q_ref[...]
        c_q = _pick_first(c_ref[i], h)

        def block(j, carry, diagonal):
            m_prev, l_prev, acc = carry
            rows = pl.ds(pl.multiple_of(j * tq, tq), tq)
            s = lax.dot_general(qv, k_ref[rows, :], (((1,), (1,)), ((), ())),
                                preferred_element_type=F32) * scale
            s = s + (c_q - _pick_row(c_ref[j], h))
            if diagonal:
                s = _causal(s)
            m_new = jnp.maximum(m_prev, jnp.max(s, axis=1, keepdims=True))
            p = jnp.exp(s - m_new)
            alpha = jnp.exp(m_prev - m_new)
            p_hi = p.astype(BF16)
            p_lo = (p - p_hi.astype(F32)).astype(BF16)
            vv = v_ref[rows, :]
            pv = (jnp.dot(p_hi, vv, preferred_element_type=F32)
                  + jnp.dot(p_lo, vv, preferred_element_type=F32))
            return m_new, alpha * l_prev + jnp.sum(p, axis=1, keepdims=True), alpha * acc + pv

        init = (jnp.full((tq, 1), MASK_VALUE, F32), jnp.zeros((tq, 1), F32), jnp.zeros((tq, HEAD_DIM), F32))
        carry = lax.fori_loop(0, i, lambda j, c: block(j, c, False), init)
        m_fin, l_fin, acc = block(i, carry, True)
        o = acc / l_fin
        o_ref[...] = o.astype(o_ref.dtype)
        o32_ref[...] = o
        lse_ref[0] = m_fin + jnp.log(l_fin)

    qspec = pl.BlockSpec((tq, HEAD_DIM), lambda h, i: (i, h))
    kspec = pl.BlockSpec((n_rows, HEAD_DIM), lambda h, i: (0, h))
    return pl.pallas_call(
        body, name=name,
        out_shape=[jax.ShapeDtypeStruct((n_rows, d), BF16), jax.ShapeDtypeStruct((n_rows, d), F32),
                   jax.ShapeDtypeStruct((n_heads, n_rows, 1), F32)],
        grid=(n_heads, nq),
        in_specs=[qspec, kspec, kspec, pl.BlockSpec((nq, n_heads, tq), lambda h, i: (0, 0, 0))],
        out_specs=[qspec, qspec, pl.BlockSpec((1, tq, 1), lambda h, i: (h, i, 0))],
        compiler_params=_params(("parallel", "arbitrary")),
    )(q, k, v, c_rows)


def attn_delta(do, o, name):
    n_rows, d = o.shape
    n_heads = d // HEAD_DIM
    tq = min(512, n_rows)

    def body(do_ref, o_ref, out_ref):
        out_ref[0] = jnp.sum(do_ref[...].astype(F32) * o_ref[...].astype(F32), axis=1, keepdims=True)

    spec = pl.BlockSpec((tq, HEAD_DIM), lambda i, h: (i, h))
    return pl.pallas_call(
        body, name=name, out_shape=jax.ShapeDtypeStruct((n_heads, n_rows, 1), F32),
        grid=(n_rows // tq, n_heads), in_specs=[spec, spec],
        out_specs=pl.BlockSpec((1, tq, 1), lambda i, h: (h, i, 0)),
        compiler_params=_params(("parallel", "parallel")),
    )(do, o)


def flash_bwd_dq(q, k, v, do, c_rows, lse, delta, name):
    n_rows, d = q.shape
    n_heads = d // HEAD_DIM
    tq = _attn_tile(n_rows)
    nq = n_rows // tq
    scale = HEAD_DIM ** -0.5

    def body(q_ref, k_ref, v_ref, do_ref, c_ref, lse_ref, dl_ref, dq_ref):
        h, i = pl.program_id(0), pl.program_id(1)
        qv, dov = q_ref[...], do_ref[...]
        lse_q, dl_q = lse_ref[0], dl_ref[0]
        c_q = _pick_first(c_ref[i], h)

        def block(j, acc, diagonal):
            rows = pl.ds(pl.multiple_of(j * tq, tq), tq)
            kk = k_ref[rows, :]
            s = lax.dot_general(qv, kk, (((1,), (1,)), ((), ())), preferred_element_type=F32) * scale
            s = s + (c_q - _pick_row(c_ref[j], h))
            if diagonal:
                s = _causal(s)
            p = jnp.exp(s - lse_q)
            dp = lax.dot_general(dov, v_ref[rows, :], (((1,), (1,)), ((), ())), preferred_element_type=F32)
            ds = p * (dp - dl_q)
            return acc + jnp.dot(ds.astype(BF16), kk, preferred_element_type=F32)

        acc = lax.fori_loop(0, i, lambda j, a: block(j, a, False), jnp.zeros((tq, HEAD_DIM), F32))
        acc = block(i, acc, True)
        dq_ref[...] = (acc * scale).astype(dq_ref.dtype)

    qspec = pl.BlockSpec((tq, HEAD_DIM), lambda h, i: (i, h))
    kspec = pl.BlockSpec((n_rows, HEAD_DIM), lambda h, i: (0, h))
    vec = pl.BlockSpec((1, tq, 1), lambda h, i: (h, i, 0))
    return pl.pallas_call(
        body, name=name, out_shape=jax.ShapeDtypeStruct((n_rows, d), BF16),
        grid=(n_heads, nq),
        in_specs=[qspec, kspec, kspec, qspec, pl.BlockSpec((nq, n_heads, tq), lambda h, i: (0, 0, 0)), vec, vec],
        out_specs=qspec,
        compiler_params=_params(("parallel", "arbitrary")),
    )(q, k, v, do, c_rows, lse, delta)


def flash_bwd_dkv(q, k, v, do, c_rows, c_cols, lse_rows, delta_rows, name):
    n_rows, d = q.shape
    n_heads = d // HEAD_DIM
    tq = _attn_tile(n_rows)
    nq = n_rows // tq
    scale = HEAD_DIM ** -0.5

    def body(q_ref, k_ref, v_ref, do_ref, c_ref, ck_ref, lse_ref, dl_ref, dk_ref, dv_ref, dc_ref):
        h, j = pl.program_id(0), pl.program_id(1)
        kk, vv = k_ref[...], v_ref[...]
        c_k = ck_ref[0]

        def block(i, carry, diagonal):
            dk, dv, dc = carry
            rows = pl.ds(pl.multiple_of(i * tq, tq), tq)
            qi, doi = q_ref[rows, :], do_ref[rows, :]
            st = lax.dot_general(kk, qi, (((1,), (1,)), ((), ())), preferred_element_type=F32) * scale
            st = st + (_pick_first(c_ref[i], h) - c_k)
            if diagonal:
                st = _causal(st, keys_on_rows=True)
            pt = jnp.exp(st - _pick_row(lse_ref[i], h))
            dpt = lax.dot_general(vv, doi, (((1,), (1,)), ((), ())), preferred_element_type=F32)
            dst = pt * (dpt - _pick_row(dl_ref[i], h))
            dv = dv + jnp.dot(pt.astype(BF16), doi, preferred_element_type=F32)
            dk = dk + jnp.dot(dst.astype(BF16), qi, preferred_element_type=F32)
            return dk, dv, dc - jnp.sum(dst, axis=1, keepdims=True)

        zero = jnp.zeros((tq, HEAD_DIM), F32)
        carry = block(j, (zero, zero, jnp.zeros((tq, 1), F32)), True)
        dk, dv, dc = lax.fori_loop(j + 1, nq, lambda i, c: block(i, c, False), carry)
        dk_ref[...] = dk * scale
        dv_ref[...] = dv
        dc_ref[0] = dc

    kspec = pl.BlockSpec((tq, HEAD_DIM), lambda h, j: (j, h))
    qspec = pl.BlockSpec((n_rows, HEAD_DIM), lambda h, j: (0, h))
    rows3 = pl.BlockSpec((nq, n_heads, tq), lambda h, j: (0, 0, 0))
    col = pl.BlockSpec((1, tq, 1), lambda h, j: (h, j, 0))
    return pl.pallas_call(
        body, name=name,
        out_shape=[jax.ShapeDtypeStruct((n_rows, d), F32), jax.ShapeDtypeStruct((n_rows, d), F32),
                   jax.ShapeDtypeStruct((n_heads, n_rows, 1), F32)],
        grid=(n_heads, nq),
        in_specs=[qspec, kspec, kspec, qspec, rows3, col, rows3, rows3],
        out_specs=[kspec, kspec, col],
        compiler_params=_params(("parallel", "arbitrary")),
    )(q, k, v, do, c_rows, c_cols, lse_rows, delta_rows)


S5_PARAM_SHAPES = [(1, S5_2P)] * 3 + [(S5_2P, 1)] * 3 + [(S5_GROUP, S5_2P)] * 4 + [(S5_2P, S5_TC)] * 4 + [(S5_TC, 1)]


def _tile_rows(a, reps):
    return jnp.concatenate([a] * reps, axis=0)


def _s5_build(lrl, lil, lsl, lrc, lic, lsc, cri, cir, bt2, bt2s, brit, birt, ctt, ctts):
    first = lax.broadcasted_iota(jnp.int32, (1, S5_2P), 1) < S5_STATE
    sgn = jnp.where(first, -1.0, 1.0)
    dt_l = jnp.exp(lsl)
    a_l, b_l = lrl * dt_l, lil * dt_l
    mag1 = jnp.exp(a_l)
    lbr, lbi = mag1 * jnp.cos(b_l), mag1 * jnp.sin(b_l)
    n = lrl * lrl + lil * lil
    kr = ((lbr - 1.0) * lrl + lbi * lil) / n
    ki = (lbi * lrl - (lbr - 1.0) * lil) / n
    blk = lax.shift_right_logical(lax.broadcasted_iota(jnp.int32, (S5_TC, 1), 0), 4).astype(F32)
    e_end = (S5_CHUNK - 1.0) - blk
    mag_e = jnp.exp(e_end * a_l)
    pr, pi = mag_e * jnp.cos(e_end * b_l), mag_e * jnp.sin(e_end * b_l)
    cr, ci = pr * kr - pi * ki, pr * ki + pi * kr
    w_end = cr * _tile_rows(bt2, S5_CHUNK) + (sgn * ci) * _tile_rows(bt2s, S5_CHUNK)
    mag_j = jnp.exp(blk * a_l)
    qr, qi = mag_j * jnp.cos(blk * b_l), mag_j * jnp.sin(blk * b_l)
    xmat = qr * _tile_rows(cri, S5_CHUNK) + (sgn * qi) * _tile_rows(cir, S5_CHUNK)
    dt_c = jnp.exp(lsc)
    a_c, b_c = lrc * dt_c, lic * dt_c
    m1c = jnp.exp(a_c)
    lbrc, lbic = m1c * jnp.cos(b_c), m1c * jnp.sin(b_c)
    n_c = lrc * lrc + lic * lic
    krc = ((lbrc - 1.0) * lrc + lbic * lic) / n_c
    kic = (lbic * lrc - (lbrc - 1.0) * lic) / n_c
    top = lax.broadcasted_iota(jnp.int32, (S5_2P, 1), 0) < S5_STATE
    rhs = jnp.where(top, krc, -krc) * brit + (-kic) * birt
    z = jnp.dot(xmat, rhs, precision=HI, preferred_element_type=F32)
    e_out = (lax.shift_right_logical(lax.broadcasted_iota(jnp.int32, (1, S5_TC), 1), 4) + 1).astype(F32)
    mag_o = jnp.exp(a_c * e_out)
    o_r, o_i = mag_o * jnp.cos(b_c * e_out), mag_o * jnp.sin(b_c * e_out)
    w_out = jnp.where(top, o_r, -o_r) * ctt + (-o_i) * ctts
    m16 = jnp.exp(float(S5_CHUNK) * a_l)
    a1 = m16 * jnp.cos(float(S5_CHUNK) * b_l)
    a2 = sgn * (m16 * jnp.sin(float(S5_CHUNK) * b_l))
    return z, w_end, w_out, a1, a2


def _lane_block_bit(b):
    blk = lax.shift_right_logical(lax.broadcasted_iota(jnp.int32, (S5_TC, S5_TC), 1), 4)
    return (lax.shift_right_logical(blk, b) & 1) == 1


def _toeplitz(z, dcol):
    for b in range(4):
        s = S5_GROUP << b
        moved = jnp.concatenate([jnp.zeros((s, S5_TC), F32), z[:S5_TC - s]], axis=0)
        z = jnp.where(_lane_block_bit(b), moved, z)
    r = lax.broadcasted_iota(jnp.int32, (S5_TC, S5_TC), 0)
    c = lax.broadcasted_iota(jnp.int32, (S5_TC, S5_TC), 1)
    return z + jnp.where(r == c, dcol, 0.0)


def _toeplitz_t(dt):
    r = lax.broadcasted_iota(jnp.int32, (S5_TC, S5_TC), 0)
    c = lax.broadcasted_iota(jnp.int32, (S5_TC, S5_TC), 1)
    ddcol = jnp.sum(jnp.where(r == c, dt, 0.0), axis=1, keepdims=True)
    for b in range(4):
        s = S5_GROUP << b
        moved = jnp.concatenate([dt[s:], jnp.zeros((s, S5_TC), F32)], axis=0)
        dt = jnp.where(_lane_block_bit(b), moved, dt)
    return dt, ddcol


def _s5_powers(a1, a2, n):
    first = lax.broadcasted_iota(jnp.int32, a1.shape, 1) < S5_STATE
    ar, ai = a1, jnp.where(first, -a2, a2)
    out = []
    d = 1
    while d < n:
        out.append((ar, jnp.where(first, -ai, ai)))
        ar, ai = ar * ar - ai * ai, 2.0 * ar * ai
        d *= 2
    return out


def _s5_prefix(e, pows, reverse):
    n = e.shape[0]
    row = lax.broadcasted_iota(jnp.int32, e.shape, 0)
    x = e
    for b, (p1, p2) in enumerate(pows):
        d = 1 << b
        if reverse:
            sh = jnp.where(row < n - d, pltpu.roll(x, n - d, 0), 0.0)
            x = x + p1 * sh - p2 * pltpu.roll(sh, S5_STATE, 1)
        else:
            sh = jnp.where(row >= d, pltpu.roll(x, d, 0), 0.0)
            x = x + p1 * sh + p2 * pltpu.roll(sh, S5_STATE, 1)
    return x


def _s5_scan(e, pows):
    n = e.shape[0]
    row = lax.broadcasted_iota(jnp.int32, e.shape, 0)
    return jnp.where(row >= 1, pltpu.roll(_s5_prefix(e, pows, False), 1, 0), 0.0)


def _s5_scan_t(dsin, pows):
    n = dsin.shape[0]
    row = lax.broadcasted_iota(jnp.int32, dsin.shape, 0)
    return jnp.where(row < n - 1, pltpu.roll(_s5_prefix(dsin, pows, True), n - 1, 0), 0.0)


def _dot_hi(a, b, dims):
    return lax.dot_general(a, b, (dims, ((), ())), precision=HI, preferred_element_type=F32)


_NN = ((1,), (0,))
_NT = ((1,), (1,))
_TN = ((0,), (0,))


def _s5_specs(gb):
    return [pl.BlockSpec((gb, *s), lambda i: (i, 0, 0)) for s in S5_PARAM_SHAPES]


def s5_fwd(u, prm, name):
    n_groups, nk, _ = u.shape
    gb = min(4, n_groups)
    n_prm = len(S5_PARAM_SHAPES)

    def body(*refs):
        u_ref = refs[0]
        p_refs = refs[1:1 + n_prm]
        o_ref = refs[1 + n_prm]
        sin_sc, toep_sc = refs[2 + n_prm:]
        for g in range(gb):
            z, w_end, w_out, a1, a2 = _s5_build(*[r[g] for r in p_refs[:-1]])
            toep_sc[...] = _toeplitz(z, p_refs[-1][g])
            ug = u_ref[g].astype(F32)
            sin_sc[...] = _s5_scan(_dot_hi(ug, w_end, _NN), _s5_powers(a1, a2, nk))
            y = _dot_hi(ug, toep_sc[...], _NT) + _dot_hi(sin_sc[...], w_out, _NN)
            o_ref[g] = _gelu(y).astype(o_ref.dtype)

    blk = pl.BlockSpec((gb, nk, S5_TC), lambda i: (i, 0, 0))
    return pl.pallas_call(
        body, name=name, out_shape=jax.ShapeDtypeStruct(u.shape, BF16), grid=(n_groups // gb,),
        in_specs=[blk] + _s5_specs(gb), out_specs=blk,
        scratch_shapes=[pltpu.VMEM((nk, S5_2P), F32), pltpu.VMEM((S5_TC, S5_TC), F32)],
        compiler_params=_params(("parallel",)),
    )(u, *prm)


def s5_bwd(u, dyg, prm, name):
    n_groups, nk, _ = u.shape
    gb = min(4, n_groups)
    n_prm = len(S5_PARAM_SHAPES)

    def body(*refs):
        u_ref, dyg_ref = refs[0], refs[1]
        p_refs = refs[2:2 + n_prm]
        du_ref = refs[2 + n_prm]
        dp_refs = refs[3 + n_prm:3 + 2 * n_prm]
        sin_sc, de_sc, dy_sc, toep_sc = refs[3 + 2 * n_prm:]
        for g in range(gb):
            outs, vjp = jax.vjp(_s5_build, *[r[g] for r in p_refs[:-1]])
            z, w_end, w_out, a1, a2 = outs
            toep_sc[...] = _toeplitz(z, p_refs[-1][g])
            pows = _s5_powers(a1, a2, nk)
            ug = u_ref[g].astype(F32)
            sin_sc[...] = _s5_scan(_dot_hi(ug, w_end, _NN), pows)
            y = _dot_hi(ug, toep_sc[...], _NT) + _dot_hi(sin_sc[...], w_out, _NN)
            dy_sc[...] = dyg_ref[g].astype(F32) * _gelu_grad(y)
            dy = dy_sc[...]
            de_sc[...] = _s5_scan_t(_dot_hi(dy, w_out, _NT), pows)
            de, sin = de_sc[...], sin_sc[...]
            du_ref[g] = (_dot_hi(dy, toep_sc[...], _NN) + _dot_hi(de, w_end, _NT)).astype(du_ref.dtype)
            dz, ddcol = _toeplitz_t(_dot_hi(dy, ug, _TN))
            da1 = jnp.sum(de * sin, axis=0, keepdims=True)
            da2 = jnp.sum(de * pltpu.roll(sin, S5_STATE, 1), axis=0, keepdims=True)
            grads = vjp((dz, _dot_hi(ug, de, _TN), _dot_hi(sin, dy, _TN), da1, da2))
            for r, v in zip(dp_refs[:-1], grads):
                r[g] = v
            dp_refs[-1][g] = ddcol

    blk = pl.BlockSpec((gb, nk, S5_TC), lambda i: (i, 0, 0))
    res = pl.pallas_call(
        body, name=name,
        out_shape=[jax.ShapeDtypeStruct(u.shape, BF16)]
        + [jax.ShapeDtypeStruct((n_groups, *s), F32) for s in S5_PARAM_SHAPES],
        grid=(n_groups // gb,),
        in_specs=[blk, blk] + _s5_specs(gb), out_specs=[blk] + _s5_specs(gb),
        scratch_shapes=[pltpu.VMEM((nk, S5_2P), F32), pltpu.VMEM((nk, S5_2P), F32),
                        pltpu.VMEM((nk, S5_TC), F32), pltpu.VMEM((S5_TC, S5_TC), F32)],
        compiler_params=_params(("parallel",)),
    )(u, dyg, *prm)
    return res[0], res[1:]


def s5_prepare(lam_re, lam_im, log_step, b_re, b_im, c_re, c_im, d):
    g = lam_re.shape[0]
    lane = lambda a: a.reshape(g, 1, -1)
    col = lambda a: a.reshape(g, -1, 1)
    ls = jnp.broadcast_to(log_step[:, None], (g, S5_2P))
    brt = jnp.swapaxes(b_re, 1, 2)
    bit = jnp.swapaxes(b_im, 1, 2)
    crt = jnp.swapaxes(c_re, 1, 2)
    cit = jnp.swapaxes(c_im, 1, 2)
    cat = jnp.concatenate
    return [
        lane(cat([lam_re, lam_re], 1)), lane(cat([lam_im, lam_im], 1)), lane(ls),
        col(cat([lam_re, lam_re], 1)), col(cat([lam_im, lam_im], 1)), col(ls),
        cat([c_re, c_im], 2), cat([c_im, c_re], 2),
        cat([brt, bit], 2), cat([bit, brt], 2),
        jnp.tile(cat([b_re, b_im], 1), (1, 1, S5_CHUNK)), jnp.tile(cat([b_im, b_re], 1), (1, 1, S5_CHUNK)),
        jnp.tile(cat([crt, cit], 1), (1, 1, S5_CHUNK)), jnp.tile(cat([cit, crt], 1), (1, 1, S5_CHUNK)),
        col(jnp.tile(d, (1, S5_CHUNK))),
    ]


def s5_param_grads(prm, grads):
    _, vjp = jax.vjp(s5_prepare, *prm)
    return vjp(list(grads))


def s5_to_groups(a):
    n_rows, d = a.shape
    g, nk = d // S5_GROUP, n_rows // S5_CHUNK
    return a.reshape(nk, S5_CHUNK, g, S5_GROUP).transpose(2, 0, 1, 3).reshape(g, nk, S5_TC)


def s5_from_groups(a):
    g, nk, _ = a.shape
    return a.reshape(g, nk, S5_CHUNK, S5_GROUP).transpose(1, 2, 0, 3).reshape(nk * S5_CHUNK, g * S5_GROUP)


def _adamw_math(w, g, m, v):
    m = ADAM_B1 * m + (1.0 - ADAM_B1) * g
    v = ADAM_B2 * v + (1.0 - ADAM_B2) * (g * g)
    m_hat = m / (1.0 - ADAM_B1 ** ADAM_STEP)
    v_hat = v / (1.0 - ADAM_B2 ** ADAM_STEP)
    delta = -ADAM_LR * (m_hat / (jnp.sqrt(v_hat) + ADAM_EPS) + ADAM_WD * w)
    return delta, m, v


def adamw_sum(parts, w, m, v, name):
    r, c = w.shape
    n_parts = parts.shape[0]
    tr = _pick(r, (128, 64, 32, 16, 8))

    def body(p_ref, w_ref, m_ref, v_ref, g_out, d_out, m_out, v_out):
        g = p_ref[0].astype(F32)
        for s in range(1, n_parts):
            g = g + p_ref[s].astype(F32)
        delta, m_new, v_new = _adamw_math(w_ref[...], g, m_ref[...], v_ref[...])
        g_out[...] = g
        d_out[...] = delta
        m_out[...] = m_new
        v_out[...] = v_new

    spec = pl.BlockSpec((tr, c), lambda i: (i, 0))
    return pl.pallas_call(
        body, name=name, out_shape=[jax.ShapeDtypeStruct((r, c), F32)] * 4, grid=(r // tr,),
        in_specs=[pl.BlockSpec((n_parts, tr, c), lambda i: (0, i, 0)), spec, spec, spec],
        out_specs=[spec] * 4, compiler_params=_params(("parallel",)),
    )(parts, w, m, v)


def sum8(parts, name):
    r, c = parts.shape[1:]
    tr = _pick(r, (256, 128, 64, 32, 16, 8))

    def body(p_ref, o_ref):
        g = p_ref[0]
        for s in range(1, N_DEV):
            g = g + p_ref[s]
        o_ref[...] = g

    return pl.pallas_call(
        body, name=name, out_shape=jax.ShapeDtypeStruct((r, c), F32), grid=(r // tr,),
        in_specs=[pl.BlockSpec((N_DEV, tr, c), lambda i: (0, i, 0))],
        out_specs=pl.BlockSpec((tr, c), lambda i: (i, 0)), compiler_params=_params(("parallel",)),
    )(parts)


def adamw_plain(g, w, m, v, name):
    r, c = w.shape
    tr = _pick(r, (256, 128, 64, 32, 16, 8))

    def body(g_ref, w_ref, m_ref, v_ref, d_out, m_out, v_out):
        delta, m_new, v_new = _adamw_math(w_ref[...], g_ref[...], m_ref[...], v_ref[...])
        d_out[...] = delta
        m_out[...] = m_new
        v_out[...] = v_new

    spec = pl.BlockSpec((tr, c), lambda i: (i, 0))
    return pl.pallas_call(
        body, name=name, out_shape=[jax.ShapeDtypeStruct((r, c), F32)] * 3, grid=(r // tr,),
        in_specs=[spec] * 4, out_specs=[spec] * 3, compiler_params=_params(("parallel",)),
    )(g, w, m, v)


PACK_LANES = 1024


def _pack(arrays):
    pieces = []
    for a in arrays:
        flat = a.reshape(-1).astype(F32)
        pieces.append(jnp.pad(flat, (0, (-flat.shape[0]) % PACK_LANES)))
    flat = jnp.concatenate(pieces)
    return jnp.pad(flat, (0, (-flat.shape[0]) % (8 * PACK_LANES))).reshape(-1, PACK_LANES)


def _unpack(buf, shapes):
    flat = buf.reshape(-1)
    out, off = [], 0
    for s in shapes:
        n = math.prod(s)
        out.append(flat[off:off + n].reshape(s))
        off += n + ((-n) % PACK_LANES)
    return out


WEIGHT_NAMES = ['a_norm', 'a_w_in', 'a_lambda_re', 'a_lambda_im', 'a_log_step', 'a_b_re', 'a_b_im', 'a_c_re',
                'a_c_im', 'a_d', 'a_w_glu', 'kv_norm', 'w_k', 'w_v', 'w_f', 'b_f', 'b_norm', 'b_w_q', 'b_w_o',
                'ffn_norm', 'ffn_w_up', 'ffn_conv_w', 'ffn_conv_b', 'ffn_w_down', 'final_norm']
BIG = {'a_w_in': 0, 'a_w_glu': 1, 'w_k': 0, 'w_v': 0, 'b_w_q': 0, 'b_w_o': 0, 'ffn_w_up': 1, 'ffn_w_down': 0}
SMALL_SHARDED = {'a_norm': 1, 'w_f': 0, 'ffn_conv_w': 2}


def _ffn_fwd(x, gain, w_up, conv_w, conv_b, w_down, tag):
    h = norm_fwd(x, gain, f"norm_fwd_{tag}")
    up = matmul(h, w_up, name=f"mm_up_{tag}")
    a = convglu_fwd(up, conv_w, conv_b, f"convglu_fwd_{tag}")
    x_out = matmul(a, w_down, add=x, name=f"mm_down_{tag}")
    return x_out, (x, h, up, a)


def _ffn_bwd(dx, dxb, saved, gain, w_up, conv_w, conv_b, w_down, tag):
    x, h, up, a = saved
    dw_down = matmul(a, dxb, ta=True, out_dtype=BF16, name=f"mm_dwdown_{tag}")
    da = matmul(dxb, w_down, tb=True, name=f"mm_da_{tag}")
    dup, dconv_w, dconv_b = convglu_bwd(up, da, conv_w, conv_b, f"convglu_bwd_{tag}")
    dw_up = matmul(h, dup, ta=True, out_dtype=BF16, name=f"mm_dwup_{tag}")
    dh = matmul(dup, w_up, tb=True, name=f"mm_dh_{tag}")
    dx_in, dxb_in, dgain = norm_bwd(x, gain, dh, dx, f"norm_bwd_{tag}")
    return dx_in, dxb_in, dict(norm=dgain, w_up=dw_up, conv_w=dconv_w, conv_b=dconv_b, w_down=dw_down)


def kernel(x, a_norm, a_w_in, a_lambda_re, a_lambda_im, a_log_step, a_b_re, a_b_im, a_c_re, a_c_im, a_d, a_w_glu, kv_norm, w_k, w_v, w_f, b_f, b_norm, b_w_q, b_w_o, ffn_norm, ffn_w_up, ffn_conv_w, ffn_conv_b, ffn_w_down, final_norm, loss_target, m_a_norm, m_a_w_in, m_a_lambda_re, m_a_lambda_im, m_a_log_step, m_a_b_re, m_a_b_im, m_a_c_re, m_a_c_im, m_a_d, m_a_w_glu, m_kv_norm, m_w_k, m_w_v, m_w_f, m_b_f, m_b_norm, m_b_w_q, m_b_w_o, m_ffn_norm, m_ffn_w_up, m_ffn_conv_w, m_ffn_conv_b, m_ffn_w_down, m_final_norm, v_a_norm, v_a_w_in, v_a_lambda_re, v_a_lambda_im, v_a_log_step, v_a_b_re, v_a_b_im, v_a_c_re, v_a_c_im, v_a_d, v_a_w_glu, v_kv_norm, v_w_k, v_w_v, v_w_f, v_b_f, v_b_norm, v_b_w_q, v_b_w_o, v_ffn_norm, v_ffn_w_up, v_ffn_conv_w, v_ffn_conv_b, v_ffn_w_down, v_final_norm):
    args = locals()
    w = {n: args[n] for n in WEIGHT_NAMES}
    mom = {n: args["m_" + n] for n in WEIGHT_NAMES}
    var = {n: args["v_" + n] for n in WEIGHT_NAMES}
    x0 = x[0]
    target = loss_target[0]
    n_rows, d = x0.shape
    n_heads = d // HEAD_DIM
    n_a = a_w_in.shape[0]
    n_b = b_w_q.shape[0]
    me = _my_index()

    def gathered(name, layer=None):
        ws = w[name] if layer is None else w[name][layer]
        tag = name if layer is None else f"{name}{layer}"
        return all_gather(ws.astype(BF16), BIG[name], f"ag_{tag}")

    small_local = [w[n] for n in SMALL_SHARDED]
    small_full = all_gather(_pack(small_local)[None], 0, "ag_small")
    per_dev = [_unpack(small_full[s], [a.shape for a in small_local]) for s in range(N_DEV)]
    full_small = {n: jnp.concatenate([per_dev[s][i] for s in range(N_DEV)], axis=SMALL_SHARDED[n])
                  for i, n in enumerate(SMALL_SHARDED)}
    a_norm_f, w_f_f, conv_w_f = full_small['a_norm'], full_small['w_f'], full_small['ffn_conv_w']
    w_f_pad = jnp.pad(w_f_f, ((0, 0), (0, 128 - n_heads))).astype(BF16)
    b_f_pad = jnp.pad(b_f, (0, 128 - n_heads)).reshape(1, 128)

    grads = {}
    big_grads = {}

    xs = x0
    saved_a, saved_b, saved_ffn = [], [], []
    weights_a, weights_b, weights_ffn = [], [], []
    for i in range(n_a):
        w_in = gathered('a_w_in', i)
        w_glu = gathered('a_w_glu', i)
        prm = s5_prepare(a_lambda_re[i], a_lambda_im[i], a_log_step[i], a_b_re[i], a_b_im[i],
                         a_c_re[i], a_c_im[i], a_d[i])
        gain = a_norm_f[i].reshape(1, d)
        h = norm_fwd(xs, gain, f"norm_fwd_a{i}")
        u = s5_to_groups(matmul(h, w_in, out_dtype=BF16, name=f"mm_win_{i}"))
        yg = s5_from_groups(s5_fwd(u, prm, f"s5_fwd_{i}"))
        z = matmul(yg, w_glu, name=f"mm_wglu_{i}")
        x_mid = glu_fwd(z, xs, f"glu_fwd_{i}")
        saved_a.append((xs, h, u, yg, z))
        weights_a.append((w_in, w_glu, prm, gain))
        fw = (ffn_norm[i].reshape(1, d), gathered('ffn_w_up', i), conv_w_f[i], ffn_conv_b[i].reshape(1, -1),
              gathered('ffn_w_down', i))
        xs, sv = _ffn_fwd(x_mid, *fw, f"f{i}")
        saved_ffn.append(sv)
        weights_ffn.append(fw)

    x_kv = xs
    kv_gain = kv_norm.reshape(1, d)
    wk_f, wv_f = gathered('w_k'), gathered('w_v')
    h_kv = norm_fwd(x_kv, kv_gain, "norm_fwd_kv")
    k = matmul(h_kv, wk_f, out_dtype=BF16, name="mm_wk")
    v = matmul(h_kv, wv_f, out_dtype=BF16, name="mm_wv")
    zf = matmul(h_kv, w_f_pad, name="mm_wf")
    ct = fgate_fwd(zf, b_f_pad, n_heads, "fgate_fwd")
    c_rows, c_cols = attn_layouts(ct)

    for j in range(n_b):
        li = n_a + j
        w_q, w_o = gathered('b_w_q', j), gathered('b_w_o', j)
        gain = b_norm[j].reshape(1, d)
        h = norm_fwd(xs, gain, f"norm_fwd_b{j}")
        q = matmul(h, w_q, out_dtype=BF16, name=f"mm_wq_{j}")
        o, o32, lse = flash_fwd(q, k, v, c_rows, f"flash_fwd_{j}")
        x_mid = matmul(o, w_o, add=xs, name=f"mm_wo_{j}")
        saved_b.append((xs, h, q, o, o32, lse))
        weights_b.append((w_q, w_o, gain))
        fw = (ffn_norm[li].reshape(1, d), gathered('ffn_w_up', li), conv_w_f[li], ffn_conv_b[li].reshape(1, -1),
              gathered('ffn_w_down', li))
        xs, sv = _ffn_fwd(x_mid, *fw, f"f{li}")
        saved_ffn.append(sv)
        weights_ffn.append(fw)

    dx, dxb, d_final, sq = loss_and_grad(xs, final_norm.reshape(1, d), target, "loss")
    loss = lax.psum(0.5 * jnp.sum(sq) / d, MESH_AXES)
    grads['final_norm'] = d_final.reshape(d)

    ffn_g = [None] * (n_a + n_b)
    b_g = [None] * n_b
    dk_tot = dv_tot = dc_tot = None
    for j in reversed(range(n_b)):
        li = n_a + j
        dx, dxb, ffn_g[li] = _ffn_bwd(dx, dxb, saved_ffn[li], *weights_ffn[li], f"f{li}")
        xs_in, h, q, o, o32, lse = saved_b[j]
        w_q, w_o, gain = weights_b[j]
        dw_o = matmul(o, dxb, ta=True, out_dtype=BF16, name=f"mm_dwo_{j}")
        do = matmul(dxb, w_o, tb=True, out_dtype=BF16, name=f"mm_do_{j}")
        delta = attn_delta(do, o32, f"attn_delta_{j}")
        dq = flash_bwd_dq(q, k, v, do, c_rows, lse, delta, f"flash_dq_{j}")
        lse_rows = attn_layouts(lse.reshape(n_heads, n_rows))[0]
        delta_rows = attn_layouts(delta.reshape(n_heads, n_rows))[0]
        dk_j, dv_j, dc_j = flash_bwd_dkv(q, k, v, do, c_rows, c_cols, lse_rows, delta_rows, f"flash_dkv_{j}")
        dc_j = dc_j.reshape(n_heads, n_rows)
        if dk_tot is None:
            dk_tot, dv_tot, dc_tot = [dk_j], [dv_j], [dc_j]
        else:
            dk_tot.append(dk_j), dv_tot.append(dv_j), dc_tot.append(dc_j)
        dw_q = matmul(h, dq, ta=True, out_dtype=BF16, name=f"mm_dwq_{j}")
        dh = matmul(dq, w_q, tb=True, name=f"mm_dhq_{j}")
        dx, dxb, dgain = norm_bwd(xs_in, gain, dh, dx, f"norm_bwd_b{j}")
        b_g[j] = dict(norm=dgain, w_q=dw_q, w_o=dw_o)

    def total(parts, tag):
        if len(parts) == 1:
            return parts[0].astype(BF16)
        acc = parts[0]
        for n_, p_ in enumerate(parts[1:-1]):
            acc = acc + p_
        return add_cast(acc, parts[-1], f"add_{tag}")

    dk_b, dv_b = total(dk_tot, "dk"), total(dv_tot, "dv")
    dc_a = dc_tot[0]
    dc_b = dc_tot[1] if len(dc_tot) > 1 else jnp.zeros_like(dc_a)
    for extra in dc_tot[2:]:
        dc_b = dc_b + extra
    dzf, db_f = fgate_bwd(dc_a, dc_b, zf, b_f_pad, "fgate_bwd")
    big_grads[('w_k', None)] = matmul(h_kv, dk_b, ta=True, out_dtype=BF16, name="mm_dwk")
    big_grads[('w_v', None)] = matmul(h_kv, dv_b, ta=True, out_dtype=BF16, name="mm_dwv")
    dw_f = matmul(h_kv, dzf, ta=True, name="mm_dwf")[:, :n_heads]
    dh_kv = matmul(dk_b, wk_f, tb=True, name="mm_dhk")
    dh_kv = matmul(dv_b, wv_f, tb=True, add=dh_kv, name="mm_dhv")
    dh_kv = matmul(dzf, w_f_pad, tb=True, add=dh_kv, name="mm_dhf")
    dx, dxb, d_kv_gain = norm_bwd(x_kv, kv_gain, dh_kv, dx, "norm_bwd_kv")
    grads['kv_norm'] = d_kv_gain.reshape(d)
    grads['b_f'] = db_f[0, :n_heads]

    a_g = [None] * n_a
    for i in reversed(range(n_a)):
        dx, dxb, ffn_g[i] = _ffn_bwd(dx, dxb, saved_ffn[i], *weights_ffn[i], f"f{i}")
        xs_in, h, u, yg, z = saved_a[i]
        w_in, w_glu, prm, gain = weights_a[i]
        dz = glu_bwd(z, dx, f"glu_bwd_{i}")
        dw_glu = matmul(yg, dz, ta=True, out_dtype=BF16, name=f"mm_dwglu_{i}")
        dyg = s5_to_groups(matmul(dz, w_glu, tb=True, out_dtype=BF16, name=f"mm_dyg_{i}"))
        du_g, dprm = s5_bwd(u, dyg, prm, f"s5_bwd_{i}")
        du = s5_from_groups(du_g)
        dw_in = matmul(h, du, ta=True, out_dtype=BF16, name=f"mm_dwin_{i}")
        dh = matmul(du, w_in, tb=True, name=f"mm_dhin_{i}")
        dx, dxb, dgain = norm_bwd(xs_in, gain, dh, dx, f"norm_bwd_a{i}")
        a_g[i] = dict(norm=dgain, w_in=dw_in, w_glu=dw_glu, prm=s5_param_grads(
            (a_lambda_re[i], a_lambda_im[i], a_log_step[i], a_b_re[i], a_b_im[i], a_c_re[i], a_c_im[i],
             a_d[i]), dprm))
    grad_x = dx[None]

    for i in range(n_a):
        big_grads[('a_w_in', i)] = a_g[i]['w_in']
        big_grads[('a_w_glu', i)] = a_g[i]['w_glu']
    for j in range(n_b):
        big_grads[('b_w_q', j)] = b_g[j]['w_q']
        big_grads[('b_w_o', j)] = b_g[j]['w_o']
    for li in range(n_a + n_b):
        big_grads[('ffn_w_up', li)] = ffn_g[li]['w_up']
        big_grads[('ffn_w_down', li)] = ffn_g[li]['w_down']
    stack = lambda parts: jnp.stack(parts, axis=0)
    grads['a_norm'] = stack([a_g[i]['norm'].reshape(d) for i in range(n_a)])
    for pi, pname in enumerate(['a_lambda_re', 'a_lambda_im', 'a_log_step', 'a_b_re', 'a_b_im', 'a_c_re',
                                'a_c_im', 'a_d']):
        grads[pname] = stack([a_g[i]['prm'][pi] for i in range(n_a)])
    grads['w_f'] = dw_f
    grads['b_norm'] = stack([b_g[j]['norm'].reshape(d) for j in range(n_b)])
    grads['ffn_norm'] = stack([g_['norm'].reshape(d) for g_ in ffn_g])
    grads['ffn_conv_w'] = stack([g_['conv_w'] for g_ in ffn_g])
    grads['ffn_conv_b'] = stack([g_['conv_b'].reshape(-1) for g_ in ffn_g])

    small_names = [n for n in WEIGHT_NAMES if n not in BIG]
    small_shapes = [grads[n].shape for n in small_names]
    g_parts = all_gather(_pack([grads[n] for n in small_names])[None], 0, "ag_grads_small")
    g_sum = _unpack(sum8(g_parts, "sum_grads_small"), small_shapes)
    g_full = dict(zip(small_names, g_sum))
    g_local = {}
    for n in small_names:
        if n in SMALL_SHARDED:
            ax = SMALL_SHARDED[n]
            size = w[n].shape[ax]
            g_local[n] = lax.dynamic_slice_in_dim(g_full[n], me * size, size, axis=ax)
        else:
            g_local[n] = g_full[n]
    local_shapes = [w[n].shape for n in small_names]
    d_s, m_s, v_s = adamw_plain(_pack([g_local[n] for n in small_names]), _pack([w[n] for n in small_names]),
                                _pack([mom[n] for n in small_names]), _pack([var[n] for n in small_names]),
                                "adamw_small")
    out_g, out_d, out_m, out_v = dict(g_local), {}, {}, {}
    for n, dd, mm, vv in zip(small_names, _unpack(d_s, local_shapes), _unpack(m_s, local_shapes),
                             _unpack(v_s, local_shapes)):
        out_d[n], out_m[n], out_v[n] = dd, mm, vv

    for name, ax in BIG.items():
        layered = w[name].ndim == 3
        layers = range(w[name].shape[0]) if layered else [None]
        res = []
        for layer in layers:
            tag = name if layer is None else f"{name}{layer}"
            mine, theirs = sibling_exchange(big_grads[(name, layer)], ax, f"rs_d2d_{tag}")
            parts = owner_exchange(chip_sum(mine, theirs, f"rs_add_{tag}"), f"rs_ici_{tag}")
            pick = (lambda a: a) if layer is None else (lambda a: a[layer])
            res.append(adamw_sum(parts, pick(w[name]), pick(mom[name]), pick(var[name]), f"adamw_{tag}"))
        for dst, idx in ((out_g, 0), (out_d, 1), (out_m, 2), (out_v, 3)):
            dst[name] = stack([r[idx] for r in res]) if layered else res[0][idx]

    return (loss, grad_x, *[out_g[n] for n in WEIGHT_NAMES], *[out_d[n] for n in WEIGHT_NAMES],
            *[out_m[n] for n in WEIGHT_NAMES], *[out_v[n] for n in WEIGHT_NAMES])
```

```python
import functools
import math

import jax
import jax.numpy as jnp
from jax import lax
from jax.experimental import pallas as pl
from jax.experimental.pallas import tpu as pltpu

F32 = jnp.float32
BF16 = jnp.bfloat16
HI = lax.Precision.HIGHEST
MESH_AXES = ("x", "y", "c")
N_DEV = 8
VMEM_LIMIT_BYTES = 48 * 1024 * 1024

NORM_EPS = 1e-6
S5_GROUP = 16
S5_STATE = 64
S5_CHUNK = 16
S5_TC = S5_CHUNK * S5_GROUP
S5_2P = 2 * S5_STATE
HEAD_DIM = 128
MASK_VALUE = -1e30
GELU_C = math.sqrt(2.0 / math.pi)
GELU_A = 0.044715

ADAM_LR = 0.001
ADAM_B1 = 0.9
ADAM_B2 = 0.999
ADAM_EPS = 1e-08
ADAM_WD = 0.01
ADAM_STEP = 10


def _params(sem=None):
    return pltpu.CompilerParams(dimension_semantics=sem, vmem_limit_bytes=VMEM_LIMIT_BYTES)


def _pick(n, cands):
    for c in cands:
        if n % c == 0:
            return c
    return n


def _gelu(x):
    return 0.5 * x * (1.0 + jnp.tanh(GELU_C * (x + GELU_A * x * x * x)))


def _gelu_grad(x):
    t = jnp.tanh(GELU_C * (x + GELU_A * x * x * x))
    return 0.5 * (1.0 + t) + 0.5 * x * (1.0 - t * t) * GELU_C * (1.0 + 3.0 * GELU_A * x * x)


def _my_index():
    return 4 * lax.axis_index("x") + 2 * lax.axis_index("y") + lax.axis_index("c")


_COMM_SCRATCH = [pltpu.SemaphoreType.DMA((N_DEV - 1,)), pltpu.SemaphoreType.DMA((N_DEV - 1,)),
                 pltpu.SemaphoreType.DMA]
_HBM = pl.BlockSpec(memory_space=pltpu.HBM)


def all_gather(xs, axis, name):
    nd = xs.ndim
    n = xs.shape[axis]
    full = list(xs.shape)
    full[axis] = n * N_DEV

    def body(x_ref, out_ref, send_sems, recv_sems, local_sem):
        x, y, c = lax.axis_index("x"), lax.axis_index("y"), lax.axis_index("c")
        me, sibling = (x, y, c), (x, y, 1 - c)
        chips = [(1 - x, y), (x, 1 - y), (1 - x, 1 - y)]

        def rows(px, py, pc):
            sl = [slice(None)] * nd
            sl[axis] = pl.ds((4 * px + 2 * py + pc) * n, n)
            return out_ref.at[tuple(sl)]

        def copy(k, block, to, src=None):
            return pltpu.make_async_remote_copy(
                src_ref=rows(*block) if src is None else src, dst_ref=rows(*block),
                send_sem=send_sems.at[k], recv_sem=recv_sems.at[k], device_id=to,
                device_id_type=pl.DeviceIdType.MESH)

        mine = pltpu.make_async_copy(x_ref, rows(*me), local_sem)
        mine.start()
        first = [copy(0, me, sibling, src=x_ref)]
        first += [copy(1 + j, me, (*chip, c), src=x_ref) for j, chip in enumerate(chips)]
        for cp in first:
            cp.start()
        passed = [copy(4 + j, (*chip, c), sibling) for j, chip in enumerate(chips)]
        for j, chip in enumerate(chips):
            copy(1 + j, (*chip, c), me).wait_recv()
            passed[j].start()
        copy(0, sibling, me).wait_recv()
        for j, chip in enumerate(chips):
            copy(4 + j, (*chip, 1 - c), me).wait_recv()
        for cp in first + passed:
            cp.wait_send()
        mine.wait()

    return pl.pallas_call(
        body, name=name, out_shape=jax.ShapeDtypeStruct(tuple(full), xs.dtype),
        in_specs=[_HBM], out_specs=_HBM, scratch_shapes=_COMM_SCRATCH,
    )(xs)


N_CHIPS = 4


def sibling_exchange(g, axis, name):
    nd = g.ndim
    n = g.shape[axis] // N_DEV
    blk = list(g.shape)
    blk[axis] = n

    def body(g_ref, theirs_ref, send_sems, recv_sems):
        x, y, c = lax.axis_index("x"), lax.axis_index("y"), lax.axis_index("c")

        def block(idx):
            sl = [slice(None)] * nd
            sl[axis] = pl.ds(idx * n, n)
            return g_ref.at[tuple(sl)]

        sends = [pltpu.make_async_remote_copy(
            src_ref=block(2 * q + 1 - c), dst_ref=theirs_ref.at[q], send_sem=send_sems.at[q],
            recv_sem=recv_sems.at[q], device_id=(x, y, 1 - c), device_id_type=pl.DeviceIdType.MESH)
            for q in range(N_CHIPS)]
        for cp in sends:
            cp.start()
        for cp in sends:
            cp.wait_send()
            cp.wait_recv()

    return pl.pallas_call(
        body, name=name, out_shape=jax.ShapeDtypeStruct((N_CHIPS, *blk), g.dtype),
        in_specs=[_HBM], out_specs=_HBM, scratch_shapes=[pltpu.SemaphoreType.DMA((N_CHIPS,))] * 2,
    )(g)


def own_blocks(g, axis):
    c = lax.axis_index("c")
    n = g.shape[axis] // N_DEV
    shape = g.shape[:axis] + (N_CHIPS, 2, n) + g.shape[axis + 1:]
    picked = lax.dynamic_index_in_dim(g.reshape(shape), c, axis=axis + 1, keepdims=False)
    return jnp.moveaxis(picked, axis, 0)


def owner_exchange(part, name):
    def body(p_ref, out_ref, send_sems, recv_sems, local_sem):
        x, y, c = lax.axis_index("x"), lax.axis_index("y"), lax.axis_index("c")
        my_chip = 2 * x + y
        mine = pltpu.make_async_copy(p_ref.at[my_chip], out_ref.at[my_chip], local_sem)
        mine.start()

        def other(k):
            px = 1 - x if (k >> 1) & 1 else x
            py = 1 - y if k & 1 else y
            return (px, py, c), 2 * px + py

        def copy(k, src_slot, dst_slot):
            return pltpu.make_async_remote_copy(
                src_ref=p_ref.at[src_slot], dst_ref=out_ref.at[dst_slot], send_sem=send_sems.at[k - 1],
                recv_sem=recv_sems.at[k - 1], device_id=other(k)[0], device_id_type=pl.DeviceIdType.MESH)

        sends = [copy(k, other(k)[1], my_chip) for k in range(1, N_CHIPS)]
        for cp in sends:
            cp.start()
        for k in range(1, N_CHIPS):
            sends[k - 1].wait_send()
            copy(k, other(k)[1], other(k)[1]).wait_recv()
        mine.wait()

    return pl.pallas_call(
        body, name=name, out_shape=jax.ShapeDtypeStruct(part.shape, part.dtype),
        in_specs=[_HBM], out_specs=_HBM,
        scratch_shapes=[pltpu.SemaphoreType.DMA((N_CHIPS - 1,)), pltpu.SemaphoreType.DMA((N_CHIPS - 1,)),
                        pltpu.SemaphoreType.DMA],
    )(part)


def chip_sum(mine, theirs, name):
    shape = mine.shape
    cols = shape[-1]

    def fn(a, b):
        return (a.astype(F32) + b.astype(F32),), ()

    out = rowwise(fn, [mine.reshape(-1, cols), theirs.reshape(-1, cols)], [], [(cols, BF16)], [], tl=512,
                  name=name)[0]
    return out.reshape(shape)


def matmul(a, b, *, ta=False, tb=False, out_dtype=F32, add=None, name):
    m = a.shape[1] if ta else a.shape[0]
    kk = a.shape[0] if ta else a.shape[1]
    n = b.shape[0] if tb else b.shape[1]
    tm = _pick(m, (1024, 512, 256, 128))
    tn = _pick(n, (1024, 512, 256, 128))
    tk = _pick(kk, (2048, 1408, 1024, 512, 256, 128))
    nk = kk // tk
    dims = (((0,) if ta else (1,), (1,) if tb else (0,)), ((), ()))

    def body(*refs):
        a_ref, b_ref = refs[0], refs[1]
        add_ref = None if add is None else refs[2]
        o_ref = refs[2 if add is None else 3]

        def finish(r):
            if add is not None:
                r = r + add_ref[...].astype(F32)
            o_ref[...] = r.astype(out_dtype)

        part = lax.dot_general(a_ref[...], b_ref[...], dims, preferred_element_type=F32)
        if nk == 1:
            finish(part)
            return
        acc = refs[-1]
        k = pl.program_id(2)

        @pl.when(k == 0)
        def _():
            acc[...] = part

        @pl.when(k > 0)
        def _():
            acc[...] += part

        @pl.when(k == nk - 1)
        def _():
            finish(acc[...])

    a_spec = (pl.BlockSpec((tk, tm), lambda i, j, k: (k, i)) if ta
              else pl.BlockSpec((tm, tk), lambda i, j, k: (i, k)))
    b_spec = (pl.BlockSpec((tn, tk), lambda i, j, k: (j, k)) if tb
              else pl.BlockSpec((tk, tn), lambda i, j, k: (k, j)))
    o_spec = pl.BlockSpec((tm, tn), lambda i, j, k: (i, j))
    ins = [a, b] + ([] if add is None else [add])
    in_specs = [a_spec, b_spec] + ([] if add is None else [o_spec])
    return pl.pallas_call(
        body, name=name, out_shape=jax.ShapeDtypeStruct((m, n), out_dtype),
        grid=(m // tm, n // tn, nk), in_specs=in_specs, out_specs=o_spec,
        scratch_shapes=[] if nk == 1 else [pltpu.VMEM((tm, tn), F32)],
        compiler_params=_params(("parallel", "parallel", "arbitrary")),
    )(*ins)


def rowwise(fn, ins, rows, outs, accs, *, tl, name):
    n_rows = ins[0].shape[0]
    tl = _pick(n_rows, tuple(tl >> s for s in range(tl.bit_length() - 3)))
    n_in, n_row, n_out, n_acc = len(ins), len(rows), len(outs), len(accs)

    def body(*refs):
        in_refs = refs[:n_in]
        row_refs = refs[n_in:n_in + n_row]
        out_refs = refs[n_in + n_row:n_in + n_row + n_out]
        acc_refs = refs[n_in + n_row + n_out:]
        res_outs, res_accs = fn(*[r[...] for r in in_refs], *[r[...] for r in row_refs])
        for r, v in zip(out_refs, res_outs):
            r[...] = v.astype(r.dtype)
        if n_acc:
            @pl.when(pl.program_id(0) == 0)
            def _():
                for r in acc_refs:
                    r[...] = jnp.zeros_like(r)

            for r, v in zip(acc_refs, res_accs):
                r[...] += v

    in_specs = [pl.BlockSpec((tl, a.shape[1]), lambda i: (i, 0)) for a in ins]
    in_specs += [pl.BlockSpec((1, a.shape[1]), lambda i: (0, 0)) for a in rows]
    out_specs = [pl.BlockSpec((tl, w), lambda i: (i, 0)) for w, _ in outs]
    out_specs += [pl.BlockSpec((1, w), lambda i: (0, 0)) for w in accs]
    out_shape = [jax.ShapeDtypeStruct((n_rows, w), dt) for w, dt in outs]
    out_shape += [jax.ShapeDtypeStruct((1, w), F32) for w in accs]
    res = pl.pallas_call(
        body, name=name, out_shape=out_shape, grid=(n_rows // tl,), in_specs=in_specs,
        out_specs=out_specs, compiler_params=_params(("arbitrary",)),
    )(*ins, *rows)
    return res


def _rsum(v):
    return jnp.sum(v, axis=0, keepdims=True)


def norm_fwd(x, g, name):
    def fn(xt, gt):
        r = lax.rsqrt(jnp.mean(xt * xt, axis=-1, keepdims=True) + NORM_EPS)
        return (xt * r * gt,), ()

    d = x.shape[1]
    return rowwise(fn, [x], [g], [(d, BF16)], [], tl=512, name=name)[0]


def norm_bwd(x, g, dh, dres, name):
    def fn(xt, dht, drt, gt):
        r = lax.rsqrt(jnp.mean(xt * xt, axis=-1, keepdims=True) + NORM_EPS)
        w = dht * gt
        dx = drt + (r * w - xt * (r * r * r) * jnp.mean(xt * w, axis=-1, keepdims=True))
        return (dx, dx), (_rsum(dht * xt * r),)

    d = x.shape[1]
    dx, dxb, dg = rowwise(fn, [x, dh, dres], [g], [(d, F32), (d, BF16)], [d], tl=256, name=name)
    return dx, dxb, dg


def loss_and_grad(x, g, target, name):
    d = x.shape[1]

    def fn(xt, tt, gt):
        r = lax.rsqrt(jnp.mean(xt * xt, axis=-1, keepdims=True) + NORM_EPS)
        y = xt * r * gt
        diff = y - tt
        dy = diff * (1.0 / d)
        w = dy * gt
        dx = r * w - xt * (r * r * r) * jnp.mean(xt * w, axis=-1, keepdims=True)
        return (dx, dx), (_rsum(dy * xt * r), _rsum(diff * diff))

    dx, dxb, dg, sq = rowwise(fn, [x, target], [g], [(d, F32), (d, BF16)], [d, d], tl=256, name=name)
    return dx, dxb, dg, sq


def glu_fwd(z, x, name):
    d = x.shape[1]

    def fn(zt, xt):
        return (xt + zt[:, :d] * jax.nn.sigmoid(zt[:, d:]),), ()

    return rowwise(fn, [z, x], [], [(d, F32)], [], tl=256, name=name)[0]


def glu_bwd(z, dx, name):
    d = dx.shape[1]

    def fn(zt, dxt):
        val, gate = zt[:, :d], zt[:, d:]
        s = jax.nn.sigmoid(gate)
        return (jnp.concatenate([dxt * s, dxt * val * s * (1.0 - s)], axis=1),), ()

    return rowwise(fn, [z, dx], [], [(2 * d, BF16)], [], tl=256, name=name)[0]


def add_cast(a, b, name):
    def fn(at, bt):
        return (at + bt,), ()

    return rowwise(fn, [a, b], [], [(a.shape[1], BF16)], [], tl=512, name=name)[0]


def _conv_tiles(n_rows, f):
    return min(256, n_rows), _pick(f, (512, 256, 128))


def _shifted(gate, h6, h7):
    row = lax.broadcasted_iota(jnp.int32, gate.shape, 0)
    g1 = jnp.where(row == 0, h7, pltpu.roll(gate, 1, 0))
    g2 = jnp.where(row == 0, h6, jnp.where(row == 1, h7, pltpu.roll(gate, 2, 0)))
    return g1, g2


def convglu_fwd(up, conv_w, conv_b, name):
    n_rows, f2 = up.shape
    f = f2 // 2
    tl, tc = _conv_tiles(n_rows, f)
    nc = f // tc
    hb = tl // 8

    def body(gate_ref, val_ref, halo_ref, w_ref, b_ref, o_ref):
        l = pl.program_id(1)
        live = (l > 0).astype(F32)
        h6 = halo_ref[6:7, :] * live
        h7 = halo_ref[7:8, :] * live
        gate = gate_ref[...]
        g1, g2 = _shifted(gate, h6, h7)
        gc = w_ref[0:1, :] * g2 + w_ref[1:2, :] * g1 + w_ref[2:3, :] * gate + b_ref[...]
        o_ref[...] = (_gelu(gc) * val_ref[...]).astype(o_ref.dtype)

    return pl.pallas_call(
        body, name=name, out_shape=jax.ShapeDtypeStruct((n_rows, f), BF16),
        grid=(nc, n_rows // tl),
        in_specs=[pl.BlockSpec((tl, tc), lambda c, l: (l, c)),
                  pl.BlockSpec((tl, tc), lambda c, l: (l, c + nc)),
                  pl.BlockSpec((8, tc), lambda c, l: (jnp.maximum(l * hb - 1, 0), c)),
                  pl.BlockSpec((3, tc), lambda c, l: (0, c)),
                  pl.BlockSpec((1, tc), lambda c, l: (0, c))],
        out_specs=pl.BlockSpec((tl, tc), lambda c, l: (l, c)),
        compiler_params=_params(("parallel", "arbitrary")),
    )(up, up, up, conv_w, conv_b)


def convglu_bwd(up, da, conv_w, conv_b, name):
    n_rows, f2 = up.shape
    f = f2 // 2
    tl = min(128, n_rows)
    tc = _pick(f, (512, 256, 128))
    nl = n_rows // tl
    hb = tl // 8
    last_halo = n_rows // 8 - 1
    ext = tl + 8

    def body(gate_ref, val_ref, da_ref, gprev_ref, gnext_ref, vnext_ref, danext_ref, w_ref, b_ref,
             dup_ref, dw_ref, db_ref):
        l = pl.program_id(0)
        live_prev = (l > 0).astype(F32)
        live_next = (l < nl - 1).astype(F32)

        @pl.when(l == 0)
        def _():
            dw_ref[...] = jnp.zeros_like(dw_ref)
            db_ref[...] = jnp.zeros_like(db_ref)

        for c0 in range(0, f, tc):
            cols = slice(c0, c0 + tc)
            h6 = gprev_ref[6:7, cols] * live_prev
            h7 = gprev_ref[7:8, cols] * live_prev
            w0, w1, w2 = w_ref[0:1, cols], w_ref[1:2, cols], w_ref[2:3, cols]
            gate = jnp.concatenate([gate_ref[:, cols], gnext_ref[:, cols]], axis=0)
            val = jnp.concatenate([val_ref[:, cols], vnext_ref[:, cols]], axis=0)
            dae = jnp.concatenate([da_ref[:, cols].astype(F32),
                                   danext_ref[:, cols].astype(F32) * live_next], axis=0)
            g1, g2 = _shifted(gate, h6, h7)
            gc = w0 * g2 + w1 * g1 + w2 * gate + b_ref[:, cols]
            t = jnp.tanh(GELU_C * (gc + GELU_A * gc * gc * gc))
            gelu = 0.5 * gc * (1.0 + t)
            gelu_grad = 0.5 * (1.0 + t) + 0.5 * gc * (1.0 - t * t) * GELU_C * (1.0 + 3.0 * GELU_A * gc * gc)
            dgc = dae * val * gelu_grad
            dgate = w2 * dgc + w1 * pltpu.roll(dgc, ext - 1, 0) + w0 * pltpu.roll(dgc, ext - 2, 0)
            dup_ref[:, cols] = dgate[:tl].astype(dup_ref.dtype)
            dup_ref[:, c0 + f:c0 + f + tc] = (dae * gelu)[:tl].astype(dup_ref.dtype)
            dg = dgc[:tl]
            db_ref[:, cols] += _rsum(dg)
            dw_ref[0:1, cols] += _rsum(dg * g2[:tl])
            dw_ref[1:2, cols] += _rsum(dg * g1[:tl])
            dw_ref[2:3, cols] += _rsum(dg * gate[:tl])

    def prev_blk(l):
        return jnp.maximum(l * hb - 1, 0)

    def next_blk(l):
        return jnp.minimum((l + 1) * hb, last_halo)

    dup, dw, db = pl.pallas_call(
        body, name=name,
        out_shape=[jax.ShapeDtypeStruct((n_rows, f2), BF16), jax.ShapeDtypeStruct((3, f), F32),
                   jax.ShapeDtypeStruct((1, f), F32)],
        grid=(nl,),
        in_specs=[pl.BlockSpec((tl, f), lambda l: (l, 0)),
                  pl.BlockSpec((tl, f), lambda l: (l, 1)),
                  pl.BlockSpec((tl, f), lambda l: (l, 0)),
                  pl.BlockSpec((8, f), lambda l: (prev_blk(l), 0)),
                  pl.BlockSpec((8, f), lambda l: (next_blk(l), 0)),
                  pl.BlockSpec((8, f), lambda l: (next_blk(l), 1)),
                  pl.BlockSpec((8, f), lambda l: (next_blk(l), 0)),
                  pl.BlockSpec((3, f), lambda l: (0, 0)),
                  pl.BlockSpec((1, f), lambda l: (0, 0))],
        out_specs=[pl.BlockSpec((tl, f2), lambda l: (l, 0)),
                   pl.BlockSpec((3, f), lambda l: (0, 0)),
                   pl.BlockSpec((1, f), lambda l: (0, 0))],
        compiler_params=_params(("arbitrary",)),
    )(up, up, da, up, up, up, da, conv_w, conv_b)
    return dup, dw, db


def _log_sigmoid(z):
    return jnp.minimum(z, 0.0) - jnp.log(1.0 + jnp.exp(-jnp.abs(z)))


def fgate_fwd(zf, bf, n_heads, name):
    n_rows = zf.shape[0]
    tl = min(512, n_rows)

    def body(z_ref, b_ref, c_ref, carry):
        @pl.when(pl.program_id(0) == 0)
        def _():
            carry[...] = jnp.zeros_like(carry)

        lf = _log_sigmoid(z_ref[...] + b_ref[...])
        lft = lf.T
        r = lax.broadcasted_iota(jnp.int32, (tl, tl), 0)
        c = lax.broadcasted_iota(jnp.int32, (tl, tl), 1)
        tri = (r <= c).astype(F32)
        cum = jnp.dot(lft, tri, precision=HI, preferred_element_type=F32) + carry[...]
        c_ref[...] = cum[:n_heads]
        carry[...] += jnp.sum(lft, axis=1, keepdims=True)

    return pl.pallas_call(
        body, name=name, out_shape=jax.ShapeDtypeStruct((n_heads, n_rows), F32),
        grid=(n_rows // tl,),
        in_specs=[pl.BlockSpec((tl, 128), lambda i: (i, 0)), pl.BlockSpec((1, 128), lambda i: (0, 0))],
        out_specs=pl.BlockSpec((n_heads, tl), lambda i: (0, i)),
        scratch_shapes=[pltpu.VMEM((128, 1), F32)],
        compiler_params=_params(("arbitrary",)),
    )(zf, bf)


def fgate_bwd(dc_a, dc_b, zf, bf, name):
    n_heads, n_rows = dc_a.shape
    tl = min(512, n_rows)
    nl = n_rows // tl

    def body(da_ref, db_ref, z_ref, b_ref, dz_ref, dbias_ref, carry):
        @pl.when(pl.program_id(0) == 0)
        def _():
            carry[...] = jnp.zeros_like(carry)
            dbias_ref[...] = jnp.zeros_like(dbias_ref)

        dc = da_ref[...] + db_ref[...]
        dcp = jnp.concatenate([dc, jnp.zeros((128 - n_heads, tl), F32)], axis=0)
        r = lax.broadcasted_iota(jnp.int32, (tl, tl), 0)
        c = lax.broadcasted_iota(jnp.int32, (tl, tl), 1)
        later = (c >= r).astype(F32)
        local = lax.dot_general(later, dcp, (((1,), (1,)), ((), ())), precision=HI,
                                preferred_element_type=F32)
        dlf = local + carry[...]
        carry[...] += local[0:1, :]
        z = z_ref[...] + b_ref[...]
        dz = dlf * jax.nn.sigmoid(-z)
        dz_ref[...] = dz.astype(dz_ref.dtype)
        dbias_ref[...] += _rsum(dz)

    return pl.pallas_call(
        body, name=name,
        out_shape=[jax.ShapeDtypeStruct((n_rows, 128), BF16), jax.ShapeDtypeStruct((1, 128), F32)],
        grid=(nl,),
        in_specs=[pl.BlockSpec((n_heads, tl), lambda i: (0, nl - 1 - i)),
                  pl.BlockSpec((n_heads, tl), lambda i: (0, nl - 1 - i)),
                  pl.BlockSpec((tl, 128), lambda i: (nl - 1 - i, 0)),
                  pl.BlockSpec((1, 128), lambda i: (0, 0))],
        out_specs=[pl.BlockSpec((tl, 128), lambda i: (nl - 1 - i, 0)),
                   pl.BlockSpec((1, 128), lambda i: (0, 0))],
        scratch_shapes=[pltpu.VMEM((1, 128), F32)],
        compiler_params=_params(("arbitrary",)),
    )(dc_a, dc_b, zf, bf)


def _pick_row(blk, h):
    row = lax.broadcasted_iota(jnp.int32, blk.shape, 0)
    return jnp.sum(jnp.where(row == h, blk, 0.0), axis=0, keepdims=True)


def _pick_first(blk, h):
    r = lax.broadcasted_iota(jnp.int32, blk.shape, 0)
    c = lax.broadcasted_iota(jnp.int32, blk.shape, 1)
    return jnp.sum(jnp.where((r == h) & (c == 0), blk, 0.0), keepdims=True)


def _causal(s, keys_on_rows=False, row0=0, col0=0):
    r = lax.broadcasted_iota(jnp.int32, s.shape, 0) + row0
    c = lax.broadcasted_iota(jnp.int32, s.shape, 1) + col0
    return jnp.where((r <= c) if keys_on_rows else (c <= r), s, MASK_VALUE)


def _attn_tile(n_rows):
    return min(512, n_rows)


def _attn_strips(tq):
    return 1


def attn_layouts(a):
    n_heads, n_rows = a.shape
    tq = _attn_tile(n_rows)
    return a.reshape(n_heads, n_rows // tq, tq).transpose(1, 0, 2), a.reshape(n_heads, n_rows, 1)


def flash_fwd(q, k, v, c_rows, name):
    n_rows, d = q.shape
    n_heads = d // HEAD_DIM
    tq = _attn_tile(n_rows)
    nq = n_rows // tq
    scale = HEAD_DIM ** -0.5

    n_strips = _attn_strips(tq)
    ts = tq // n_strips

    def body(q_ref, k_ref, v_ref, c_ref, o_ref, o32_ref, lse_ref):
        h, i = pl.program_id(0), pl.program_id(1)
        c_q = _pick_first(c_ref[i], h)

        def block(j, carry, diagonal):
            rows = pl.ds(pl.multiple_of(j * tq, tq), tq)
            kk, vv = k_ref[rows, :], v_ref[rows, :]
            bias = c_q - _pick_row(c_ref[j], h)
            out = []
            for r in range(n_strips):
                m_prev, l_prev, acc = carry[r]
                s = lax.dot_general(q_ref[r * ts:(r + 1) * ts, :], kk, (((1,), (1,)), ((), ())),
                                    preferred_element_type=F32) * scale + bias
                if diagonal:
                    s = _causal(s, row0=r * ts)
                m_new = jnp.maximum(m_prev, jnp.max(s, axis=1, keepdims=True))
                p = jnp.exp(s - m_new)
                alpha = jnp.exp(m_prev - m_new)
                p_hi = p.astype(BF16)
                p_lo = (p - p_hi.astype(F32)).astype(BF16)
                pv = (jnp.dot(p_hi, vv, preferred_element_type=F32)
                      + jnp.dot(p_lo, vv, preferred_element_type=F32))
                out.append((m_new, alpha * l_prev + jnp.sum(p, axis=1, keepdims=True), alpha * acc + pv))
            return tuple(out)

        init = tuple((jnp.full((ts, 1), MASK_VALUE, F32), jnp.zeros((ts, 1), F32),
                      jnp.zeros((ts, HEAD_DIM), F32)) for _ in range(n_strips))
        carry = lax.fori_loop(0, i, lambda j, c: block(j, c, False), init)
        for r, (m_fin, l_fin, acc) in enumerate(block(i, carry, True)):
            o = acc / l_fin
            o_ref[r * ts:(r + 1) * ts, :] = o.astype(o_ref.dtype)
            o32_ref[r * ts:(r + 1) * ts, :] = o
            lse_ref[0, r * ts:(r + 1) * ts, :] = m_fin + jnp.log(l_fin)

    qspec = pl.BlockSpec((tq, HEAD_DIM), lambda h, i: (i, h))
    kspec = pl.BlockSpec((n_rows, HEAD_DIM), lambda h, i: (0, h))
    return pl.pallas_call(
        body, name=name,
        out_shape=[jax.ShapeDtypeStruct((n_rows, d), BF16), jax.ShapeDtypeStruct((n_rows, d), F32),
                   jax.ShapeDtypeStruct((n_heads, n_rows, 1), F32)],
        grid=(n_heads, nq),
        in_specs=[qspec, kspec, kspec, pl.BlockSpec((nq, n_heads, tq), lambda h, i: (0, 0, 0))],
        out_specs=[qspec, qspec, pl.BlockSpec((1, tq, 1), lambda h, i: (h, i, 0))],
        compiler_params=_params(("parallel", "arbitrary")),
    )(q, k, v, c_rows)


def attn_delta(do, o, name):
    n_rows, d = o.shape
    n_heads = d // HEAD_DIM
    tq = min(512, n_rows)

    def body(do_ref, o_ref, out_ref):
        out_ref[0] = jnp.sum(do_ref[...].astype(F32) * o_ref[...].astype(F32), axis=1, keepdims=True)

    spec = pl.BlockSpec((tq, HEAD_DIM), lambda i, h: (i, h))
    return pl.pallas_call(
        body, name=name, out_shape=jax.ShapeDtypeStruct((n_heads, n_rows, 1), F32),
        grid=(n_rows // tq, n_heads), in_specs=[spec, spec],
        out_specs=pl.BlockSpec((1, tq, 1), lambda i, h: (h, i, 0)),
        compiler_params=_params(("parallel", "parallel")),
    )(do, o)


def flash_bwd_dq(q, k, v, do, c_rows, lse, delta, name):
    n_rows, d = q.shape
    n_heads = d // HEAD_DIM
    tq = _attn_tile(n_rows)
    nq = n_rows // tq
    scale = HEAD_DIM ** -0.5

    def body(q_ref, k_ref, v_ref, do_ref, c_ref, lse_ref, dl_ref, dq_ref):
        h, i = pl.program_id(0), pl.program_id(1)
        qv, dov = q_ref[...], do_ref[...]
        lse_q, dl_q = lse_ref[0], dl_ref[0]
        c_q = _pick_first(c_ref[i], h)

        def block(j, acc, diagonal):
            rows = pl.ds(pl.multiple_of(j * tq, tq), tq)
            kk = k_ref[rows, :]
            s = lax.dot_general(qv, kk, (((1,), (1,)), ((), ())), preferred_element_type=F32) * scale
            s = s + (c_q - _pick_row(c_ref[j], h))
            if diagonal:
                s = _causal(s)
            p = jnp.exp(s - lse_q)
            dp = lax.dot_general(dov, v_ref[rows, :], (((1,), (1,)), ((), ())), preferred_element_type=F32)
            ds = p * (dp - dl_q)
            return acc + jnp.dot(ds.astype(BF16), kk, preferred_element_type=F32)

        acc = lax.fori_loop(0, i, lambda j, a: block(j, a, False), jnp.zeros((tq, HEAD_DIM), F32))
        acc = block(i, acc, True)
        dq_ref[...] = (acc * scale).astype(dq_ref.dtype)

    qspec = pl.BlockSpec((tq, HEAD_DIM), lambda h, i: (i, h))
    kspec = pl.BlockSpec((n_rows, HEAD_DIM), lambda h, i: (0, h))
    vec = pl.BlockSpec((1, tq, 1), lambda h, i: (h, i, 0))
    return pl.pallas_call(
        body, name=name, out_shape=jax.ShapeDtypeStruct((n_rows, d), BF16),
        grid=(n_heads, nq),
        in_specs=[qspec, kspec, kspec, qspec, pl.BlockSpec((nq, n_heads, tq), lambda h, i: (0, 0, 0)), vec, vec],
        out_specs=qspec,
        compiler_params=_params(("parallel", "arbitrary")),
    )(q, k, v, do, c_rows, lse, delta)


def flash_bwd_dkv(q, k, v, do, c_rows, c_cols, lse_rows, delta_rows, name):
    n_rows, d = q.shape
    n_heads = d // HEAD_DIM
    tq = _attn_tile(n_rows)
    nq = n_rows // tq
    scale = HEAD_DIM ** -0.5

    def body(q_ref, k_ref, v_ref, do_ref, c_ref, ck_ref, lse_ref, dl_ref, dk_ref, dv_ref, dc_ref):
        h, j = pl.program_id(0), pl.program_id(1)
        kk, vv = k_ref[...], v_ref[...]
        c_k = ck_ref[0]

        def block(i, carry, diagonal):
            dk, dv, dc = carry
            rows = pl.ds(pl.multiple_of(i * tq, tq), tq)
            qi, doi = q_ref[rows, :], do_ref[rows, :]
            st = lax.dot_general(kk, qi, (((1,), (1,)), ((), ())), preferred_element_type=F32) * scale
            st = st + (_pick_first(c_ref[i], h) - c_k)
            if diagonal:
                st = _causal(st, keys_on_rows=True)
            pt = jnp.exp(st - _pick_row(lse_ref[i], h))
            dpt = lax.dot_general(vv, doi, (((1,), (1,)), ((), ())), preferred_element_type=F32)
            dst = pt * (dpt - _pick_row(dl_ref[i], h))
            dv = dv + jnp.dot(pt.astype(BF16), doi, preferred_element_type=F32)
            dk = dk + jnp.dot(dst.astype(BF16), qi, preferred_element_type=F32)
            return dk, dv, dc - jnp.sum(dst, axis=1, keepdims=True)

        zero = jnp.zeros((tq, HEAD_DIM), F32)
        carry = block(j, (zero, zero, jnp.zeros((tq, 1), F32)), True)
        dk, dv, dc = lax.fori_loop(j + 1, nq, lambda i, c: block(i, c, False), carry)
        dk_ref[...] = dk * scale
        dv_ref[...] = dv
        dc_ref[0] = dc

    kspec = pl.BlockSpec((tq, HEAD_DIM), lambda h, j: (j, h))
    qspec = pl.BlockSpec((n_rows, HEAD_DIM), lambda h, j: (0, h))
    rows3 = pl.BlockSpec((nq, n_heads, tq), lambda h, j: (0, 0, 0))
    col = pl.BlockSpec((1, tq, 1), lambda h, j: (h, j, 0))
    return pl.pallas_call(
        body, name=name,
        out_shape=[jax.ShapeDtypeStruct((n_rows, d), F32), jax.ShapeDtypeStruct((n_rows, d), F32),
                   jax.ShapeDtypeStruct((n_heads, n_rows, 1), F32)],
        grid=(n_heads, nq),
        in_specs=[qspec, kspec, kspec, qspec, rows3, col, rows3, rows3],
        out_specs=[kspec, kspec, col],
        compiler_params=_params(("parallel", "arbitrary")),
    )(q, k, v, do, c_rows, c_cols, lse_rows, delta_rows)


S5_PARAM_SHAPES = [(1, S5_2P)] * 3 + [(S5_2P, 1)] * 3 + [(S5_GROUP, S5_2P)] * 4 + [(S5_2P, S5_TC)] * 4 + [(S5_TC, 1)]


def _tile_rows(a, reps):
    return jnp.concatenate([a] * reps, axis=0)


def _s5_build(lrl, lil, lsl, lrc, lic, lsc, cri, cir, bt2, bt2s, brit, birt, ctt, ctts):
    first = lax.broadcasted_iota(jnp.int32, (1, S5_2P), 1) < S5_STATE
    sgn = jnp.where(first, -1.0, 1.0)
    dt_l = jnp.exp(lsl)
    a_l, b_l = lrl * dt_l, lil * dt_l
    mag1 = jnp.exp(a_l)
    lbr, lbi = mag1 * jnp.cos(b_l), mag1 * jnp.sin(b_l)
    n = lrl * lrl + lil * lil
    kr = ((lbr - 1.0) * lrl + lbi * lil) / n
    ki = (lbi * lrl - (lbr - 1.0) * lil) / n
    blk = lax.shift_right_logical(lax.broadcasted_iota(jnp.int32, (S5_TC, 1), 0), 4).astype(F32)
    e_end = (S5_CHUNK - 1.0) - blk
    mag_e = jnp.exp(e_end * a_l)
    pr, pi = mag_e * jnp.cos(e_end * b_l), mag_e * jnp.sin(e_end * b_l)
    cr, ci = pr * kr - pi * ki, pr * ki + pi * kr
    w_end = cr * _tile_rows(bt2, S5_CHUNK) + (sgn * ci) * _tile_rows(bt2s, S5_CHUNK)
    mag_j = jnp.exp(blk * a_l)
    qr, qi = mag_j * jnp.cos(blk * b_l), mag_j * jnp.sin(blk * b_l)
    xmat = qr * _tile_rows(cri, S5_CHUNK) + (sgn * qi) * _tile_rows(cir, S5_CHUNK)
    dt_c = jnp.exp(lsc)
    a_c, b_c = lrc * dt_c, lic * dt_c
    m1c = jnp.exp(a_c)
    lbrc, lbic = m1c * jnp.cos(b_c), m1c * jnp.sin(b_c)
    n_c = lrc * lrc + lic * lic
    krc = ((lbrc - 1.0) * lrc + lbic * lic) / n_c
    kic = (lbic * lrc - (lbrc - 1.0) * lic) / n_c
    top = lax.broadcasted_iota(jnp.int32, (S5_2P, 1), 0) < S5_STATE
    rhs = jnp.where(top, krc, -krc) * brit + (-kic) * birt
    z = jnp.dot(xmat, rhs, precision=HI, preferred_element_type=F32)
    e_out = (lax.shift_right_logical(lax.broadcasted_iota(jnp.int32, (1, S5_TC), 1), 4) + 1).astype(F32)
    mag_o = jnp.exp(a_c * e_out)
    o_r, o_i = mag_o * jnp.cos(b_c * e_out), mag_o * jnp.sin(b_c * e_out)
    w_out = jnp.where(top, o_r, -o_r) * ctt + (-o_i) * ctts
    m16 = jnp.exp(float(S5_CHUNK) * a_l)
    a1 = m16 * jnp.cos(float(S5_CHUNK) * b_l)
    a2 = sgn * (m16 * jnp.sin(float(S5_CHUNK) * b_l))
    return z, w_end, w_out, a1, a2


def _lane_block_bit(b):
    blk = lax.shift_right_logical(lax.broadcasted_iota(jnp.int32, (S5_TC, S5_TC), 1), 4)
    return (lax.shift_right_logical(blk, b) & 1) == 1


def _toeplitz(z, dcol):
    for b in range(4):
        s = S5_GROUP << b
        moved = jnp.concatenate([jnp.zeros((s, S5_TC), F32), z[:S5_TC - s]], axis=0)
        z = jnp.where(_lane_block_bit(b), moved, z)
    r = lax.broadcasted_iota(jnp.int32, (S5_TC, S5_TC), 0)
    c = lax.broadcasted_iota(jnp.int32, (S5_TC, S5_TC), 1)
    return z + jnp.where(r == c, dcol, 0.0)


def _toeplitz_t(dt):
    r = lax.broadcasted_iota(jnp.int32, (S5_TC, S5_TC), 0)
    c = lax.broadcasted_iota(jnp.int32, (S5_TC, S5_TC), 1)
    ddcol = jnp.sum(jnp.where(r == c, dt, 0.0), axis=1, keepdims=True)
    for b in range(4):
        s = S5_GROUP << b
        moved = jnp.concatenate([dt[s:], jnp.zeros((s, S5_TC), F32)], axis=0)
        dt = jnp.where(_lane_block_bit(b), moved, dt)
    return dt, ddcol


def _s5_powers(a1, a2, n):
    first = lax.broadcasted_iota(jnp.int32, a1.shape, 1) < S5_STATE
    ar, ai = a1, jnp.where(first, -a2, a2)
    out = []
    d = 1
    while d < n:
        out.append((ar, jnp.where(first, -ai, ai)))
        ar, ai = ar * ar - ai * ai, 2.0 * ar * ai
        d *= 2
    return out


def _s5_prefix(e, pows, reverse):
    n = e.shape[0]
    row = lax.broadcasted_iota(jnp.int32, e.shape, 0)
    x = e
    for b, (p1, p2) in enumerate(pows):
        d = 1 << b
        if reverse:
            sh = jnp.where(row < n - d, pltpu.roll(x, n - d, 0), 0.0)
            x = x + p1 * sh - p2 * pltpu.roll(sh, S5_STATE, 1)
        else:
            sh = jnp.where(row >= d, pltpu.roll(x, d, 0), 0.0)
            x = x + p1 * sh + p2 * pltpu.roll(sh, S5_STATE, 1)
    return x


def _s5_scan(e, pows):
    n = e.shape[0]
    row = lax.broadcasted_iota(jnp.int32, e.shape, 0)
    return jnp.where(row >= 1, pltpu.roll(_s5_prefix(e, pows, False), 1, 0), 0.0)


def _s5_scan_t(dsin, pows):
    n = dsin.shape[0]
    row = lax.broadcasted_iota(jnp.int32, dsin.shape, 0)
    return jnp.where(row < n - 1, pltpu.roll(_s5_prefix(dsin, pows, True), n - 1, 0), 0.0)


def _dot_hi(a, b, dims):
    return lax.dot_general(a, b, (dims, ((), ())), precision=HI, preferred_element_type=F32)


_NN = ((1,), (0,))
_NT = ((1,), (1,))
_TN = ((0,), (0,))


def _s5_specs(gb):
    return [pl.BlockSpec((gb, *s), lambda i: (i, 0, 0)) for s in S5_PARAM_SHAPES]


def s5_fwd(u, prm, name):
    n_groups, nk, _ = u.shape
    gb = min(4, n_groups)
    n_prm = len(S5_PARAM_SHAPES)

    def body(*refs):
        u_ref = refs[0]
        p_refs = refs[1:1 + n_prm]
        o_ref = refs[1 + n_prm]
        sin_sc, toep_sc = refs[2 + n_prm:]
        for g in range(gb):
            z, w_end, w_out, a1, a2 = _s5_build(*[r[g] for r in p_refs[:-1]])
            toep_sc[...] = _toeplitz(z, p_refs[-1][g])
            ug = u_ref[g].astype(F32)
            sin_sc[...] = _s5_scan(_dot_hi(ug, w_end, _NN), _s5_powers(a1, a2, nk))
            y = _dot_hi(ug, toep_sc[...], _NT) + _dot_hi(sin_sc[...], w_out, _NN)
            o_ref[g] = _gelu(y).astype(o_ref.dtype)

    blk = pl.BlockSpec((gb, nk, S5_TC), lambda i: (i, 0, 0))
    return pl.pallas_call(
        body, name=name, out_shape=jax.ShapeDtypeStruct(u.shape, BF16), grid=(n_groups // gb,),
        in_specs=[blk] + _s5_specs(gb), out_specs=blk,
        scratch_shapes=[pltpu.VMEM((nk, S5_2P), F32), pltpu.VMEM((S5_TC, S5_TC), F32)],
        compiler_params=_params(("parallel",)),
    )(u, *prm)


def s5_bwd(u, dyg, prm, name):
    n_groups, nk, _ = u.shape
    gb = min(4, n_groups)
    n_prm = len(S5_PARAM_SHAPES)

    def body(*refs):
        u_ref, dyg_ref = refs[0], refs[1]
        p_refs = refs[2:2 + n_prm]
        du_ref = refs[2 + n_prm]
        dp_refs = refs[3 + n_prm:3 + 2 * n_prm]
        sin_sc, de_sc, dy_sc, toep_sc = refs[3 + 2 * n_prm:]
        for g in range(gb):
            outs, vjp = jax.vjp(_s5_build, *[r[g] for r in p_refs[:-1]])
            z, w_end, w_out, a1, a2 = outs
            toep_sc[...] = _toeplitz(z, p_refs[-1][g])
            pows = _s5_powers(a1, a2, nk)
            ug = u_ref[g].astype(F32)
            sin_sc[...] = _s5_scan(_dot_hi(ug, w_end, _NN), pows)
            y = _dot_hi(ug, toep_sc[...], _NT) + _dot_hi(sin_sc[...], w_out, _NN)
            dy_sc[...] = dyg_ref[g].astype(F32) * _gelu_grad(y)
            dy = dy_sc[...]
            de_sc[...] = _s5_scan_t(_dot_hi(dy, w_out, _NT), pows)
            de, sin = de_sc[...], sin_sc[...]
            du_ref[g] = (_dot_hi(dy, toep_sc[...], _NN) + _dot_hi(de, w_end, _NT)).astype(du_ref.dtype)
            dz, ddcol = _toeplitz_t(_dot_hi(dy, ug, _TN))
            da1 = jnp.sum(de * sin, axis=0, keepdims=True)
            da2 = jnp.sum(de * pltpu.roll(sin, S5_STATE, 1), axis=0, keepdims=True)
            grads = vjp((dz, _dot_hi(ug, de, _TN), _dot_hi(sin, dy, _TN), da1, da2))
            for r, v in zip(dp_refs[:-1], grads):
                r[g] = v
            dp_refs[-1][g] = ddcol

    blk = pl.BlockSpec((gb, nk, S5_TC), lambda i: (i, 0, 0))
    res = pl.pallas_call(
        body, name=name,
        out_shape=[jax.ShapeDtypeStruct(u.shape, BF16)]
        + [jax.ShapeDtypeStruct((n_groups, *s), F32) for s in S5_PARAM_SHAPES],
        grid=(n_groups // gb,),
        in_specs=[blk, blk] + _s5_specs(gb), out_specs=[blk] + _s5_specs(gb),
        scratch_shapes=[pltpu.VMEM((nk, S5_2P), F32), pltpu.VMEM((nk, S5_2P), F32),
                        pltpu.VMEM((nk, S5_TC), F32), pltpu.VMEM((S5_TC, S5_TC), F32)],
        compiler_params=_params(("parallel",)),
    )(u, dyg, *prm)
    return res[0], res[1:]


def s5_prepare(lam_re, lam_im, log_step, b_re, b_im, c_re, c_im, d):
    g = lam_re.shape[0]
    lane = lambda a: a.reshape(g, 1, -1)
    col = lambda a: a.reshape(g, -1, 1)
    ls = jnp.broadcast_to(log_step[:, None], (g, S5_2P))
    brt = jnp.swapaxes(b_re, 1, 2)
    bit = jnp.swapaxes(b_im, 1, 2)
    crt = jnp.swapaxes(c_re, 1, 2)
    cit = jnp.swapaxes(c_im, 1, 2)
    cat = jnp.concatenate
    return [
        lane(cat([lam_re, lam_re], 1)), lane(cat([lam_im, lam_im], 1)), lane(ls),
        col(cat([lam_re, lam_re], 1)), col(cat([lam_im, lam_im], 1)), col(ls),
        cat([c_re, c_im], 2), cat([c_im, c_re], 2),
        cat([brt, bit], 2), cat([bit, brt], 2),
        jnp.tile(cat([b_re, b_im], 1), (1, 1, S5_CHUNK)), jnp.tile(cat([b_im, b_re], 1), (1, 1, S5_CHUNK)),
        jnp.tile(cat([crt, cit], 1), (1, 1, S5_CHUNK)), jnp.tile(cat([cit, crt], 1), (1, 1, S5_CHUNK)),
        col(jnp.tile(d, (1, S5_CHUNK))),
    ]


def s5_param_grads(prm, grads):
    _, vjp = jax.vjp(s5_prepare, *prm)
    return vjp(list(grads))


def s5_to_groups(a):
    n_rows, d = a.shape
    g, nk = d // S5_GROUP, n_rows // S5_CHUNK
    return a.reshape(nk, S5_CHUNK, g, S5_GROUP).transpose(2, 0, 1, 3).reshape(g, nk, S5_TC)


def s5_from_groups(a):
    g, nk, _ = a.shape
    return a.reshape(g, nk, S5_CHUNK, S5_GROUP).transpose(1, 2, 0, 3).reshape(nk * S5_CHUNK, g * S5_GROUP)


def _adamw_math(w, g, m, v):
    m = ADAM_B1 * m + (1.0 - ADAM_B1) * g
    v = ADAM_B2 * v + (1.0 - ADAM_B2) * (g * g)
    m_hat = m / (1.0 - ADAM_B1 ** ADAM_STEP)
    v_hat = v / (1.0 - ADAM_B2 ** ADAM_STEP)
    delta = -ADAM_LR * (m_hat / (jnp.sqrt(v_hat) + ADAM_EPS) + ADAM_WD * w)
    return delta, m, v


def adamw_sum(parts, w, m, v, name):
    r, c = w.shape
    n_parts = parts.shape[0]
    tr = _pick(r, (128, 64, 32, 16, 8))

    def body(p_ref, w_ref, m_ref, v_ref, g_out, d_out, m_out, v_out):
        g = p_ref[0].astype(F32)
        for s in range(1, n_parts):
            g = g + p_ref[s].astype(F32)
        delta, m_new, v_new = _adamw_math(w_ref[...], g, m_ref[...], v_ref[...])
        g_out[...] = g
        d_out[...] = delta
        m_out[...] = m_new
        v_out[...] = v_new

    spec = pl.BlockSpec((tr, c), lambda i: (i, 0))
    return pl.pallas_call(
        body, name=name, out_shape=[jax.ShapeDtypeStruct((r, c), F32)] * 4, grid=(r // tr,),
        in_specs=[pl.BlockSpec((n_parts, tr, c), lambda i: (0, i, 0)), spec, spec, spec],
        out_specs=[spec] * 4, compiler_params=_params(("parallel",)),
    )(parts, w, m, v)


def sum8(parts, name):
    r, c = parts.shape[1:]
    tr = _pick(r, (256, 128, 64, 32, 16, 8))

    def body(p_ref, o_ref):
        g = p_ref[0]
        for s in range(1, N_DEV):
            g = g + p_ref[s]
        o_ref[...] = g

    return pl.pallas_call(
        body, name=name, out_shape=jax.ShapeDtypeStruct((r, c), F32), grid=(r // tr,),
        in_specs=[pl.BlockSpec((N_DEV, tr, c), lambda i: (0, i, 0))],
        out_specs=pl.BlockSpec((tr, c), lambda i: (i, 0)), compiler_params=_params(("parallel",)),
    )(parts)


def adamw_plain(g, w, m, v, name):
    r, c = w.shape
    tr = _pick(r, (256, 128, 64, 32, 16, 8))

    def body(g_ref, w_ref, m_ref, v_ref, d_out, m_out, v_out):
        delta, m_new, v_new = _adamw_math(w_ref[...], g_ref[...], m_ref[...], v_ref[...])
        d_out[...] = delta
        m_out[...] = m_new
        v_out[...] = v_new

    spec = pl.BlockSpec((tr, c), lambda i: (i, 0))
    return pl.pallas_call(
        body, name=name, out_shape=[jax.ShapeDtypeStruct((r, c), F32)] * 3, grid=(r // tr,),
        in_specs=[spec] * 4, out_specs=[spec] * 3, compiler_params=_params(("parallel",)),
    )(g, w, m, v)


PACK_LANES = 1024


def _pack(arrays):
    pieces = []
    for a in arrays:
        flat = a.reshape(-1).astype(F32)
        pieces.append(jnp.pad(flat, (0, (-flat.shape[0]) % PACK_LANES)))
    flat = jnp.concatenate(pieces)
    return jnp.pad(flat, (0, (-flat.shape[0]) % (8 * PACK_LANES))).reshape(-1, PACK_LANES)


def _unpack(buf, shapes):
    flat = buf.reshape(-1)
    out, off = [], 0
    for s in shapes:
        n = math.prod(s)
        out.append(flat[off:off + n].reshape(s))
        off += n + ((-n) % PACK_LANES)
    return out


WEIGHT_NAMES = ['a_norm', 'a_w_in', 'a_lambda_re', 'a_lambda_im', 'a_log_step', 'a_b_re', 'a_b_im', 'a_c_re',
                'a_c_im', 'a_d', 'a_w_glu', 'kv_norm', 'w_k', 'w_v', 'w_f', 'b_f', 'b_norm', 'b_w_q', 'b_w_o',
                'ffn_norm', 'ffn_w_up', 'ffn_conv_w', 'ffn_conv_b', 'ffn_w_down', 'final_norm']
BIG = {'a_w_in': 0, 'a_w_glu': 1, 'w_k': 0, 'w_v': 0, 'b_w_q': 0, 'b_w_o': 0, 'ffn_w_up': 1, 'ffn_w_down': 0}
SMALL_SHARDED = {'a_norm': 1, 'w_f': 0, 'ffn_conv_w': 2}


def _ffn_fwd(x, gain, w_up, conv_w, conv_b, w_down, tag):
    h = norm_fwd(x, gain, f"norm_fwd_{tag}")
    up = matmul(h, w_up, name=f"mm_up_{tag}")
    a = convglu_fwd(up, conv_w, conv_b, f"convglu_fwd_{tag}")
    x_out = matmul(a, w_down, add=x, name=f"mm_down_{tag}")
    return x_out, (x, h, up, a)


def _ffn_bwd(dx, dxb, saved, gain, w_up, conv_w, conv_b, w_down, tag):
    x, h, up, a = saved
    dw_down = matmul(a, dxb, ta=True, out_dtype=BF16, name=f"mm_dwdown_{tag}")
    da = matmul(dxb, w_down, tb=True, name=f"mm_da_{tag}")
    dup, dconv_w, dconv_b = convglu_bwd(up, da, conv_w, conv_b, f"convglu_bwd_{tag}")
    dw_up = matmul(h, dup, ta=True, out_dtype=BF16, name=f"mm_dwup_{tag}")
    dh = matmul(dup, w_up, tb=True, name=f"mm_dh_{tag}")
    dx_in, dxb_in, dgain = norm_bwd(x, gain, dh, dx, f"norm_bwd_{tag}")
    return dx_in, dxb_in, dict(norm=dgain, w_up=dw_up, conv_w=dconv_w, conv_b=dconv_b, w_down=dw_down)


def kernel(x, a_norm, a_w_in, a_lambda_re, a_lambda_im, a_log_step, a_b_re, a_b_im, a_c_re, a_c_im, a_d, a_w_glu, kv_norm, w_k, w_v, w_f, b_f, b_norm, b_w_q, b_w_o, ffn_norm, ffn_w_up, ffn_conv_w, ffn_conv_b, ffn_w_down, final_norm, loss_target, m_a_norm, m_a_w_in, m_a_lambda_re, m_a_lambda_im, m_a_log_step, m_a_b_re, m_a_b_im, m_a_c_re, m_a_c_im, m_a_d, m_a_w_glu, m_kv_norm, m_w_k, m_w_v, m_w_f, m_b_f, m_b_norm, m_b_w_q, m_b_w_o, m_ffn_norm, m_ffn_w_up, m_ffn_conv_w, m_ffn_conv_b, m_ffn_w_down, m_final_norm, v_a_norm, v_a_w_in, v_a_lambda_re, v_a_lambda_im, v_a_log_step, v_a_b_re, v_a_b_im, v_a_c_re, v_a_c_im, v_a_d, v_a_w_glu, v_kv_norm, v_w_k, v_w_v, v_w_f, v_b_f, v_b_norm, v_b_w_q, v_b_w_o, v_ffn_norm, v_ffn_w_up, v_ffn_conv_w, v_ffn_conv_b, v_ffn_w_down, v_final_norm):
    args = locals()
    w = {n: args[n] for n in WEIGHT_NAMES}
    mom = {n: args["m_" + n] for n in WEIGHT_NAMES}
    var = {n: args["v_" + n] for n in WEIGHT_NAMES}
    x0 = x[0]
    target = loss_target[0]
    n_rows, d = x0.shape
    n_heads = d // HEAD_DIM
    n_a = a_w_in.shape[0]
    n_b = b_w_q.shape[0]
    me = _my_index()

    def gathered(name, layer=None):
        ws = w[name] if layer is None else w[name][layer]
        tag = name if layer is None else f"{name}{layer}"
        return all_gather(ws.astype(BF16), BIG[name], f"ag_{tag}")

    small_local = [w[n] for n in SMALL_SHARDED]
    small_full = all_gather(_pack(small_local)[None], 0, "ag_small")
    per_dev = [_unpack(small_full[s], [a.shape for a in small_local]) for s in range(N_DEV)]
    full_small = {n: jnp.concatenate([per_dev[s][i] for s in range(N_DEV)], axis=SMALL_SHARDED[n])
                  for i, n in enumerate(SMALL_SHARDED)}
    a_norm_f, w_f_f, conv_w_f = full_small['a_norm'], full_small['w_f'], full_small['ffn_conv_w']
    w_f_pad = jnp.pad(w_f_f, ((0, 0), (0, 128 - n_heads))).astype(BF16)
    b_f_pad = jnp.pad(b_f, (0, 128 - n_heads)).reshape(1, 128)

    grads = {}
    big_grads = {}

    xs = x0
    saved_a, saved_b, saved_ffn = [], [], []
    weights_a, weights_b, weights_ffn = [], [], []
    for i in range(n_a):
        w_in = gathered('a_w_in', i)
        w_glu = gathered('a_w_glu', i)
        prm = s5_prepare(a_lambda_re[i], a_lambda_im[i], a_log_step[i], a_b_re[i], a_b_im[i],
                         a_c_re[i], a_c_im[i], a_d[i])
        gain = a_norm_f[i].reshape(1, d)
        h = norm_fwd(xs, gain, f"norm_fwd_a{i}")
        u = s5_to_groups(matmul(h, w_in, out_dtype=BF16, name=f"mm_win_{i}"))
        yg = s5_from_groups(s5_fwd(u, prm, f"s5_fwd_{i}"))
        z = matmul(yg, w_glu, name=f"mm_wglu_{i}")
        x_mid = glu_fwd(z, xs, f"glu_fwd_{i}")
        saved_a.append((xs, h, u, yg, z))
        weights_a.append((w_in, w_glu, prm, gain))
        fw = (ffn_norm[i].reshape(1, d), gathered('ffn_w_up', i), conv_w_f[i], ffn_conv_b[i].reshape(1, -1),
              gathered('ffn_w_down', i))
        xs, sv = _ffn_fwd(x_mid, *fw, f"f{i}")
        saved_ffn.append(sv)
        weights_ffn.append(fw)

    x_kv = xs
    kv_gain = kv_norm.reshape(1, d)
    wk_f, wv_f = gathered('w_k'), gathered('w_v')
    h_kv = norm_fwd(x_kv, kv_gain, "norm_fwd_kv")
    k = matmul(h_kv, wk_f, out_dtype=BF16, name="mm_wk")
    v = matmul(h_kv, wv_f, out_dtype=BF16, name="mm_wv")
    zf = matmul(h_kv, w_f_pad, name="mm_wf")
    ct = fgate_fwd(zf, b_f_pad, n_heads, "fgate_fwd")
    c_rows, c_cols = attn_layouts(ct)

    for j in range(n_b):
        li = n_a + j
        w_q, w_o = gathered('b_w_q', j), gathered('b_w_o', j)
        gain = b_norm[j].reshape(1, d)
        h = norm_fwd(xs, gain, f"norm_fwd_b{j}")
        q = matmul(h, w_q, out_dtype=BF16, name=f"mm_wq_{j}")
        o, o32, lse = flash_fwd(q, k, v, c_rows, f"flash_fwd_{j}")
        x_mid = matmul(o, w_o, add=xs, name=f"mm_wo_{j}")
        saved_b.append((xs, h, q, o, o32, lse))
        weights_b.append((w_q, w_o, gain))
        fw = (ffn_norm[li].reshape(1, d), gathered('ffn_w_up', li), conv_w_f[li], ffn_conv_b[li].reshape(1, -1),
              gathered('ffn_w_down', li))
        xs, sv = _ffn_fwd(x_mid, *fw, f"f{li}")
        saved_ffn.append(sv)
        weights_ffn.append(fw)

    dx, dxb, d_final, sq = loss_and_grad(xs, final_norm.reshape(1, d), target, "loss")
    loss = lax.psum(0.5 * jnp.sum(sq) / d, MESH_AXES)
    grads['final_norm'] = d_final.reshape(d)

    ffn_g = [None] * (n_a + n_b)
    b_g = [None] * n_b
    dk_tot = dv_tot = dc_tot = None
    for j in reversed(range(n_b)):
        li = n_a + j
        dx, dxb, ffn_g[li] = _ffn_bwd(dx, dxb, saved_ffn[li], *weights_ffn[li], f"f{li}")
        xs_in, h, q, o, o32, lse = saved_b[j]
        w_q, w_o, gain = weights_b[j]
        dw_o = matmul(o, dxb, ta=True, out_dtype=BF16, name=f"mm_dwo_{j}")
        do = matmul(dxb, w_o, tb=True, out_dtype=BF16, name=f"mm_do_{j}")
        delta = attn_delta(do, o32, f"attn_delta_{j}")
        dq = flash_bwd_dq(q, k, v, do, c_rows, lse, delta, f"flash_dq_{j}")
        lse_rows = attn_layouts(lse.reshape(n_heads, n_rows))[0]
        delta_rows = attn_layouts(delta.reshape(n_heads, n_rows))[0]
        dk_j, dv_j, dc_j = flash_bwd_dkv(q, k, v, do, c_rows, c_cols, lse_rows, delta_rows, f"flash_dkv_{j}")
        dc_j = dc_j.reshape(n_heads, n_rows)
        if dk_tot is None:
            dk_tot, dv_tot, dc_tot = [dk_j], [dv_j], [dc_j]
        else:
            dk_tot.append(dk_j), dv_tot.append(dv_j), dc_tot.append(dc_j)
        dw_q = matmul(h, dq, ta=True, out_dtype=BF16, name=f"mm_dwq_{j}")
        dh = matmul(dq, w_q, tb=True, name=f"mm_dhq_{j}")
        dx, dxb, dgain = norm_bwd(xs_in, gain, dh, dx, f"norm_bwd_b{j}")
        b_g[j] = dict(norm=dgain, w_q=dw_q, w_o=dw_o)

    def total(parts, tag):
        if len(parts) == 1:
            return parts[0].astype(BF16)
        acc = parts[0]
        for n_, p_ in enumerate(parts[1:-1]):
            acc = acc + p_
        return add_cast(acc, parts[-1], f"add_{tag}")

    dk_b, dv_b = total(dk_tot, "dk"), total(dv_tot, "dv")
    dc_a = dc_tot[0]
    dc_b = dc_tot[1] if len(dc_tot) > 1 else jnp.zeros_like(dc_a)
    for extra in dc_tot[2:]:
        dc_b = dc_b + extra
    dzf, db_f = fgate_bwd(dc_a, dc_b, zf, b_f_pad, "fgate_bwd")
    big_grads[('w_k', None)] = matmul(h_kv, dk_b, ta=True, out_dtype=BF16, name="mm_dwk")
    big_grads[('w_v', None)] = matmul(h_kv, dv_b, ta=True, out_dtype=BF16, name="mm_dwv")
    dw_f = matmul(h_kv, dzf, ta=True, name="mm_dwf")[:, :n_heads]
    dh_kv = matmul(dk_b, wk_f, tb=True, name="mm_dhk")
    dh_kv = matmul(dv_b, wv_f, tb=True, add=dh_kv, name="mm_dhv")
    dh_kv = matmul(dzf, w_f_pad, tb=True, add=dh_kv, name="mm_dhf")
    dx, dxb, d_kv_gain = norm_bwd(x_kv, kv_gain, dh_kv, dx, "norm_bwd_kv")
    grads['kv_norm'] = d_kv_gain.reshape(d)
    grads['b_f'] = db_f[0, :n_heads]

    a_g = [None] * n_a
    for i in reversed(range(n_a)):
        dx, dxb, ffn_g[i] = _ffn_bwd(dx, dxb, saved_ffn[i], *weights_ffn[i], f"f{i}")
        xs_in, h, u, yg, z = saved_a[i]
        w_in, w_glu, prm, gain = weights_a[i]
        dz = glu_bwd(z, dx, f"glu_bwd_{i}")
        dw_glu = matmul(yg, dz, ta=True, out_dtype=BF16, name=f"mm_dwglu_{i}")
        dyg = s5_to_groups(matmul(dz, w_glu, tb=True, out_dtype=BF16, name=f"mm_dyg_{i}"))
        du_g, dprm = s5_bwd(u, dyg, prm, f"s5_bwd_{i}")
        du = s5_from_groups(du_g)
        dw_in = matmul(h, du, ta=True, out_dtype=BF16, name=f"mm_dwin_{i}")
        dh = matmul(du, w_in, tb=True, name=f"mm_dhin_{i}")
        dx, dxb, dgain = norm_bwd(xs_in, gain, dh, dx, f"norm_bwd_a{i}")
        a_g[i] = dict(norm=dgain, w_in=dw_in, w_glu=dw_glu, prm=s5_param_grads(
            (a_lambda_re[i], a_lambda_im[i], a_log_step[i], a_b_re[i], a_b_im[i], a_c_re[i], a_c_im[i],
             a_d[i]), dprm))
    grad_x = dx[None]

    for i in range(n_a):
        big_grads[('a_w_in', i)] = a_g[i]['w_in']
        big_grads[('a_w_glu', i)] = a_g[i]['w_glu']
    for j in range(n_b):
        big_grads[('b_w_q', j)] = b_g[j]['w_q']
        big_grads[('b_w_o', j)] = b_g[j]['w_o']
    for li in range(n_a + n_b):
        big_grads[('ffn_w_up', li)] = ffn_g[li]['w_up']
        big_grads[('ffn_w_down', li)] = ffn_g[li]['w_down']
    stack = lambda parts: jnp.stack(parts, axis=0)
    grads['a_norm'] = stack([a_g[i]['norm'].reshape(d) for i in range(n_a)])
    for pi, pname in enumerate(['a_lambda_re', 'a_lambda_im', 'a_log_step', 'a_b_re', 'a_b_im', 'a_c_re',
                                'a_c_im', 'a_d']):
        grads[pname] = stack([a_g[i]['prm'][pi] for i in range(n_a)])
    grads['w_f'] = dw_f
    grads['b_norm'] = stack([b_g[j]['norm'].reshape(d) for j in range(n_b)])
    grads['ffn_norm'] = stack([g_['norm'].reshape(d) for g_ in ffn_g])
    grads['ffn_conv_w'] = stack([g_['conv_w'] for g_ in ffn_g])
    grads['ffn_conv_b'] = stack([g_['conv_b'].reshape(-1) for g_ in ffn_g])

    small_names = [n for n in WEIGHT_NAMES if n not in BIG]
    small_shapes = [grads[n].shape for n in small_names]
    g_parts = all_gather(_pack([grads[n] for n in small_names])[None], 0, "ag_grads_small")
    g_sum = _unpack(sum8(g_parts, "sum_grads_small"), small_shapes)
    g_full = dict(zip(small_names, g_sum))
    g_local = {}
    for n in small_names:
        if n in SMALL_SHARDED:
            ax = SMALL_SHARDED[n]
            size = w[n].shape[ax]
            g_local[n] = lax.dynamic_slice_in_dim(g_full[n], me * size, size, axis=ax)
        else:
            g_local[n] = g_full[n]
    local_shapes = [w[n].shape for n in small_names]
    d_s, m_s, v_s = adamw_plain(_pack([g_local[n] for n in small_names]), _pack([w[n] for n in small_names]),
                                _pack([mom[n] for n in small_names]), _pack([var[n] for n in small_names]),
                                "adamw_small")
    out_g, out_d, out_m, out_v = dict(g_local), {}, {}, {}
    for n, dd, mm, vv in zip(small_names, _unpack(d_s, local_shapes), _unpack(m_s, local_shapes),
                             _unpack(v_s, local_shapes)):
        out_d[n], out_m[n], out_v[n] = dd, mm, vv

    for name, ax in BIG.items():
        layered = w[name].ndim == 3
        layers = range(w[name].shape[0]) if layered else [None]
        res = []
        for layer in layers:
            tag = name if layer is None else f"{name}{layer}"
            g_mat = big_grads[(name, layer)]
            theirs = sibling_exchange(g_mat, ax, f"rs_d2d_{tag}")
            parts = owner_exchange(chip_sum(own_blocks(g_mat, ax), theirs, f"rs_add_{tag}"), f"rs_ici_{tag}")
            pick = (lambda a: a) if layer is None else (lambda a: a[layer])
            res.append(adamw_sum(parts, pick(w[name]), pick(mom[name]), pick(var[name]), f"adamw_{tag}"))
        for dst, idx in ((out_g, 0), (out_d, 1), (out_m, 2), (out_v, 3)):
            dst[name] = stack([r[idx] for r in res]) if layered else res[0][idx]

    return (loss, grad_x, *[out_g[n] for n in WEIGHT_NAMES], *[out_d[n] for n in WEIGHT_NAMES],
            *[out_m[n] for n in WEIGHT_NAMES], *[out_v[n] for n in WEIGHT_NAMES])
```

```python
import functools
import math

import jax
import jax.numpy as jnp
from jax import lax
from jax.experimental import pallas as pl
from jax.experimental.pallas import tpu as pltpu

F32 = jnp.float32
BF16 = jnp.bfloat16
HI = lax.Precision.HIGHEST
MESH_AXES = ("x", "y", "c")
N_DEV = 8
VMEM_LIMIT_BYTES = 48 * 1024 * 1024

NORM_EPS = 1e-6
S5_GROUP = 16
S5_STATE = 64
S5_CHUNK = 16
S5_TC = S5_CHUNK * S5_GROUP
S5_2P = 2 * S5_STATE
HEAD_DIM = 128
MASK_VALUE = -1e30
GELU_C = math.sqrt(2.0 / math.pi)
GELU_A = 0.044715

ADAM_LR = 0.001
ADAM_B1 = 0.9
ADAM_B2 = 0.999
ADAM_EPS = 1e-08
ADAM_WD = 0.01
ADAM_STEP = 10


def _params(sem=None):
    return pltpu.CompilerParams(dimension_semantics=sem, vmem_limit_bytes=VMEM_LIMIT_BYTES)


def _pick(n, cands):
    for c in cands:
        if n % c == 0:
            return c
    return n


def _gelu(x):
    return 0.5 * x * (1.0 + jnp.tanh(GELU_C * (x + GELU_A * x * x * x)))


def _gelu_grad(x):
    t = jnp.tanh(GELU_C * (x + GELU_A * x * x * x))
    return 0.5 * (1.0 + t) + 0.5 * x * (1.0 - t * t) * GELU_C * (1.0 + 3.0 * GELU_A * x * x)


def _my_index():
    return 4 * lax.axis_index("x") + 2 * lax.axis_index("y") + lax.axis_index("c")


_COMM_SCRATCH = [pltpu.SemaphoreType.DMA((N_DEV - 1,)), pltpu.SemaphoreType.DMA((N_DEV - 1,)),
                 pltpu.SemaphoreType.DMA]
_HBM = pl.BlockSpec(memory_space=pltpu.HBM)


def all_gather(xs, axis, name):
    nd = xs.ndim
    n = xs.shape[axis]
    full = list(xs.shape)
    full[axis] = n * N_DEV

    def body(x_ref, out_ref, send_sems, recv_sems, local_sem):
        x, y, c = lax.axis_index("x"), lax.axis_index("y"), lax.axis_index("c")
        me, sibling = (x, y, c), (x, y, 1 - c)
        chips = [(1 - x, y), (x, 1 - y), (1 - x, 1 - y)]

        def rows(px, py, pc):
            sl = [slice(None)] * nd
            sl[axis] = pl.ds((4 * px + 2 * py + pc) * n, n)
            return out_ref.at[tuple(sl)]

        def copy(k, block, to, src=None):
            return pltpu.make_async_remote_copy(
                src_ref=rows(*block) if src is None else src, dst_ref=rows(*block),
                send_sem=send_sems.at[k], recv_sem=recv_sems.at[k], device_id=to,
                device_id_type=pl.DeviceIdType.MESH)

        mine = pltpu.make_async_copy(x_ref, rows(*me), local_sem)
        mine.start()
        first = [copy(0, me, sibling, src=x_ref)]
        first += [copy(1 + j, me, (*chip, c), src=x_ref) for j, chip in enumerate(chips)]
        for cp in first:
            cp.start()
        passed = [copy(4 + j, (*chip, c), sibling) for j, chip in enumerate(chips)]
        for j, chip in enumerate(chips):
            copy(1 + j, (*chip, c), me).wait_recv()
            passed[j].start()
        copy(0, sibling, me).wait_recv()
        for j, chip in enumerate(chips):
            copy(4 + j, (*chip, 1 - c), me).wait_recv()
        for cp in first + passed:
            cp.wait_send()
        mine.wait()

    return pl.pallas_call(
        body, name=name, out_shape=jax.ShapeDtypeStruct(tuple(full), xs.dtype),
        in_specs=[_HBM], out_specs=_HBM, scratch_shapes=_COMM_SCRATCH,
    )(xs)


N_CHIPS = 4


def sibling_exchange(g, axis, name):
    nd = g.ndim
    n = g.shape[axis] // N_DEV
    blk = list(g.shape)
    blk[axis] = n

    def body(g_ref, theirs_ref, send_sems, recv_sems):
        x, y, c = lax.axis_index("x"), lax.axis_index("y"), lax.axis_index("c")

        def block(idx):
            sl = [slice(None)] * nd
            sl[axis] = pl.ds(idx * n, n)
            return g_ref.at[tuple(sl)]

        sends = [pltpu.make_async_remote_copy(
            src_ref=block(2 * q + 1 - c), dst_ref=theirs_ref.at[q], send_sem=send_sems.at[q],
            recv_sem=recv_sems.at[q], device_id=(x, y, 1 - c), device_id_type=pl.DeviceIdType.MESH)
            for q in range(N_CHIPS)]
        for cp in sends:
            cp.start()
        for cp in sends:
            cp.wait_send()
            cp.wait_recv()

    return pl.pallas_call(
        body, name=name, out_shape=jax.ShapeDtypeStruct((N_CHIPS, *blk), g.dtype),
        in_specs=[_HBM], out_specs=_HBM, scratch_shapes=[pltpu.SemaphoreType.DMA((N_CHIPS,))] * 2,
    )(g)


def own_blocks(g, axis):
    c = lax.axis_index("c")
    n = g.shape[axis] // N_DEV
    return jnp.stack([lax.dynamic_slice_in_dim(g, (2 * q + c) * n, n, axis=axis) for q in range(N_CHIPS)])


def owner_exchange(part, name):
    def body(p_ref, out_ref, send_sems, recv_sems, local_sem):
        x, y, c = lax.axis_index("x"), lax.axis_index("y"), lax.axis_index("c")
        my_chip = 2 * x + y
        mine = pltpu.make_async_copy(p_ref.at[my_chip], out_ref.at[my_chip], local_sem)
        mine.start()

        def other(k):
            px = 1 - x if (k >> 1) & 1 else x
            py = 1 - y if k & 1 else y
            return (px, py, c), 2 * px + py

        def copy(k, src_slot, dst_slot):
            return pltpu.make_async_remote_copy(
                src_ref=p_ref.at[src_slot], dst_ref=out_ref.at[dst_slot], send_sem=send_sems.at[k - 1],
                recv_sem=recv_sems.at[k - 1], device_id=other(k)[0], device_id_type=pl.DeviceIdType.MESH)

        sends = [copy(k, other(k)[1], my_chip) for k in range(1, N_CHIPS)]
        for cp in sends:
            cp.start()
        for k in range(1, N_CHIPS):
            sends[k - 1].wait_send()
            copy(k, other(k)[1], other(k)[1]).wait_recv()
        mine.wait()

    return pl.pallas_call(
        body, name=name, out_shape=jax.ShapeDtypeStruct(part.shape, part.dtype),
        in_specs=[_HBM], out_specs=_HBM,
        scratch_shapes=[pltpu.SemaphoreType.DMA((N_CHIPS - 1,)), pltpu.SemaphoreType.DMA((N_CHIPS - 1,)),
                        pltpu.SemaphoreType.DMA],
    )(part)


def chip_sum(mine, theirs, name):
    shape = mine.shape
    cols = shape[-1]

    def fn(a, b):
        return (a.astype(F32) + b.astype(F32),), ()

    out = rowwise(fn, [mine.reshape(-1, cols), theirs.reshape(-1, cols)], [], [(cols, BF16)], [], tl=512,
                  name=name)[0]
    return out.reshape(shape)


def matmul(a, b, *, ta=False, tb=False, out_dtype=F32, add=None, name):
    m = a.shape[1] if ta else a.shape[0]
    kk = a.shape[0] if ta else a.shape[1]
    n = b.shape[0] if tb else b.shape[1]
    tm = _pick(m, (1024, 512, 256, 128))
    tn = _pick(n, (1024, 512, 256, 128))
    tk = _pick(kk, (2048, 1408, 1024, 512, 256, 128))
    nk = kk // tk
    dims = (((0,) if ta else (1,), (1,) if tb else (0,)), ((), ()))

    def body(*refs):
        a_ref, b_ref = refs[0], refs[1]
        add_ref = None if add is None else refs[2]
        o_ref = refs[2 if add is None else 3]

        def finish(r):
            if add is not None:
                r = r + add_ref[...].astype(F32)
            o_ref[...] = r.astype(out_dtype)

        part = lax.dot_general(a_ref[...], b_ref[...], dims, preferred_element_type=F32)
        if nk == 1:
            finish(part)
            return
        acc = refs[-1]
        k = pl.program_id(2)

        @pl.when(k == 0)
        def _():
            acc[...] = part

        @pl.when(k > 0)
        def _():
            acc[...] += part

        @pl.when(k == nk - 1)
        def _():
            finish(acc[...])

    a_spec = (pl.BlockSpec((tk, tm), lambda i, j, k: (k, i)) if ta
              else pl.BlockSpec((tm, tk), lambda i, j, k: (i, k)))
    b_spec = (pl.BlockSpec((tn, tk), lambda i, j, k: (j, k)) if tb
              else pl.BlockSpec((tk, tn), lambda i, j, k: (k, j)))
    o_spec = pl.BlockSpec((tm, tn), lambda i, j, k: (i, j))
    ins = [a, b] + ([] if add is None else [add])
    in_specs = [a_spec, b_spec] + ([] if add is None else [o_spec])
    return pl.pallas_call(
        body, name=name, out_shape=jax.ShapeDtypeStruct((m, n), out_dtype),
        grid=(m // tm, n // tn, nk), in_specs=in_specs, out_specs=o_spec,
        scratch_shapes=[] if nk == 1 else [pltpu.VMEM((tm, tn), F32)],
        compiler_params=_params(("parallel", "parallel", "arbitrary")),
    )(*ins)


def rowwise(fn, ins, rows, outs, accs, *, tl, name):
    n_rows = ins[0].shape[0]
    tl = _pick(n_rows, tuple(tl >> s for s in range(tl.bit_length() - 3)))
    n_in, n_row, n_out, n_acc = len(ins), len(rows), len(outs), len(accs)

    def body(*refs):
        in_refs = refs[:n_in]
        row_refs = refs[n_in:n_in + n_row]
        out_refs = refs[n_in + n_row:n_in + n_row + n_out]
        acc_refs = refs[n_in + n_row + n_out:]
        res_outs, res_accs = fn(*[r[...] for r in in_refs], *[r[...] for r in row_refs])
        for r, v in zip(out_refs, res_outs):
            r[...] = v.astype(r.dtype)
        if n_acc:
            @pl.when(pl.program_id(0) == 0)
            def _():
                for r in acc_refs:
                    r[...] = jnp.zeros_like(r)

            for r, v in zip(acc_refs, res_accs):
                r[...] += v

    in_specs = [pl.BlockSpec((tl, a.shape[1]), lambda i: (i, 0)) for a in ins]
    in_specs += [pl.BlockSpec((1, a.shape[1]), lambda i: (0, 0)) for a in rows]
    out_specs = [pl.BlockSpec((tl, w), lambda i: (i, 0)) for w, _ in outs]
    out_specs += [pl.BlockSpec((1, w), lambda i: (0, 0)) for w in accs]
    out_shape = [jax.ShapeDtypeStruct((n_rows, w), dt) for w, dt in outs]
    out_shape += [jax.ShapeDtypeStruct((1, w), F32) for w in accs]
    res = pl.pallas_call(
        body, name=name, out_shape=out_shape, grid=(n_rows // tl,), in_specs=in_specs,
        out_specs=out_specs, compiler_params=_params(("arbitrary",)),
    )(*ins, *rows)
    return res


def _rsum(v):
    return jnp.sum(v, axis=0, keepdims=True)


def norm_fwd(x, g, name):
    def fn(xt, gt):
        r = lax.rsqrt(jnp.mean(xt * xt, axis=-1, keepdims=True) + NORM_EPS)
        return (xt * r * gt,), ()

    d = x.shape[1]
    return rowwise(fn, [x], [g], [(d, BF16)], [], tl=512, name=name)[0]


def norm_bwd(x, g, dh, dres, name):
    def fn(xt, dht, drt, gt):
        r = lax.rsqrt(jnp.mean(xt * xt, axis=-1, keepdims=True) + NORM_EPS)
        w = dht * gt
        dx = drt + (r * w - xt * (r * r * r) * jnp.mean(xt * w, axis=-1, keepdims=True))
        return (dx, dx), (_rsum(dht * xt * r),)

    d = x.shape[1]
    dx, dxb, dg = rowwise(fn, [x, dh, dres], [g], [(d, F32), (d, BF16)], [d], tl=256, name=name)
    return dx, dxb, dg


def loss_and_grad(x, g, target, name):
    d = x.shape[1]

    def fn(xt, tt, gt):
        r = lax.rsqrt(jnp.mean(xt * xt, axis=-1, keepdims=True) + NORM_EPS)
        y = xt * r * gt
        diff = y - tt
        dy = diff * (1.0 / d)
        w = dy * gt
        dx = r * w - xt * (r * r * r) * jnp.mean(xt * w, axis=-1, keepdims=True)
        return (dx, dx), (_rsum(dy * xt * r), _rsum(diff * diff))

    dx, dxb, dg, sq = rowwise(fn, [x, target], [g], [(d, F32), (d, BF16)], [d, d], tl=256, name=name)
    return dx, dxb, dg, sq


def glu_fwd(z, x, name):
    d = x.shape[1]

    def fn(zt, xt):
        return (xt + zt[:, :d] * jax.nn.sigmoid(zt[:, d:]),), ()

    return rowwise(fn, [z, x], [], [(d, F32)], [], tl=256, name=name)[0]


def glu_bwd(z, dx, name):
    d = dx.shape[1]

    def fn(zt, dxt):
        val, gate = zt[:, :d], zt[:, d:]
        s = jax.nn.sigmoid(gate)
        return (jnp.concatenate([dxt * s, dxt * val * s * (1.0 - s)], axis=1),), ()

    return rowwise(fn, [z, dx], [], [(2 * d, BF16)], [], tl=256, name=name)[0]


def add_cast(a, b, name):
    def fn(at, bt):
        return (at + bt,), ()

    return rowwise(fn, [a, b], [], [(a.shape[1], BF16)], [], tl=512, name=name)[0]


def _conv_tiles(n_rows, f):
    return min(256, n_rows), _pick(f, (512, 256, 128))


def _shifted(gate, h6, h7):
    row = lax.broadcasted_iota(jnp.int32, gate.shape, 0)
    g1 = jnp.where(row == 0, h7, pltpu.roll(gate, 1, 0))
    g2 = jnp.where(row == 0, h6, jnp.where(row == 1, h7, pltpu.roll(gate, 2, 0)))
    return g1, g2


def convglu_fwd(up, conv_w, conv_b, name):
    n_rows, f2 = up.shape
    f = f2 // 2
    tl, tc = _conv_tiles(n_rows, f)
    nc = f // tc
    hb = tl // 8

    def body(gate_ref, val_ref, halo_ref, w_ref, b_ref, o_ref):
        l = pl.program_id(1)
        live = (l > 0).astype(F32)
        h6 = halo_ref[6:7, :] * live
        h7 = halo_ref[7:8, :] * live
        gate = gate_ref[...]
        g1, g2 = _shifted(gate, h6, h7)
        gc = w_ref[0:1, :] * g2 + w_ref[1:2, :] * g1 + w_ref[2:3, :] * gate + b_ref[...]
        o_ref[...] = (_gelu(gc) * val_ref[...]).astype(o_ref.dtype)

    return pl.pallas_call(
        body, name=name, out_shape=jax.ShapeDtypeStruct((n_rows, f), BF16),
        grid=(nc, n_rows // tl),
        in_specs=[pl.BlockSpec((tl, tc), lambda c, l: (l, c)),
                  pl.BlockSpec((tl, tc), lambda c, l: (l, c + nc)),
                  pl.BlockSpec((8, tc), lambda c, l: (jnp.maximum(l * hb - 1, 0), c)),
                  pl.BlockSpec((3, tc), lambda c, l: (0, c)),
                  pl.BlockSpec((1, tc), lambda c, l: (0, c))],
        out_specs=pl.BlockSpec((tl, tc), lambda c, l: (l, c)),
        compiler_params=_params(("parallel", "arbitrary")),
    )(up, up, up, conv_w, conv_b)


def convglu_bwd(up, da, conv_w, conv_b, name):
    n_rows, f2 = up.shape
    f = f2 // 2
    tl = min(128, n_rows)
    tc = _pick(f, (512, 256, 128))
    nl = n_rows // tl
    hb = tl // 8
    last_halo = n_rows // 8 - 1
    ext = tl + 8

    def body(gate_ref, val_ref, da_ref, gprev_ref, gnext_ref, vnext_ref, danext_ref, w_ref, b_ref,
             dup_ref, dw_ref, db_ref):
        l = pl.program_id(0)
        live_prev = (l > 0).astype(F32)
        live_next = (l < nl - 1).astype(F32)

        @pl.when(l == 0)
        def _():
            dw_ref[...] = jnp.zeros_like(dw_ref)
            db_ref[...] = jnp.zeros_like(db_ref)

        for c0 in range(0, f, tc):
            cols = slice(c0, c0 + tc)
            h6 = gprev_ref[6:7, cols] * live_prev
            h7 = gprev_ref[7:8, cols] * live_prev
            w0, w1, w2 = w_ref[0:1, cols], w_ref[1:2, cols], w_ref[2:3, cols]
            gate = jnp.concatenate([gate_ref[:, cols], gnext_ref[:, cols]], axis=0)
            val = jnp.concatenate([val_ref[:, cols], vnext_ref[:, cols]], axis=0)
            dae = jnp.concatenate([da_ref[:, cols].astype(F32),
                                   danext_ref[:, cols].astype(F32) * live_next], axis=0)
            g1, g2 = _shifted(gate, h6, h7)
            gc = w0 * g2 + w1 * g1 + w2 * gate + b_ref[:, cols]
            t = jnp.tanh(GELU_C * (gc + GELU_A * gc * gc * gc))
            gelu = 0.5 * gc * (1.0 + t)
            gelu_grad = 0.5 * (1.0 + t) + 0.5 * gc * (1.0 - t * t) * GELU_C * (1.0 + 3.0 * GELU_A * gc * gc)
            dgc = dae * val * gelu_grad
            dgate = w2 * dgc + w1 * pltpu.roll(dgc, ext - 1, 0) + w0 * pltpu.roll(dgc, ext - 2, 0)
            dup_ref[:, cols] = dgate[:tl].astype(dup_ref.dtype)
            dup_ref[:, c0 + f:c0 + f + tc] = (dae * gelu)[:tl].astype(dup_ref.dtype)
            dg = dgc[:tl]
            db_ref[:, cols] += _rsum(dg)
            dw_ref[0:1, cols] += _rsum(dg * g2[:tl])
            dw_ref[1:2, cols] += _rsum(dg * g1[:tl])
            dw_ref[2:3, cols] += _rsum(dg * gate[:tl])

    def prev_blk(l):
        return jnp.maximum(l * hb - 1, 0)

    def next_blk(l):
        return jnp.minimum((l + 1) * hb, last_halo)

    dup, dw, db = pl.pallas_call(
        body, name=name,
        out_shape=[jax.ShapeDtypeStruct((n_rows, f2), BF16), jax.ShapeDtypeStruct((3, f), F32),
                   jax.ShapeDtypeStruct((1, f), F32)],
        grid=(nl,),
        in_specs=[pl.BlockSpec((tl, f), lambda l: (l, 0)),
                  pl.BlockSpec((tl, f), lambda l: (l, 1)),
                  pl.BlockSpec((tl, f), lambda l: (l, 0)),
                  pl.BlockSpec((8, f), lambda l: (prev_blk(l), 0)),
                  pl.BlockSpec((8, f), lambda l: (next_blk(l), 0)),
                  pl.BlockSpec((8, f), lambda l: (next_blk(l), 1)),
                  pl.BlockSpec((8, f), lambda l: (next_blk(l), 0)),
                  pl.BlockSpec((3, f), lambda l: (0, 0)),
                  pl.BlockSpec((1, f), lambda l: (0, 0))],
        out_specs=[pl.BlockSpec((tl, f2), lambda l: (l, 0)),
                   pl.BlockSpec((3, f), lambda l: (0, 0)),
                   pl.BlockSpec((1, f), lambda l: (0, 0))],
        compiler_params=_params(("arbitrary",)),
    )(up, up, da, up, up, up, da, conv_w, conv_b)
    return dup, dw, db


def _log_sigmoid(z):
    return jnp.minimum(z, 0.0) - jnp.log(1.0 + jnp.exp(-jnp.abs(z)))


def fgate_fwd(zf, bf, n_heads, name):
    n_rows = zf.shape[0]
    tl = min(512, n_rows)

    def body(z_ref, b_ref, c_ref, carry):
        @pl.when(pl.program_id(0) == 0)
        def _():
            carry[...] = jnp.zeros_like(carry)

        lf = _log_sigmoid(z_ref[...] + b_ref[...])
        lft = lf.T
        r = lax.broadcasted_iota(jnp.int32, (tl, tl), 0)
        c = lax.broadcasted_iota(jnp.int32, (tl, tl), 1)
        tri = (r <= c).astype(F32)
        cum = jnp.dot(lft, tri, precision=HI, preferred_element_type=F32) + carry[...]
        c_ref[...] = cum[:n_heads]
        carry[...] += jnp.sum(lft, axis=1, keepdims=True)

    return pl.pallas_call(
        body, name=name, out_shape=jax.ShapeDtypeStruct((n_heads, n_rows), F32),
        grid=(n_rows // tl,),
        in_specs=[pl.BlockSpec((tl, 128), lambda i: (i, 0)), pl.BlockSpec((1, 128), lambda i: (0, 0))],
        out_specs=pl.BlockSpec((n_heads, tl), lambda i: (0, i)),
        scratch_shapes=[pltpu.VMEM((128, 1), F32)],
        compiler_params=_params(("arbitrary",)),
    )(zf, bf)


def fgate_bwd(dc_a, dc_b, zf, bf, name):
    n_heads, n_rows = dc_a.shape
    tl = min(512, n_rows)
    nl = n_rows // tl

    def body(da_ref, db_ref, z_ref, b_ref, dz_ref, dbias_ref, carry):
        @pl.when(pl.program_id(0) == 0)
        def _():
            carry[...] = jnp.zeros_like(carry)
            dbias_ref[...] = jnp.zeros_like(dbias_ref)

        dc = da_ref[...] + db_ref[...]
        dcp = jnp.concatenate([dc, jnp.zeros((128 - n_heads, tl), F32)], axis=0)
        r = lax.broadcasted_iota(jnp.int32, (tl, tl), 0)
        c = lax.broadcasted_iota(jnp.int32, (tl, tl), 1)
        later = (c >= r).astype(F32)
        local = lax.dot_general(later, dcp, (((1,), (1,)), ((), ())), precision=HI,
                                preferred_element_type=F32)
        dlf = local + carry[...]
        carry[...] += local[0:1, :]
        z = z_ref[...] + b_ref[...]
        dz = dlf * jax.nn.sigmoid(-z)
        dz_ref[...] = dz.astype(dz_ref.dtype)
        dbias_ref[...] += _rsum(dz)

    return pl.pallas_call(
        body, name=name,
        out_shape=[jax.ShapeDtypeStruct((n_rows, 128), BF16), jax.ShapeDtypeStruct((1, 128), F32)],
        grid=(nl,),
        in_specs=[pl.BlockSpec((n_heads, tl), lambda i: (0, nl - 1 - i)),
                  pl.BlockSpec((n_heads, tl), lambda i: (0, nl - 1 - i)),
                  pl.BlockSpec((tl, 128), lambda i: (nl - 1 - i, 0)),
                  pl.BlockSpec((1, 128), lambda i: (0, 0))],
        out_specs=[pl.BlockSpec((tl, 128), lambda i: (nl - 1 - i, 0)),
                   pl.BlockSpec((1, 128), lambda i: (0, 0))],
        scratch_shapes=[pltpu.VMEM((1, 128), F32)],
        compiler_params=_params(("arbitrary",)),
    )(dc_a, dc_b, zf, bf)


def _pick_row(blk, h):
    row = lax.broadcasted_iota(jnp.int32, blk.shape, 0)
    return jnp.sum(jnp.where(row == h, blk, 0.0), axis=0, keepdims=True)


def _pick_first(blk, h):
    r = lax.broadcasted_iota(jnp.int32, blk.shape, 0)
    c = lax.broadcasted_iota(jnp.int32, blk.shape, 1)
    return jnp.sum(jnp.where((r == h) & (c == 0), blk, 0.0), keepdims=True)


def _causal(s, keys_on_rows=False, row0=0, col0=0):
    r = lax.broadcasted_iota(jnp.int32, s.shape, 0) + row0
    c = lax.broadcasted_iota(jnp.int32, s.shape, 1) + col0
    return jnp.where((r <= c) if keys_on_rows else (c <= r), s, MASK_VALUE)


def _attn_tile(n_rows):
    return min(512, n_rows)


def _attn_strips(tq):
    return 1


def attn_layouts(a):
    n_heads, n_rows = a.shape
    tq = _attn_tile(n_rows)
    return a.reshape(n_heads, n_rows // tq, tq).transpose(1, 0, 2), a.reshape(n_heads, n_rows, 1)


def flash_fwd(q, k, v, c_rows, name):
    n_rows, d = q.shape
    n_heads = d // HEAD_DIM
    tq = _attn_tile(n_rows)
    nq = n_rows // tq
    scale = HEAD_DIM ** -0.5

    n_strips = _attn_strips(tq)
    ts = tq // n_strips

    def body(q_ref, k_ref, v_ref, c_ref, o_ref, o32_ref, lse_ref):
        h, i = pl.program_id(0), pl.program_id(1)
        c_q = _pick_first(c_ref[i], h)

        def block(j, carry, diagonal):
            rows = pl.ds(pl.multiple_of(j * tq, tq), tq)
            kk, vv = k_ref[rows, :], v_ref[rows, :]
            bias = c_q - _pick_row(c_ref[j], h)
            out = []
            for r in range(n_strips):
                m_prev, l_prev, acc = carry[r]
                s = lax.dot_general(q_ref[r * ts:(r + 1) * ts, :], kk, (((1,), (1,)), ((), ())),
                                    preferred_element_type=F32) * scale + bias
                if diagonal:
                    s = _causal(s, row0=r * ts)
                m_new = jnp.maximum(m_prev, jnp.max(s, axis=1, keepdims=True))
                p = jnp.exp(s - m_new)
                alpha = jnp.exp(m_prev - m_new)
                p_hi = p.astype(BF16)
                p_lo = (p - p_hi.astype(F32)).astype(BF16)
                pv = (jnp.dot(p_hi, vv, preferred_element_type=F32)
                      + jnp.dot(p_lo, vv, preferred_element_type=F32))
                out.append((m_new, alpha * l_prev + jnp.sum(p, axis=1, keepdims=True), alpha * acc + pv))
            return tuple(out)

        init = tuple((jnp.full((ts, 1), MASK_VALUE, F32), jnp.zeros((ts, 1), F32),
                      jnp.zeros((ts, HEAD_DIM), F32)) for _ in range(n_strips))
        carry = lax.fori_loop(0, i, lambda j, c: block(j, c, False), init)
        for r, (m_fin, l_fin, acc) in enumerate(block(i, carry, True)):
            o = acc / l_fin
            o_ref[r * ts:(r + 1) * ts, :] = o.astype(o_ref.dtype)
            o32_ref[r * ts:(r + 1) * ts, :] = o
            lse_ref[0, r * ts:(r + 1) * ts, :] = m_fin + jnp.log(l_fin)

    qspec = pl.BlockSpec((tq, HEAD_DIM), lambda h, i: (i, h))
    kspec = pl.BlockSpec((n_rows, HEAD_DIM), lambda h, i: (0, h))
    return pl.pallas_call(
        body, name=name,
        out_shape=[jax.ShapeDtypeStruct((n_rows, d), BF16), jax.ShapeDtypeStruct((n_rows, d), F32),
                   jax.ShapeDtypeStruct((n_heads, n_rows, 1), F32)],
        grid=(n_heads, nq),
        in_specs=[qspec, kspec, kspec, pl.BlockSpec((nq, n_heads, tq), lambda h, i: (0, 0, 0))],
        out_specs=[qspec, qspec, pl.BlockSpec((1, tq, 1), lambda h, i: (h, i, 0))],
        compiler_params=_params(("parallel", "arbitrary")),
    )(q, k, v, c_rows)


def attn_delta(do, o, name):
    n_rows, d = o.shape
    n_heads = d // HEAD_DIM
    tq = min(512, n_rows)

    def body(do_ref, o_ref, out_ref):
        out_ref[0] = jnp.sum(do_ref[...].astype(F32) * o_ref[...].astype(F32), axis=1, keepdims=True)

    spec = pl.BlockSpec((tq, HEAD_DIM), lambda i, h: (i, h))
    return pl.pallas_call(
        body, name=name, out_shape=jax.ShapeDtypeStruct((n_heads, n_rows, 1), F32),
        grid=(n_rows // tq, n_heads), in_specs=[spec, spec],
        out_specs=pl.BlockSpec((1, tq, 1), lambda i, h: (h, i, 0)),
        compiler_params=_params(("parallel", "parallel")),
    )(do, o)


def flash_bwd_dq(q, k, v, do, c_rows, lse, delta, name):
    n_rows, d = q.shape
    n_heads = d // HEAD_DIM
    tq = _attn_tile(n_rows)
    nq = n_rows // tq
    scale = HEAD_DIM ** -0.5

    def body(q_ref, k_ref, v_ref, do_ref, c_ref, lse_ref, dl_ref, dq_ref):
        h, i = pl.program_id(0), pl.program_id(1)
        qv, dov = q_ref[...], do_ref[...]
        lse_q, dl_q = lse_ref[0], dl_ref[0]
        c_q = _pick_first(c_ref[i], h)

        def block(j, acc, diagonal):
            rows = pl.ds(pl.multiple_of(j * tq, tq), tq)
            kk = k_ref[rows, :]
            s = lax.dot_general(qv, kk, (((1,), (1,)), ((), ())), preferred_element_type=F32) * scale
            s = s + (c_q - _pick_row(c_ref[j], h))
            if diagonal:
                s = _causal(s)
            p = jnp.exp(s - lse_q)
            dp = lax.dot_general(dov, v_ref[rows, :], (((1,), (1,)), ((), ())), preferred_element_type=F32)
            ds = p * (dp - dl_q)
            return acc + jnp.dot(ds.astype(BF16), kk, preferred_element_type=F32)

        acc = lax.fori_loop(0, i, lambda j, a: block(j, a, False), jnp.zeros((tq, HEAD_DIM), F32))
        acc = block(i, acc, True)
        dq_ref[...] = (acc * scale).astype(dq_ref.dtype)

    qspec = pl.BlockSpec((tq, HEAD_DIM), lambda h, i: (i, h))
    kspec = pl.BlockSpec((n_rows, HEAD_DIM), lambda h, i: (0, h))
    vec = pl.BlockSpec((1, tq, 1), lambda h, i: (h, i, 0))
    return pl.pallas_call(
        body, name=name, out_shape=jax.ShapeDtypeStruct((n_rows, d), BF16),
        grid=(n_heads, nq),
        in_specs=[qspec, kspec, kspec, qspec, pl.BlockSpec((nq, n_heads, tq), lambda h, i: (0, 0, 0)), vec, vec],
        out_specs=qspec,
        compiler_params=_params(("parallel", "arbitrary")),
    )(q, k, v, do, c_rows, lse, delta)


def flash_bwd_dkv(q, k, v, do, c_rows, c_cols, lse_rows, delta_rows, name):
    n_rows, d = q.shape
    n_heads = d // HEAD_DIM
    tq = _attn_tile(n_rows)
    nq = n_rows // tq
    scale = HEAD_DIM ** -0.5

    def body(q_ref, k_ref, v_ref, do_ref, c_ref, ck_ref, lse_ref, dl_ref, dk_ref, dv_ref, dc_ref):
        h, j = pl.program_id(0), pl.program_id(1)
        kk, vv = k_ref[...], v_ref[...]
        c_k = ck_ref[0]

        def block(i, carry, diagonal):
            dk, dv, dc = carry
            rows = pl.ds(pl.multiple_of(i * tq, tq), tq)
            qi, doi = q_ref[rows, :], do_ref[rows, :]
            st = lax.dot_general(kk, qi, (((1,), (1,)), ((), ())), preferred_element_type=F32) * scale
            st = st + (_pick_first(c_ref[i], h) - c_k)
            if diagonal:
                st = _causal(st, keys_on_rows=True)
            pt = jnp.exp(st - _pick_row(lse_ref[i], h))
            dpt = lax.dot_general(vv, doi, (((1,), (1,)), ((), ())), preferred_element_type=F32)
            dst = pt * (dpt - _pick_row(dl_ref[i], h))
            dv = dv + jnp.dot(pt.astype(BF16), doi, preferred_element_type=F32)
            dk = dk + jnp.dot(dst.astype(BF16), qi, preferred_element_type=F32)
            return dk, dv, dc - jnp.sum(dst, axis=1, keepdims=True)

        zero = jnp.zeros((tq, HEAD_DIM), F32)
        carry = block(j, (zero, zero, jnp.zeros((tq, 1), F32)), True)
        dk, dv, dc = lax.fori_loop(j + 1, nq, lambda i, c: block(i, c, False), carry)
        dk_ref[...] = dk * scale
        dv_ref[...] = dv
        dc_ref[0] = dc

    kspec = pl.BlockSpec((tq, HEAD_DIM), lambda h, j: (j, h))
    qspec = pl.BlockSpec((n_rows, HEAD_DIM), lambda h, j: (0, h))
    rows3 = pl.BlockSpec((nq, n_heads, tq), lambda h, j: (0, 0, 0))
    col = pl.BlockSpec((1, tq, 1), lambda h, j: (h, j, 0))
    return pl.pallas_call(
        body, name=name,
        out_shape=[jax.ShapeDtypeStruct((n_rows, d), F32), jax.ShapeDtypeStruct((n_rows, d), F32),
                   jax.ShapeDtypeStruct((n_heads, n_rows, 1), F32)],
        grid=(n_heads, nq),
        in_specs=[qspec, kspec, kspec, qspec, rows3, col, rows3, rows3],
        out_specs=[kspec, kspec, col],
        compiler_params=_params(("parallel", "arbitrary")),
    )(q, k, v, do, c_rows, c_cols, lse_rows, delta_rows)


S5_PARAM_SHAPES = [(1, S5_2P)] * 3 + [(S5_2P, 1)] * 3 + [(S5_GROUP, S5_2P)] * 4 + [(S5_2P, S5_TC)] * 4 + [(S5_TC, 1)]


def _tile_rows(a, reps):
    return jnp.concatenate([a] * reps, axis=0)


def _s5_build(lrl, lil, lsl, lrc, lic, lsc, cri, cir, bt2, bt2s, brit, birt, ctt, ctts):
    first = lax.broadcasted_iota(jnp.int32, (1, S5_2P), 1) < S5_STATE
    sgn = jnp.where(first, -1.0, 1.0)
    dt_l = jnp.exp(lsl)
    a_l, b_l = lrl * dt_l, lil * dt_l
    mag1 = jnp.exp(a_l)
    lbr, lbi = mag1 * jnp.cos(b_l), mag1 * jnp.sin(b_l)
    n = lrl * lrl + lil * lil
    kr = ((lbr - 1.0) * lrl + lbi * lil) / n
    ki = (lbi * lrl - (lbr - 1.0) * lil) / n
    blk = lax.shift_right_logical(lax.broadcasted_iota(jnp.int32, (S5_TC, 1), 0), 4).astype(F32)
    e_end = (S5_CHUNK - 1.0) - blk
    mag_e = jnp.exp(e_end * a_l)
    pr, pi = mag_e * jnp.cos(e_end * b_l), mag_e * jnp.sin(e_end * b_l)
    cr, ci = pr * kr - pi * ki, pr * ki + pi * kr
    w_end = cr * _tile_rows(bt2, S5_CHUNK) + (sgn * ci) * _tile_rows(bt2s, S5_CHUNK)
    mag_j = jnp.exp(blk * a_l)
    qr, qi = mag_j * jnp.cos(blk * b_l), mag_j * jnp.sin(blk * b_l)
    xmat = qr * _tile_rows(cri, S5_CHUNK) + (sgn * qi) * _tile_rows(cir, S5_CHUNK)
    dt_c = jnp.exp(lsc)
    a_c, b_c = lrc * dt_c, lic * dt_c
    m1c = jnp.exp(a_c)
    lbrc, lbic = m1c * jnp.cos(b_c), m1c * jnp.sin(b_c)
    n_c = lrc * lrc + lic * lic
    krc = ((lbrc - 1.0) * lrc + lbic * lic) / n_c
    kic = (lbic * lrc - (lbrc - 1.0) * lic) / n_c
    top = lax.broadcasted_iota(jnp.int32, (S5_2P, 1), 0) < S5_STATE
    rhs = jnp.where(top, krc, -krc) * brit + (-kic) * birt
    z = jnp.dot(xmat, rhs, precision=HI, preferred_element_type=F32)
    e_out = (lax.shift_right_logical(lax.broadcasted_iota(jnp.int32, (1, S5_TC), 1), 4) + 1).astype(F32)
    mag_o = jnp.exp(a_c * e_out)
    o_r, o_i = mag_o * jnp.cos(b_c * e_out), mag_o * jnp.sin(b_c * e_out)
    w_out = jnp.where(top, o_r, -o_r) * ctt + (-o_i) * ctts
    m16 = jnp.exp(float(S5_CHUNK) * a_l)
    a1 = m16 * jnp.cos(float(S5_CHUNK) * b_l)
    a2 = sgn * (m16 * jnp.sin(float(S5_CHUNK) * b_l))
    return z, w_end, w_out, a1, a2


def _lane_block_bit(b):
    blk = lax.shift_right_logical(lax.broadcasted_iota(jnp.int32, (S5_TC, S5_TC), 1), 4)
    return (lax.shift_right_logical(blk, b) & 1) == 1


def _toeplitz(z, dcol):
    for b in range(4):
        s = S5_GROUP << b
        moved = jnp.concatenate([jnp.zeros((s, S5_TC), F32), z[:S5_TC - s]], axis=0)
        z = jnp.where(_lane_block_bit(b), moved, z)
    r = lax.broadcasted_iota(jnp.int32, (S5_TC, S5_TC), 0)
    c = lax.broadcasted_iota(jnp.int32, (S5_TC, S5_TC), 1)
    return z + jnp.where(r == c, dcol, 0.0)


def _toeplitz_t(dt):
    r = lax.broadcasted_iota(jnp.int32, (S5_TC, S5_TC), 0)
    c = lax.broadcasted_iota(jnp.int32, (S5_TC, S5_TC), 1)
    ddcol = jnp.sum(jnp.where(r == c, dt, 0.0), axis=1, keepdims=True)
    for b in range(4):
        s = S5_GROUP << b
        moved = jnp.concatenate([dt[s:], jnp.zeros((s, S5_TC), F32)], axis=0)
        dt = jnp.where(_lane_block_bit(b), moved, dt)
    return dt, ddcol


def _s5_powers(a1, a2, n):
    first = lax.broadcasted_iota(jnp.int32, a1.shape, 1) < S5_STATE
    ar, ai = a1, jnp.where(first, -a2, a2)
    out = []
    d = 1
    while d < n:
        out.append((ar, jnp.where(first, -ai, ai)))
        ar, ai = ar * ar - ai * ai, 2.0 * ar * ai
        d *= 2
    return out


def _s5_prefix(e, pows, reverse):
    n = e.shape[0]
    row = lax.broadcasted_iota(jnp.int32, e.shape, 0)
    x = e
    for b, (p1, p2) in enumerate(pows):
        d = 1 << b
        if reverse:
            sh = jnp.where(row < n - d, pltpu.roll(x, n - d, 0), 0.0)
            x = x + p1 * sh - p2 * pltpu.roll(sh, S5_STATE, 1)
        else:
            sh = jnp.where(row >= d, pltpu.roll(x, d, 0), 0.0)
            x = x + p1 * sh + p2 * pltpu.roll(sh, S5_STATE, 1)
    return x


def _s5_scan(e, pows):
    n = e.shape[0]
    row = lax.broadcasted_iota(jnp.int32, e.shape, 0)
    return jnp.where(row >= 1, pltpu.roll(_s5_prefix(e, pows, False), 1, 0), 0.0)


def _s5_scan_t(dsin, pows):
    n = dsin.shape[0]
    row = lax.broadcasted_iota(jnp.int32, dsin.shape, 0)
    return jnp.where(row < n - 1, pltpu.roll(_s5_prefix(dsin, pows, True), n - 1, 0), 0.0)


def _dot16(a, b, dims):
    return lax.dot_general(a.astype(BF16), b.astype(BF16), (dims, ((), ())), preferred_element_type=F32)


_NN = ((1,), (0,))
_NT = ((1,), (1,))
_TN = ((0,), (0,))


def _s5_specs(gb):
    return [pl.BlockSpec((gb, *s), lambda i: (i, 0, 0)) for s in S5_PARAM_SHAPES]


def s5_fwd(u, prm, name):
    n_groups, nk, _ = u.shape
    gb = min(4, n_groups)
    n_prm = len(S5_PARAM_SHAPES)

    def body(*refs):
        u_ref = refs[0]
        p_refs = refs[1:1 + n_prm]
        o_ref = refs[1 + n_prm]
        sin_sc, toep_sc = refs[2 + n_prm:]
        for g in range(gb):
            z, w_end, w_out, a1, a2 = _s5_build(*[r[g] for r in p_refs[:-1]])
            toep_sc[...] = _toeplitz(z, p_refs[-1][g])
            ug = u_ref[g].astype(F32)
            sin_sc[...] = _s5_scan(_dot16(ug, w_end, _NN), _s5_powers(a1, a2, nk))
            y = _dot16(ug, toep_sc[...], _NT) + _dot16(sin_sc[...], w_out, _NN)
            o_ref[g] = _gelu(y).astype(o_ref.dtype)

    blk = pl.BlockSpec((gb, nk, S5_TC), lambda i: (i, 0, 0))
    return pl.pallas_call(
        body, name=name, out_shape=jax.ShapeDtypeStruct(u.shape, BF16), grid=(n_groups // gb,),
        in_specs=[blk] + _s5_specs(gb), out_specs=blk,
        scratch_shapes=[pltpu.VMEM((nk, S5_2P), F32), pltpu.VMEM((S5_TC, S5_TC), F32)],
        compiler_params=_params(("parallel",)),
    )(u, *prm)


def s5_bwd(u, dyg, prm, name):
    n_groups, nk, _ = u.shape
    gb = min(4, n_groups)
    n_prm = len(S5_PARAM_SHAPES)

    def body(*refs):
        u_ref, dyg_ref = refs[0], refs[1]
        p_refs = refs[2:2 + n_prm]
        du_ref = refs[2 + n_prm]
        dp_refs = refs[3 + n_prm:3 + 2 * n_prm]
        sin_sc, de_sc, dy_sc, toep_sc = refs[3 + 2 * n_prm:]
        for g in range(gb):
            outs, vjp = jax.vjp(_s5_build, *[r[g] for r in p_refs[:-1]])
            z, w_end, w_out, a1, a2 = outs
            toep_sc[...] = _toeplitz(z, p_refs[-1][g])
            pows = _s5_powers(a1, a2, nk)
            ug = u_ref[g].astype(F32)
            sin_sc[...] = _s5_scan(_dot16(ug, w_end, _NN), pows)
            y = _dot16(ug, toep_sc[...], _NT) + _dot16(sin_sc[...], w_out, _NN)
            dy_sc[...] = dyg_ref[g].astype(F32) * _gelu_grad(y)
            dy = dy_sc[...]
            de_sc[...] = _s5_scan_t(_dot16(dy, w_out, _NT), pows)
            de, sin = de_sc[...], sin_sc[...]
            du_ref[g] = (_dot16(dy, toep_sc[...], _NN) + _dot16(de, w_end, _NT)).astype(du_ref.dtype)
            dz, ddcol = _toeplitz_t(_dot16(dy, ug, _TN))
            da1 = jnp.sum(de * sin, axis=0, keepdims=True)
            da2 = jnp.sum(de * pltpu.roll(sin, S5_STATE, 1), axis=0, keepdims=True)
            grads = vjp((dz, _dot16(ug, de, _TN), _dot16(sin, dy, _TN), da1, da2))
            for r, v in zip(dp_refs[:-1], grads):
                r[g] = v
            dp_refs[-1][g] = ddcol

    blk = pl.BlockSpec((gb, nk, S5_TC), lambda i: (i, 0, 0))
    res = pl.pallas_call(
        body, name=name,
        out_shape=[jax.ShapeDtypeStruct(u.shape, BF16)]
        + [jax.ShapeDtypeStruct((n_groups, *s), F32) for s in S5_PARAM_SHAPES],
        grid=(n_groups // gb,),
        in_specs=[blk, blk] + _s5_specs(gb), out_specs=[blk] + _s5_specs(gb),
        scratch_shapes=[pltpu.VMEM((nk, S5_2P), F32), pltpu.VMEM((nk, S5_2P), F32),
                        pltpu.VMEM((nk, S5_TC), F32), pltpu.VMEM((S5_TC, S5_TC), F32)],
        compiler_params=_params(("parallel",)),
    )(u, dyg, *prm)
    return res[0], res[1:]


def s5_prepare(lam_re, lam_im, log_step, b_re, b_im, c_re, c_im, d):
    g = lam_re.shape[0]
    lane = lambda a: a.reshape(g, 1, -1)
    col = lambda a: a.reshape(g, -1, 1)
    ls = jnp.broadcast_to(log_step[:, None], (g, S5_2P))
    brt = jnp.swapaxes(b_re, 1, 2)
    bit = jnp.swapaxes(b_im, 1, 2)
    crt = jnp.swapaxes(c_re, 1, 2)
    cit = jnp.swapaxes(c_im, 1, 2)
    cat = jnp.concatenate
    return [
        lane(cat([lam_re, lam_re], 1)), lane(cat([lam_im, lam_im], 1)), lane(ls),
        col(cat([lam_re, lam_re], 1)), col(cat([lam_im, lam_im], 1)), col(ls),
        cat([c_re, c_im], 2), cat([c_im, c_re], 2),
        cat([brt, bit], 2), cat([bit, brt], 2),
        jnp.tile(cat([b_re, b_im], 1), (1, 1, S5_CHUNK)), jnp.tile(cat([b_im, b_re], 1), (1, 1, S5_CHUNK)),
        jnp.tile(cat([crt, cit], 1), (1, 1, S5_CHUNK)), jnp.tile(cat([cit, crt], 1), (1, 1, S5_CHUNK)),
        col(jnp.tile(d, (1, S5_CHUNK))),
    ]


def s5_param_grads(prm, grads):
    _, vjp = jax.vjp(s5_prepare, *prm)
    return vjp(list(grads))


def s5_to_groups(a):
    n_rows, d = a.shape
    g, nk = d // S5_GROUP, n_rows // S5_CHUNK
    return a.reshape(nk, S5_CHUNK, g, S5_GROUP).transpose(2, 0, 1, 3).reshape(g, nk, S5_TC)


def s5_from_groups(a):
    g, nk, _ = a.shape
    return a.reshape(g, nk, S5_CHUNK, S5_GROUP).transpose(1, 2, 0, 3).reshape(nk * S5_CHUNK, g * S5_GROUP)


def _adamw_math(w, g, m, v):
    m = ADAM_B1 * m + (1.0 - ADAM_B1) * g
    v = ADAM_B2 * v + (1.0 - ADAM_B2) * (g * g)
    m_hat = m / (1.0 - ADAM_B1 ** ADAM_STEP)
    v_hat = v / (1.0 - ADAM_B2 ** ADAM_STEP)
    delta = -ADAM_LR * (m_hat / (jnp.sqrt(v_hat) + ADAM_EPS) + ADAM_WD * w)
    return delta, m, v


def adamw_sum(parts, w, m, v, name):
    r, c = w.shape
    n_parts = parts.shape[0]
    tr = _pick(r, (128, 64, 32, 16, 8))

    def body(p_ref, w_ref, m_ref, v_ref, g_out, d_out, m_out, v_out):
        g = p_ref[0].astype(F32)
        for s in range(1, n_parts):
            g = g + p_ref[s].astype(F32)
        delta, m_new, v_new = _adamw_math(w_ref[...], g, m_ref[...], v_ref[...])
        g_out[...] = g
        d_out[...] = delta
        m_out[...] = m_new
        v_out[...] = v_new

    spec = pl.BlockSpec((tr, c), lambda i: (i, 0))
    return pl.pallas_call(
        body, name=name, out_shape=[jax.ShapeDtypeStruct((r, c), F32)] * 4, grid=(r // tr,),
        in_specs=[pl.BlockSpec((n_parts, tr, c), lambda i: (0, i, 0)), spec, spec, spec],
        out_specs=[spec] * 4, compiler_params=_params(("parallel",)),
    )(parts, w, m, v)


def sum8(parts, name):
    r, c = parts.shape[1:]
    tr = _pick(r, (256, 128, 64, 32, 16, 8))

    def body(p_ref, o_ref):
        g = p_ref[0]
        for s in range(1, N_DEV):
            g = g + p_ref[s]
        o_ref[...] = g

    return pl.pallas_call(
        body, name=name, out_shape=jax.ShapeDtypeStruct((r, c), F32), grid=(r // tr,),
        in_specs=[pl.BlockSpec((N_DEV, tr, c), lambda i: (0, i, 0))],
        out_specs=pl.BlockSpec((tr, c), lambda i: (i, 0)), compiler_params=_params(("parallel",)),
    )(parts)


def adamw_plain(g, w, m, v, name):
    r, c = w.shape
    tr = _pick(r, (256, 128, 64, 32, 16, 8))

    def body(g_ref, w_ref, m_ref, v_ref, d_out, m_out, v_out):
        delta, m_new, v_new = _adamw_math(w_ref[...], g_ref[...], m_ref[...], v_ref[...])
        d_out[...] = delta
        m_out[...] = m_new
        v_out[...] = v_new

    spec = pl.BlockSpec((tr, c), lambda i: (i, 0))
    return pl.pallas_call(
        body, name=name, out_shape=[jax.ShapeDtypeStruct((r, c), F32)] * 3, grid=(r // tr,),
        in_specs=[spec] * 4, out_specs=[spec] * 3, compiler_params=_params(("parallel",)),
    )(g, w, m, v)


PACK_LANES = 1024


def _pack(arrays):
    pieces = []
    for a in arrays:
        flat = a.reshape(-1).astype(F32)
        pieces.append(jnp.pad(flat, (0, (-flat.shape[0]) % PACK_LANES)))
    flat = jnp.concatenate(pieces)
    return jnp.pad(flat, (0, (-flat.shape[0]) % (8 * PACK_LANES))).reshape(-1, PACK_LANES)


def _unpack(buf, shapes):
    flat = buf.reshape(-1)
    out, off = [], 0
    for s in shapes:
        n = math.prod(s)
        out.append(flat[off:off + n].reshape(s))
        off += n + ((-n) % PACK_LANES)
    return out


WEIGHT_NAMES = ['a_norm', 'a_w_in', 'a_lambda_re', 'a_lambda_im', 'a_log_step', 'a_b_re', 'a_b_im', 'a_c_re',
                'a_c_im', 'a_d', 'a_w_glu', 'kv_norm', 'w_k', 'w_v', 'w_f', 'b_f', 'b_norm', 'b_w_q', 'b_w_o',
                'ffn_norm', 'ffn_w_up', 'ffn_conv_w', 'ffn_conv_b', 'ffn_w_down', 'final_norm']
BIG = {'a_w_in': 0, 'a_w_glu': 1, 'w_k': 0, 'w_v': 0, 'b_w_q': 0, 'b_w_o': 0, 'ffn_w_up': 1, 'ffn_w_down': 0}
SMALL_SHARDED = {'a_norm': 1, 'w_f': 0, 'ffn_conv_w': 2}


def _ffn_fwd(x, gain, w_up, conv_w, conv_b, w_down, tag):
    h = norm_fwd(x, gain, f"norm_fwd_{tag}")
    up = matmul(h, w_up, name=f"mm_up_{tag}")
    a = convglu_fwd(up, conv_w, conv_b, f"convglu_fwd_{tag}")
    x_out = matmul(a, w_down, add=x, name=f"mm_down_{tag}")
    return x_out, (x, h, up, a)


def _ffn_bwd(dx, dxb, saved, gain, w_up, conv_w, conv_b, w_down, tag):
    x, h, up, a = saved
    dw_down = matmul(a, dxb, ta=True, out_dtype=BF16, name=f"mm_dwdown_{tag}")
    da = matmul(dxb, w_down, tb=True, name=f"mm_da_{tag}")
    dup, dconv_w, dconv_b = convglu_bwd(up, da, conv_w, conv_b, f"convglu_bwd_{tag}")
    dw_up = matmul(h, dup, ta=True, out_dtype=BF16, name=f"mm_dwup_{tag}")
    dh = matmul(dup, w_up, tb=True, name=f"mm_dh_{tag}")
    dx_in, dxb_in, dgain = norm_bwd(x, gain, dh, dx, f"norm_bwd_{tag}")
    return dx_in, dxb_in, dict(norm=dgain, w_up=dw_up, conv_w=dconv_w, conv_b=dconv_b, w_down=dw_down)


def kernel(x, a_norm, a_w_in, a_lambda_re, a_lambda_im, a_log_step, a_b_re, a_b_im, a_c_re, a_c_im, a_d, a_w_glu, kv_norm, w_k, w_v, w_f, b_f, b_norm, b_w_q, b_w_o, ffn_norm, ffn_w_up, ffn_conv_w, ffn_conv_b, ffn_w_down, final_norm, loss_target, m_a_norm, m_a_w_in, m_a_lambda_re, m_a_lambda_im, m_a_log_step, m_a_b_re, m_a_b_im, m_a_c_re, m_a_c_im, m_a_d, m_a_w_glu, m_kv_norm, m_w_k, m_w_v, m_w_f, m_b_f, m_b_norm, m_b_w_q, m_b_w_o, m_ffn_norm, m_ffn_w_up, m_ffn_conv_w, m_ffn_conv_b, m_ffn_w_down, m_final_norm, v_a_norm, v_a_w_in, v_a_lambda_re, v_a_lambda_im, v_a_log_step, v_a_b_re, v_a_b_im, v_a_c_re, v_a_c_im, v_a_d, v_a_w_glu, v_kv_norm, v_w_k, v_w_v, v_w_f, v_b_f, v_b_norm, v_b_w_q, v_b_w_o, v_ffn_norm, v_ffn_w_up, v_ffn_conv_w, v_ffn_conv_b, v_ffn_w_down, v_final_norm):
    args = locals()
    w = {n: args[n] for n in WEIGHT_NAMES}
    mom = {n: args["m_" + n] for n in WEIGHT_NAMES}
    var = {n: args["v_" + n] for n in WEIGHT_NAMES}
    x0 = x[0]
    target = loss_target[0]
    n_rows, d = x0.shape
    n_heads = d // HEAD_DIM
    n_a = a_w_in.shape[0]
    n_b = b_w_q.shape[0]
    me = _my_index()

    def gathered(name, layer=None):
        ws = w[name] if layer is None else w[name][layer]
        tag = name if layer is None else f"{name}{layer}"
        return all_gather(ws.astype(BF16), BIG[name], f"ag_{tag}")

    small_local = [w[n] for n in SMALL_SHARDED]
    small_full = all_gather(_pack(small_local)[None], 0, "ag_small")
    per_dev = [_unpack(small_full[s], [a.shape for a in small_local]) for s in range(N_DEV)]
    full_small = {n: jnp.concatenate([per_dev[s][i] for s in range(N_DEV)], axis=SMALL_SHARDED[n])
                  for i, n in enumerate(SMALL_SHARDED)}
    a_norm_f, w_f_f, conv_w_f = full_small['a_norm'], full_small['w_f'], full_small['ffn_conv_w']
    w_f_pad = jnp.pad(w_f_f, ((0, 0), (0, 128 - n_heads))).astype(BF16)
    b_f_pad = jnp.pad(b_f, (0, 128 - n_heads)).reshape(1, 128)

    grads = {}
    big_grads = {}

    xs = x0
    saved_a, saved_b, saved_ffn = [], [], []
    weights_a, weights_b, weights_ffn = [], [], []
    for i in range(n_a):
        w_in = gathered('a_w_in', i)
        w_glu = gathered('a_w_glu', i)
        prm = s5_prepare(a_lambda_re[i], a_lambda_im[i], a_log_step[i], a_b_re[i], a_b_im[i],
                         a_c_re[i], a_c_im[i], a_d[i])
        gain = a_norm_f[i].reshape(1, d)
        h = norm_fwd(xs, gain, f"norm_fwd_a{i}")
        u = s5_to_groups(matmul(h, w_in, out_dtype=BF16, name=f"mm_win_{i}"))
        yg = s5_from_groups(s5_fwd(u, prm, f"s5_fwd_{i}"))
        z = matmul(yg, w_glu, name=f"mm_wglu_{i}")
        x_mid = glu_fwd(z, xs, f"glu_fwd_{i}")
        saved_a.append((xs, h, u, yg, z))
        weights_a.append((w_in, w_glu, prm, gain))
        fw = (ffn_norm[i].reshape(1, d), gathered('ffn_w_up', i), conv_w_f[i], ffn_conv_b[i].reshape(1, -1),
              gathered('ffn_w_down', i))
        xs, sv = _ffn_fwd(x_mid, *fw, f"f{i}")
        saved_ffn.append(sv)
        weights_ffn.append(fw)

    x_kv = xs
    kv_gain = kv_norm.reshape(1, d)
    wk_f, wv_f = gathered('w_k'), gathered('w_v')
    h_kv = norm_fwd(x_kv, kv_gain, "norm_fwd_kv")
    k = matmul(h_kv, wk_f, out_dtype=BF16, name="mm_wk")
    v = matmul(h_kv, wv_f, out_dtype=BF16, name="mm_wv")
    zf = matmul(h_kv, w_f_pad, name="mm_wf")
    ct = fgate_fwd(zf, b_f_pad, n_heads, "fgate_fwd")
    c_rows, c_cols = attn_layouts(ct)

    for j in range(n_b):
        li = n_a + j
        w_q, w_o = gathered('b_w_q', j), gathered('b_w_o', j)
        gain = b_norm[j].reshape(1, d)
        h = norm_fwd(xs, gain, f"norm_fwd_b{j}")
        q = matmul(h, w_q, out_dtype=BF16, name=f"mm_wq_{j}")
        o, o32, lse = flash_fwd(q, k, v, c_rows, f"flash_fwd_{j}")
        x_mid = matmul(o, w_o, add=xs, name=f"mm_wo_{j}")
        saved_b.append((xs, h, q, o, o32, lse))
        weights_b.append((w_q, w_o, gain))
        fw = (ffn_norm[li].reshape(1, d), gathered('ffn_w_up', li), conv_w_f[li], ffn_conv_b[li].reshape(1, -1),
              gathered('ffn_w_down', li))
        xs, sv = _ffn_fwd(x_mid, *fw, f"f{li}")
        saved_ffn.append(sv)
        weights_ffn.append(fw)

    dx, dxb, d_final, sq = loss_and_grad(xs, final_norm.reshape(1, d), target, "loss")
    loss = lax.psum(0.5 * jnp.sum(sq) / d, MESH_AXES)
    grads['final_norm'] = d_final.reshape(d)

    ffn_g = [None] * (n_a + n_b)
    b_g = [None] * n_b
    dk_tot = dv_tot = dc_tot = None
    for j in reversed(range(n_b)):
        li = n_a + j
        dx, dxb, ffn_g[li] = _ffn_bwd(dx, dxb, saved_ffn[li], *weights_ffn[li], f"f{li}")
        xs_in, h, q, o, o32, lse = saved_b[j]
        w_q, w_o, gain = weights_b[j]
        dw_o = matmul(o, dxb, ta=True, out_dtype=BF16, name=f"mm_dwo_{j}")
        do = matmul(dxb, w_o, tb=True, out_dtype=BF16, name=f"mm_do_{j}")
        delta = attn_delta(do, o32, f"attn_delta_{j}")
        dq = flash_bwd_dq(q, k, v, do, c_rows, lse, delta, f"flash_dq_{j}")
        lse_rows = attn_layouts(lse.reshape(n_heads, n_rows))[0]
        delta_rows = attn_layouts(delta.reshape(n_heads, n_rows))[0]
        dk_j, dv_j, dc_j = flash_bwd_dkv(q, k, v, do, c_rows, c_cols, lse_rows, delta_rows, f"flash_dkv_{j}")
        dc_j = dc_j.reshape(n_heads, n_rows)
        if dk_tot is None:
            dk_tot, dv_tot, dc_tot = [dk_j], [dv_j], [dc_j]
        else:
            dk_tot.append(dk_j), dv_tot.append(dv_j), dc_tot.append(dc_j)
        dw_q = matmul(h, dq, ta=True, out_dtype=BF16, name=f"mm_dwq_{j}")
        dh = matmul(dq, w_q, tb=True, name=f"mm_dhq_{j}")
        dx, dxb, dgain = norm_bwd(xs_in, gain, dh, dx, f"norm_bwd_b{j}")
        b_g[j] = dict(norm=dgain, w_q=dw_q, w_o=dw_o)

    def total(parts, tag):
        if len(parts) == 1:
            return parts[0].astype(BF16)
        acc = parts[0]
        for n_, p_ in enumerate(parts[1:-1]):
            acc = acc + p_
        return add_cast(acc, parts[-1], f"add_{tag}")

    dk_b, dv_b = total(dk_tot, "dk"), total(dv_tot, "dv")
    dc_a = dc_tot[0]
    dc_b = dc_tot[1] if len(dc_tot) > 1 else jnp.zeros_like(dc_a)
    for extra in dc_tot[2:]:
        dc_b = dc_b + extra
    dzf, db_f = fgate_bwd(dc_a, dc_b, zf, b_f_pad, "fgate_bwd")
    big_grads[('w_k', None)] = matmul(h_kv, dk_b, ta=True, out_dtype=BF16, name="mm_dwk")
    big_grads[('w_v', None)] = matmul(h_kv, dv_b, ta=True, out_dtype=BF16, name="mm_dwv")
    dw_f = matmul(h_kv, dzf, ta=True, name="mm_dwf")[:, :n_heads]
    dh_kv = matmul(dk_b, wk_f, tb=True, name="mm_dhk")
    dh_kv = matmul(dv_b, wv_f, tb=True, add=dh_kv, name="mm_dhv")
    dh_kv = matmul(dzf, w_f_pad, tb=True, add=dh_kv, name="mm_dhf")
    dx, dxb, d_kv_gain = norm_bwd(x_kv, kv_gain, dh_kv, dx, "norm_bwd_kv")
    grads['kv_norm'] = d_kv_gain.reshape(d)
    grads['b_f'] = db_f[0, :n_heads]

    a_g = [None] * n_a
    for i in reversed(range(n_a)):
        dx, dxb, ffn_g[i] = _ffn_bwd(dx, dxb, saved_ffn[i], *weights_ffn[i], f"f{i}")
        xs_in, h, u, yg, z = saved_a[i]
        w_in, w_glu, prm, gain = weights_a[i]
        dz = glu_bwd(z, dx, f"glu_bwd_{i}")
        dw_glu = matmul(yg, dz, ta=True, out_dtype=BF16, name=f"mm_dwglu_{i}")
        dyg = s5_to_groups(matmul(dz, w_glu, tb=True, out_dtype=BF16, name=f"mm_dyg_{i}"))
        du_g, dprm = s5_bwd(u, dyg, prm, f"s5_bwd_{i}")
        du = s5_from_groups(du_g)
        dw_in = matmul(h, du, ta=True, out_dtype=BF16, name=f"mm_dwin_{i}")
        dh = matmul(du, w_in, tb=True, name=f"mm_dhin_{i}")
        dx, dxb, dgain = norm_bwd(xs_in, gain, dh, dx, f"norm_bwd_a{i}")
        a_g[i] = dict(norm=dgain, w_in=dw_in, w_glu=dw_glu, prm=s5_param_grads(
            (a_lambda_re[i], a_lambda_im[i], a_log_step[i], a_b_re[i], a_b_im[i], a_c_re[i], a_c_im[i],
             a_d[i]), dprm))
    grad_x = dx[None]

    for i in range(n_a):
        big_grads[('a_w_in', i)] = a_g[i]['w_in']
        big_grads[('a_w_glu', i)] = a_g[i]['w_glu']
    for j in range(n_b):
        big_grads[('b_w_q', j)] = b_g[j]['w_q']
        big_grads[('b_w_o', j)] = b_g[j]['w_o']
    for li in range(n_a + n_b):
        big_grads[('ffn_w_up', li)] = ffn_g[li]['w_up']
        big_grads[('ffn_w_down', li)] = ffn_g[li]['w_down']
    stack = lambda parts: jnp.stack(parts, axis=0)
    grads['a_norm'] = stack([a_g[i]['norm'].reshape(d) for i in range(n_a)])
    for pi, pname in enumerate(['a_lambda_re', 'a_lambda_im', 'a_log_step', 'a_b_re', 'a_b_im', 'a_c_re',
                                'a_c_im', 'a_d']):
        grads[pname] = stack([a_g[i]['prm'][pi] for i in range(n_a)])
    grads['w_f'] = dw_f
    grads['b_norm'] = stack([b_g[j]['norm'].reshape(d) for j in range(n_b)])
    grads['ffn_norm'] = stack([g_['norm'].reshape(d) for g_ in ffn_g])
    grads['ffn_conv_w'] = stack([g_['conv_w'] for g_ in ffn_g])
    grads['ffn_conv_b'] = stack([g_['conv_b'].reshape(-1) for g_ in ffn_g])

    small_names = [n for n in WEIGHT_NAMES if n not in BIG]
    small_shapes = [grads[n].shape for n in small_names]
    g_parts = all_gather(_pack([grads[n] for n in small_names])[None], 0, "ag_grads_small")
    g_sum = _unpack(sum8(g_parts, "sum_grads_small"), small_shapes)
    g_full = dict(zip(small_names, g_sum))
    g_local = {}
    for n in small_names:
        if n in SMALL_SHARDED:
            ax = SMALL_SHARDED[n]
            size = w[n].shape[ax]
            g_local[n] = lax.dynamic_slice_in_dim(g_full[n], me * size, size, axis=ax)
        else:
            g_local[n] = g_full[n]
    local_shapes = [w[n].shape for n in small_names]
    d_s, m_s, v_s = adamw_plain(_pack([g_local[n] for n in small_names]), _pack([w[n] for n in small_names]),
                                _pack([mom[n] for n in small_names]), _pack([var[n] for n in small_names]),
                                "adamw_small")
    out_g, out_d, out_m, out_v = dict(g_local), {}, {}, {}
    for n, dd, mm, vv in zip(small_names, _unpack(d_s, local_shapes), _unpack(m_s, local_shapes),
                             _unpack(v_s, local_shapes)):
        out_d[n], out_m[n], out_v[n] = dd, mm, vv

    for name, ax in BIG.items():
        layered = w[name].ndim == 3
        layers = range(w[name].shape[0]) if layered else [None]
        res = []
        for layer in layers:
            tag = name if layer is None else f"{name}{layer}"
            g_mat = big_grads[(name, layer)]
            theirs = sibling_exchange(g_mat, ax, f"rs_d2d_{tag}")
            parts = owner_exchange(chip_sum(own_blocks(g_mat, ax), theirs, f"rs_add_{tag}"), f"rs_ici_{tag}")
            pick = (lambda a: a) if layer is None else (lambda a: a[layer])
            res.append(adamw_sum(parts, pick(w[name]), pick(mom[name]), pick(var[name]), f"adamw_{tag}"))
        for dst, idx in ((out_g, 0), (out_d, 1), (out_m, 2), (out_v, 3)):
            dst[name] = stack([r[idx] for r in res]) if layered else res[0][idx]

    return (loss, grad_x, *[out_g[n] for n in WEIGHT_NAMES], *[out_d[n] for n in WEIGHT_NAMES],
            *[out_m[n] for n in WEIGHT_NAMES], *[out_v[n] for n in WEIGHT_NAMES])
```

```python
import functools
import math

import jax
import jax.numpy as jnp
from jax import lax
from jax.experimental import pallas as pl
from jax.experimental.pallas import tpu as pltpu

F32 = jnp.float32
BF16 = jnp.bfloat16
HI = lax.Precision.HIGHEST
MESH_AXES = ("x", "y", "c")
N_DEV = 8
VMEM_LIMIT_BYTES = 48 * 1024 * 1024

NORM_EPS = 1e-6
S5_GROUP = 16
S5_STATE = 64
S5_CHUNK = 16
S5_TC = S5_CHUNK * S5_GROUP
S5_2P = 2 * S5_STATE
HEAD_DIM = 128
MASK_VALUE = -1e30
GELU_C = math.sqrt(2.0 / math.pi)
GELU_A = 0.044715

ADAM_LR = 0.001
ADAM_B1 = 0.9
ADAM_B2 = 0.999
ADAM_EPS = 1e-08
ADAM_WD = 0.01
ADAM_STEP = 10


def _params(sem=None):
    return pltpu.CompilerParams(dimension_semantics=sem, vmem_limit_bytes=VMEM_LIMIT_BYTES)


def _pick(n, cands):
    for c in cands:
        if n % c == 0:
            return c
    return n


def _gelu(x):
    return 0.5 * x * (1.0 + jnp.tanh(GELU_C * (x + GELU_A * x * x * x)))


def _gelu_grad(x):
    t = jnp.tanh(GELU_C * (x + GELU_A * x * x * x))
    return 0.5 * (1.0 + t) + 0.5 * x * (1.0 - t * t) * GELU_C * (1.0 + 3.0 * GELU_A * x * x)


def _my_index():
    return 4 * lax.axis_index("x") + 2 * lax.axis_index("y") + lax.axis_index("c")


_COMM_SCRATCH = [pltpu.SemaphoreType.DMA((N_DEV - 1,)), pltpu.SemaphoreType.DMA((N_DEV - 1,)),
                 pltpu.SemaphoreType.DMA]
_HBM = pl.BlockSpec(memory_space=pltpu.HBM)


def all_gather(xs, axis, name):
    nd = xs.ndim
    n = xs.shape[axis]
    full = list(xs.shape)
    full[axis] = n * N_DEV

    def body(x_ref, out_ref, send_sems, recv_sems, local_sem):
        x, y, c = lax.axis_index("x"), lax.axis_index("y"), lax.axis_index("c")
        me, sibling = (x, y, c), (x, y, 1 - c)
        chips = [(1 - x, y), (x, 1 - y), (1 - x, 1 - y)]

        def rows(px, py, pc):
            sl = [slice(None)] * nd
            sl[axis] = pl.ds((4 * px + 2 * py + pc) * n, n)
            return out_ref.at[tuple(sl)]

        def copy(k, block, to, src=None):
            return pltpu.make_async_remote_copy(
                src_ref=rows(*block) if src is None else src, dst_ref=rows(*block),
                send_sem=send_sems.at[k], recv_sem=recv_sems.at[k], device_id=to,
                device_id_type=pl.DeviceIdType.MESH)

        mine = pltpu.make_async_copy(x_ref, rows(*me), local_sem)
        mine.start()
        first = [copy(0, me, sibling, src=x_ref)]
        first += [copy(1 + j, me, (*chip, c), src=x_ref) for j, chip in enumerate(chips)]
        for cp in first:
            cp.start()
        passed = [copy(4 + j, (*chip, c), sibling) for j, chip in enumerate(chips)]
        for j, chip in enumerate(chips):
            copy(1 + j, (*chip, c), me).wait_recv()
            passed[j].start()
        copy(0, sibling, me).wait_recv()
        for j, chip in enumerate(chips):
            copy(4 + j, (*chip, 1 - c), me).wait_recv()
        for cp in first + passed:
            cp.wait_send()
        mine.wait()

    return pl.pallas_call(
        body, name=name, out_shape=jax.ShapeDtypeStruct(tuple(full), xs.dtype),
        in_specs=[_HBM], out_specs=_HBM, scratch_shapes=_COMM_SCRATCH,
    )(xs)


N_CHIPS = 4


def sibling_exchange(g, axis, name):
    nd = g.ndim
    n = g.shape[axis] // N_DEV
    blk = list(g.shape)
    blk[axis] = n

    def body(g_ref, theirs_ref, send_sems, recv_sems):
        x, y, c = lax.axis_index("x"), lax.axis_index("y"), lax.axis_index("c")

        def block(idx):
            sl = [slice(None)] * nd
            sl[axis] = pl.ds(idx * n, n)
            return g_ref.at[tuple(sl)]

        sends = [pltpu.make_async_remote_copy(
            src_ref=block(2 * q + 1 - c), dst_ref=theirs_ref.at[q], send_sem=send_sems.at[q],
            recv_sem=recv_sems.at[q], device_id=(x, y, 1 - c), device_id_type=pl.DeviceIdType.MESH)
            for q in range(N_CHIPS)]
        for cp in sends:
            cp.start()
        for cp in sends:
            cp.wait_send()
            cp.wait_recv()

    return pl.pallas_call(
        body, name=name, out_shape=jax.ShapeDtypeStruct((N_CHIPS, *blk), g.dtype),
        in_specs=[_HBM], out_specs=_HBM, scratch_shapes=[pltpu.SemaphoreType.DMA((N_CHIPS,))] * 2,
    )(g)


def own_blocks(g, axis):
    c = lax.axis_index("c")
    n = g.shape[axis] // N_DEV
    return jnp.stack([lax.dynamic_slice_in_dim(g, (2 * q + c) * n, n, axis=axis) for q in range(N_CHIPS)])


def owner_exchange(part, name):
    def body(p_ref, out_ref, send_sems, recv_sems, local_sem):
        x, y, c = lax.axis_index("x"), lax.axis_index("y"), lax.axis_index("c")
        my_chip = 2 * x + y
        mine = pltpu.make_async_copy(p_ref.at[my_chip], out_ref.at[my_chip], local_sem)
        mine.start()

        def other(k):
            px = 1 - x if (k >> 1) & 1 else x
            py = 1 - y if k & 1 else y
            return (px, py, c), 2 * px + py

        def copy(k, src_slot, dst_slot):
            return pltpu.make_async_remote_copy(
                src_ref=p_ref.at[src_slot], dst_ref=out_ref.at[dst_slot], send_sem=send_sems.at[k - 1],
                recv_sem=recv_sems.at[k - 1], device_id=other(k)[0], device_id_type=pl.DeviceIdType.MESH)

        sends = [copy(k, other(k)[1], my_chip) for k in range(1, N_CHIPS)]
        for cp in sends:
            cp.start()
        for k in range(1, N_CHIPS):
            sends[k - 1].wait_send()
            copy(k, other(k)[1], other(k)[1]).wait_recv()
        mine.wait()

    return pl.pallas_call(
        body, name=name, out_shape=jax.ShapeDtypeStruct(part.shape, part.dtype),
        in_specs=[_HBM], out_specs=_HBM,
        scratch_shapes=[pltpu.SemaphoreType.DMA((N_CHIPS - 1,)), pltpu.SemaphoreType.DMA((N_CHIPS - 1,)),
                        pltpu.SemaphoreType.DMA],
    )(part)


def chip_sum(mine, theirs, name):
    shape = mine.shape
    cols = shape[-1]

    def fn(a, b):
        return (a.astype(F32) + b.astype(F32),), ()

    out = rowwise(fn, [mine.reshape(-1, cols), theirs.reshape(-1, cols)], [], [(cols, BF16)], [], tl=512,
                  name=name)[0]
    return out.reshape(shape)


def matmul(a, b, *, ta=False, tb=False, out_dtype=F32, add=None, name):
    m = a.shape[1] if ta else a.shape[0]
    kk = a.shape[0] if ta else a.shape[1]
    n = b.shape[0] if tb else b.shape[1]
    tm = _pick(m, (1024, 512, 256, 128))
    tn = _pick(n, (1024, 512, 256, 128))
    tk = _pick(kk, (2048, 1408, 1024, 512, 256, 128))
    nk = kk // tk
    dims = (((0,) if ta else (1,), (1,) if tb else (0,)), ((), ()))

    def body(*refs):
        a_ref, b_ref = refs[0], refs[1]
        add_ref = None if add is None else refs[2]
        o_ref = refs[2 if add is None else 3]

        def finish(r):
            if add is not None:
                r = r + add_ref[...].astype(F32)
            o_ref[...] = r.astype(out_dtype)

        part = lax.dot_general(a_ref[...], b_ref[...], dims, preferred_element_type=F32)
        if nk == 1:
            finish(part)
            return
        acc = refs[-1]
        k = pl.program_id(2)

        @pl.when(k == 0)
        def _():
            acc[...] = part

        @pl.when(k > 0)
        def _():
            acc[...] += part

        @pl.when(k == nk - 1)
        def _():
            finish(acc[...])

    a_spec = (pl.BlockSpec((tk, tm), lambda i, j, k: (k, i)) if ta
              else pl.BlockSpec((tm, tk), lambda i, j, k: (i, k)))
    b_spec = (pl.BlockSpec((tn, tk), lambda i, j, k: (j, k)) if tb
              else pl.BlockSpec((tk, tn), lambda i, j, k: (k, j)))
    o_spec = pl.BlockSpec((tm, tn), lambda i, j, k: (i, j))
    ins = [a, b] + ([] if add is None else [add])
    in_specs = [a_spec, b_spec] + ([] if add is None else [o_spec])
    return pl.pallas_call(
        body, name=name, out_shape=jax.ShapeDtypeStruct((m, n), out_dtype),
        grid=(m // tm, n // tn, nk), in_specs=in_specs, out_specs=o_spec,
        scratch_shapes=[] if nk == 1 else [pltpu.VMEM((tm, tn), F32)],
        compiler_params=_params(("parallel", "parallel", "arbitrary")),
    )(*ins)


def rowwise(fn, ins, rows, outs, accs, *, tl, name):
    n_rows = ins[0].shape[0]
    tl = _pick(n_rows, tuple(tl >> s for s in range(tl.bit_length() - 3)))
    n_in, n_row, n_out, n_acc = len(ins), len(rows), len(outs), len(accs)

    def body(*refs):
        in_refs = refs[:n_in]
        row_refs = refs[n_in:n_in + n_row]
        out_refs = refs[n_in + n_row:n_in + n_row + n_out]
        acc_refs = refs[n_in + n_row + n_out:]
        res_outs, res_accs = fn(*[r[...] for r in in_refs], *[r[...] for r in row_refs])
        for r, v in zip(out_refs, res_outs):
            r[...] = v.astype(r.dtype)
        if n_acc:
            @pl.when(pl.program_id(0) == 0)
            def _():
                for r in acc_refs:
                    r[...] = jnp.zeros_like(r)

            for r, v in zip(acc_refs, res_accs):
                r[...] += v

    in_specs = [pl.BlockSpec((tl, a.shape[1]), lambda i: (i, 0)) for a in ins]
    in_specs += [pl.BlockSpec((1, a.shape[1]), lambda i: (0, 0)) for a in rows]
    out_specs = [pl.BlockSpec((tl, w), lambda i: (i, 0)) for w, _ in outs]
    out_specs += [pl.BlockSpec((1, w), lambda i: (0, 0)) for w in accs]
    out_shape = [jax.ShapeDtypeStruct((n_rows, w), dt) for w, dt in outs]
    out_shape += [jax.ShapeDtypeStruct((1, w), F32) for w in accs]
    res = pl.pallas_call(
        body, name=name, out_shape=out_shape, grid=(n_rows // tl,), in_specs=in_specs,
        out_specs=out_specs, compiler_params=_params(("arbitrary",)),
    )(*ins, *rows)
    return res


def _rsum(v):
    return jnp.sum(v, axis=0, keepdims=True)


def norm_fwd(x, g, name):
    def fn(xt, gt):
        r = lax.rsqrt(jnp.mean(xt * xt, axis=-1, keepdims=True) + NORM_EPS)
        return (xt * r * gt,), ()

    d = x.shape[1]
    return rowwise(fn, [x], [g], [(d, BF16)], [], tl=512, name=name)[0]


def norm_bwd(x, g, dh, dres, name):
    def fn(xt, dht, drt, gt):
        r = lax.rsqrt(jnp.mean(xt * xt, axis=-1, keepdims=True) + NORM_EPS)
        w = dht * gt
        dx = drt + (r * w - xt * (r * r * r) * jnp.mean(xt * w, axis=-1, keepdims=True))
        return (dx, dx), (_rsum(dht * xt * r),)

    d = x.shape[1]
    dx, dxb, dg = rowwise(fn, [x, dh, dres], [g], [(d, F32), (d, BF16)], [d], tl=256, name=name)
    return dx, dxb, dg


def loss_and_grad(x, g, target, name):
    d = x.shape[1]

    def fn(xt, tt, gt):
        r = lax.rsqrt(jnp.mean(xt * xt, axis=-1, keepdims=True) + NORM_EPS)
        y = xt * r * gt
        diff = y - tt
        dy = diff * (1.0 / d)
        w = dy * gt
        dx = r * w - xt * (r * r * r) * jnp.mean(xt * w, axis=-1, keepdims=True)
        return (dx, dx), (_rsum(dy * xt * r), _rsum(diff * diff))

    dx, dxb, dg, sq = rowwise(fn, [x, target], [g], [(d, F32), (d, BF16)], [d, d], tl=256, name=name)
    return dx, dxb, dg, sq


def glu_fwd(z, x, name):
    d = x.shape[1]

    def fn(zt, xt):
        return (xt + zt[:, :d] * jax.nn.sigmoid(zt[:, d:]),), ()

    return rowwise(fn, [z, x], [], [(d, F32)], [], tl=256, name=name)[0]


def glu_bwd(z, dx, name):
    d = dx.shape[1]

    def fn(zt, dxt):
        val, gate = zt[:, :d], zt[:, d:]
        s = jax.nn.sigmoid(gate)
        return (jnp.concatenate([dxt * s, dxt * val * s * (1.0 - s)], axis=1),), ()

    return rowwise(fn, [z, dx], [], [(2 * d, BF16)], [], tl=256, name=name)[0]


def add_cast(a, b, name):
    def fn(at, bt):
        return (at + bt,), ()

    return rowwise(fn, [a, b], [], [(a.shape[1], BF16)], [], tl=512, name=name)[0]


def _conv_tiles(n_rows, f):
    return min(256, n_rows), _pick(f, (512, 256, 128))


def _shifted(gate, h6, h7):
    row = lax.broadcasted_iota(jnp.int32, gate.shape, 0)
    g1 = jnp.where(row == 0, h7, pltpu.roll(gate, 1, 0))
    g2 = jnp.where(row == 0, h6, jnp.where(row == 1, h7, pltpu.roll(gate, 2, 0)))
    return g1, g2


HALO_ROWS = 16


def _last_rows(halo, live):
    row = lax.broadcasted_iota(jnp.int32, halo.shape, 0)
    pick = lambda r: jnp.sum(jnp.where(row == r, halo, 0.0), axis=0, keepdims=True) * live
    return pick(HALO_ROWS - 2), pick(HALO_ROWS - 1)


def convglu_fwd(up, conv_w, conv_b, name):
    n_rows, f2 = up.shape
    f = f2 // 2
    tl, tc = _conv_tiles(n_rows, f)
    nc = f // tc
    hb = tl // HALO_ROWS

    def body(gate_ref, val_ref, halo_ref, w_ref, b_ref, o_ref):
        l = pl.program_id(1)
        h6, h7 = _last_rows(halo_ref[...].astype(F32), (l > 0).astype(F32))
        gate = gate_ref[...].astype(F32)
        g1, g2 = _shifted(gate, h6, h7)
        gc = w_ref[0:1, :] * g2 + w_ref[1:2, :] * g1 + w_ref[2:3, :] * gate + b_ref[...]
        o_ref[...] = (_gelu(gc) * val_ref[...].astype(F32)).astype(o_ref.dtype)

    return pl.pallas_call(
        body, name=name, out_shape=jax.ShapeDtypeStruct((n_rows, f), BF16),
        grid=(nc, n_rows // tl),
        in_specs=[pl.BlockSpec((tl, tc), lambda c, l: (l, c)),
                  pl.BlockSpec((tl, tc), lambda c, l: (l, c + nc)),
                  pl.BlockSpec((HALO_ROWS, tc), lambda c, l: (jnp.maximum(l * hb - 1, 0), c)),
                  pl.BlockSpec((3, tc), lambda c, l: (0, c)),
                  pl.BlockSpec((1, tc), lambda c, l: (0, c))],
        out_specs=pl.BlockSpec((tl, tc), lambda c, l: (l, c)),
        compiler_params=_params(("parallel", "arbitrary")),
    )(up, up, up, conv_w, conv_b)


def convglu_bwd(up, da, conv_w, conv_b, name):
    n_rows, f2 = up.shape
    f = f2 // 2
    tl = min(128, n_rows)
    tc = _pick(f, (512, 256, 128))
    nl = n_rows // tl
    hb = tl // 8
    last_halo = n_rows // 8 - 1
    hb16 = tl // HALO_ROWS
    last_halo16 = n_rows // HALO_ROWS - 1
    ext = tl + 8

    def body(gate_ref, val_ref, da_ref, gprev_ref, gnext_ref, vnext_ref, danext_ref, w_ref, b_ref,
             dup_ref, dw_ref, db_ref):
        l = pl.program_id(0)
        live_prev = (l > 0).astype(F32)
        live_next = (l < nl - 1).astype(F32)

        @pl.when(l == 0)
        def _():
            dw_ref[...] = jnp.zeros_like(dw_ref)
            db_ref[...] = jnp.zeros_like(db_ref)

        for c0 in range(0, f, tc):
            cols = slice(c0, c0 + tc)
            h6, h7 = _last_rows(gprev_ref[:, cols].astype(F32), live_prev)
            w0, w1, w2 = w_ref[0:1, cols], w_ref[1:2, cols], w_ref[2:3, cols]
            gate = jnp.concatenate([gate_ref[:, cols].astype(F32), gnext_ref[:, cols].astype(F32)[:8]], axis=0)
            val = jnp.concatenate([val_ref[:, cols].astype(F32), vnext_ref[:, cols].astype(F32)[:8]], axis=0)
            dae = jnp.concatenate([da_ref[:, cols].astype(F32),
                                   danext_ref[:, cols].astype(F32) * live_next], axis=0)
            g1, g2 = _shifted(gate, h6, h7)
            gc = w0 * g2 + w1 * g1 + w2 * gate + b_ref[:, cols]
            t = jnp.tanh(GELU_C * (gc + GELU_A * gc * gc * gc))
            gelu = 0.5 * gc * (1.0 + t)
            gelu_grad = 0.5 * (1.0 + t) + 0.5 * gc * (1.0 - t * t) * GELU_C * (1.0 + 3.0 * GELU_A * gc * gc)
            dgc = dae * val * gelu_grad
            dgate = w2 * dgc + w1 * pltpu.roll(dgc, ext - 1, 0) + w0 * pltpu.roll(dgc, ext - 2, 0)
            dup_ref[:, cols] = dgate[:tl].astype(dup_ref.dtype)
            dup_ref[:, c0 + f:c0 + f + tc] = (dae * gelu)[:tl].astype(dup_ref.dtype)
            dg = dgc[:tl]
            db_ref[:, cols] += _rsum(dg)
            dw_ref[0:1, cols] += _rsum(dg * g2[:tl])
            dw_ref[1:2, cols] += _rsum(dg * g1[:tl])
            dw_ref[2:3, cols] += _rsum(dg * gate[:tl])

    def prev_blk(l):
        return jnp.maximum(l * hb16 - 1, 0)

    def next_blk(l):
        return jnp.minimum((l + 1) * hb, last_halo)

    def next_blk16(l):
        return jnp.minimum((l + 1) * hb16, last_halo16)

    dup, dw, db = pl.pallas_call(
        body, name=name,
        out_shape=[jax.ShapeDtypeStruct((n_rows, f2), BF16), jax.ShapeDtypeStruct((3, f), F32),
                   jax.ShapeDtypeStruct((1, f), F32)],
        grid=(nl,),
        in_specs=[pl.BlockSpec((tl, f), lambda l: (l, 0)),
                  pl.BlockSpec((tl, f), lambda l: (l, 1)),
                  pl.BlockSpec((tl, f), lambda l: (l, 0)),
                  pl.BlockSpec((HALO_ROWS, f), lambda l: (prev_blk(l), 0)),
                  pl.BlockSpec((HALO_ROWS, f), lambda l: (next_blk16(l), 0)),
                  pl.BlockSpec((HALO_ROWS, f), lambda l: (next_blk16(l), 1)),
                  pl.BlockSpec((8, f), lambda l: (next_blk(l), 0)),
                  pl.BlockSpec((3, f), lambda l: (0, 0)),
                  pl.BlockSpec((1, f), lambda l: (0, 0))],
        out_specs=[pl.BlockSpec((tl, f2), lambda l: (l, 0)),
                   pl.BlockSpec((3, f), lambda l: (0, 0)),
                   pl.BlockSpec((1, f), lambda l: (0, 0))],
        compiler_params=_params(("arbitrary",)),
    )(up, up, da, up, up, up, da, conv_w, conv_b)
    return dup, dw, db


def _log_sigmoid(z):
    return jnp.minimum(z, 0.0) - jnp.log(1.0 + jnp.exp(-jnp.abs(z)))


def fgate_fwd(zf, bf, n_heads, name):
    n_rows = zf.shape[0]
    tl = min(512, n_rows)

    def body(z_ref, b_ref, c_ref, carry):
        @pl.when(pl.program_id(0) == 0)
        def _():
            carry[...] = jnp.zeros_like(carry)

        lf = _log_sigmoid(z_ref[...] + b_ref[...])
        lft = lf.T
        r = lax.broadcasted_iota(jnp.int32, (tl, tl), 0)
        c = lax.broadcasted_iota(jnp.int32, (tl, tl), 1)
        tri = (r <= c).astype(F32)
        cum = jnp.dot(lft, tri, precision=HI, preferred_element_type=F32) + carry[...]
        c_ref[...] = cum[:n_heads]
        carry[...] += jnp.sum(lft, axis=1, keepdims=True)

    return pl.pallas_call(
        body, name=name, out_shape=jax.ShapeDtypeStruct((n_heads, n_rows), F32),
        grid=(n_rows // tl,),
        in_specs=[pl.BlockSpec((tl, 128), lambda i: (i, 0)), pl.BlockSpec((1, 128), lambda i: (0, 0))],
        out_specs=pl.BlockSpec((n_heads, tl), lambda i: (0, i)),
        scratch_shapes=[pltpu.VMEM((128, 1), F32)],
        compiler_params=_params(("arbitrary",)),
    )(zf, bf)


def fgate_bwd(dc_a, dc_b, zf, bf, name):
    n_heads, n_rows = dc_a.shape
    tl = min(512, n_rows)
    nl = n_rows // tl

    def body(da_ref, db_ref, z_ref, b_ref, dz_ref, dbias_ref, carry):
        @pl.when(pl.program_id(0) == 0)
        def _():
            carry[...] = jnp.zeros_like(carry)
            dbias_ref[...] = jnp.zeros_like(dbias_ref)

        dc = da_ref[...] + db_ref[...]
        dcp = jnp.concatenate([dc, jnp.zeros((128 - n_heads, tl), F32)], axis=0)
        r = lax.broadcasted_iota(jnp.int32, (tl, tl), 0)
        c = lax.broadcasted_iota(jnp.int32, (tl, tl), 1)
        later = (c >= r).astype(F32)
        local = lax.dot_general(later, dcp, (((1,), (1,)), ((), ())), precision=HI,
                                preferred_element_type=F32)
        dlf = local + carry[...]
        carry[...] += local[0:1, :]
        z = z_ref[...] + b_ref[...]
        dz = dlf * jax.nn.sigmoid(-z)
        dz_ref[...] = dz.astype(dz_ref.dtype)
        dbias_ref[...] += _rsum(dz)

    return pl.pallas_call(
        body, name=name,
        out_shape=[jax.ShapeDtypeStruct((n_rows, 128), BF16), jax.ShapeDtypeStruct((1, 128), F32)],
        grid=(nl,),
        in_specs=[pl.BlockSpec((n_heads, tl), lambda i: (0, nl - 1 - i)),
                  pl.BlockSpec((n_heads, tl), lambda i: (0, nl - 1 - i)),
                  pl.BlockSpec((tl, 128), lambda i: (nl - 1 - i, 0)),
                  pl.BlockSpec((1, 128), lambda i: (0, 0))],
        out_specs=[pl.BlockSpec((tl, 128), lambda i: (nl - 1 - i, 0)),
                   pl.BlockSpec((1, 128), lambda i: (0, 0))],
        scratch_shapes=[pltpu.VMEM((1, 128), F32)],
        compiler_params=_params(("arbitrary",)),
    )(dc_a, dc_b, zf, bf)


def _pick_row(blk, h):
    row = lax.broadcasted_iota(jnp.int32, blk.shape, 0)
    return jnp.sum(jnp.where(row == h, blk, 0.0), axis=0, keepdims=True)


def _pick_first(blk, h):
    r = lax.broadcasted_iota(jnp.int32, blk.shape, 0)
    c = lax.broadcasted_iota(jnp.int32, blk.shape, 1)
    return jnp.sum(jnp.where((r == h) & (c == 0), blk, 0.0), keepdims=True)


def _causal(s, keys_on_rows=False):
    r = lax.broadcasted_iota(jnp.int32, s.shape, 0)
    c = lax.broadcasted_iota(jnp.int32, s.shape, 1)
    return jnp.where((r <= c) if keys_on_rows else (c <= r), s, MASK_VALUE)


def _attn_tile(n_rows):
    return min(512, n_rows)


def attn_layouts(a):
    n_heads, n_rows = a.shape
    tq = _attn_tile(n_rows)
    return a.reshape(n_heads, n_rows // tq, tq).transpose(1, 0, 2), a.reshape(n_heads, n_rows, 1)


def flash_fwd(q, k, v, c_rows, name):
    n_rows, d = q.shape
    n_heads = d // HEAD_DIM
    tq = _attn_tile(n_rows)
    nq = n_rows // tq
    scale = HEAD_DIM ** -0.5

    def body(q_ref, k_ref, v_ref, c_ref, o_ref, o32_ref, lse_ref):
        h, i = pl.program_id(0), pl.program_id(1)
        qv = q_ref[...]
        c_q = _pick_first(c_ref[i], h)

        def block(j, carry, diagonal):
            m_prev, l_prev, acc = carry
            rows = pl.ds(pl.multiple_of(j * tq, tq), tq)
            s = lax.dot_general(qv, k_ref[rows, :], (((1,), (1,)), ((), ())),
                                preferred_element_type=F32) * scale
            s = s + (c_q - _pick_row(c_ref[j], h))
            if diagonal:
                s = _causal(s)
            m_new = jnp.maximum(m_prev, jnp.max(s, axis=1, keepdims=True))
            p = jnp.exp(s - m_new)
            alpha = jnp.exp(m_prev - m_new)
            p_hi = p.astype(BF16)
            p_lo = (p - p_hi.astype(F32)).astype(BF16)
            vv = v_ref[rows, :]
            pv = (jnp.dot(p_hi, vv, preferred_element_type=F32)
                  + jnp.dot(p_lo, vv, preferred_element_type=F32))
            return m_new, alpha * l_prev + jnp.sum(p, axis=1, keepdims=True), alpha * acc + pv

        init = (jnp.full((tq, 1), MASK_VALUE, F32), jnp.zeros((tq, 1), F32), jnp.zeros((tq, HEAD_DIM), F32))
        carry = lax.fori_loop(0, i, lambda j, c: block(j, c, False), init)
        m_fin, l_fin, acc = block(i, carry, True)
        o = acc / l_fin
        o_ref[...] = o.astype(o_ref.dtype)
        o32_ref[...] = o
        lse_ref[0] = m_fin + jnp.log(l_fin)

    qspec = pl.BlockSpec((tq, HEAD_DIM), lambda h, i: (i, h))
    kspec = pl.BlockSpec((n_rows, HEAD_DIM), lambda h, i: (0, h))
    return pl.pallas_call(
        body, name=name,
        out_shape=[jax.ShapeDtypeStruct((n_rows, d), BF16), jax.ShapeDtypeStruct((n_rows, d), F32),
                   jax.ShapeDtypeStruct((n_heads, n_rows, 1), F32)],
        grid=(n_heads, nq),
        in_specs=[qspec, kspec, kspec, pl.BlockSpec((nq, n_heads, tq), lambda h, i: (0, 0, 0))],
        out_specs=[qspec, qspec, pl.BlockSpec((1, tq, 1), lambda h, i: (h, i, 0))],
        compiler_params=_params(("parallel", "arbitrary")),
    )(q, k, v, c_rows)


def attn_delta(do, o, name):
    n_rows, d = o.shape
    n_heads = d // HEAD_DIM
    tq = min(512, n_rows)

    def body(do_ref, o_ref, out_ref):
        out_ref[0] = jnp.sum(do_ref[...].astype(F32) * o_ref[...].astype(F32), axis=1, keepdims=True)

    spec = pl.BlockSpec((tq, HEAD_DIM), lambda i, h: (i, h))
    return pl.pallas_call(
        body, name=name, out_shape=jax.ShapeDtypeStruct((n_heads, n_rows, 1), F32),
        grid=(n_rows // tq, n_heads), in_specs=[spec, spec],
        out_specs=pl.BlockSpec((1, tq, 1), lambda i, h: (h, i, 0)),
        compiler_params=_params(("parallel", "parallel")),
    )(do, o)


def flash_bwd_dq(q, k, v, do, c_rows, lse, delta, name):
    n_rows, d = q.shape
    n_heads = d // HEAD_DIM
    tq = _attn_tile(n_rows)
    nq = n_rows // tq
    scale = HEAD_DIM ** -0.5

    def body(q_ref, k_ref, v_ref, do_ref, c_ref, lse_ref, dl_ref, dq_ref):
        h, i = pl.program_id(0), pl.program_id(1)
        qv, dov = q_ref[...], do_ref[...]
        lse_q, dl_q = lse_ref[0], dl_ref[0]
        c_q = _pick_first(c_ref[i], h)

        def block(j, acc, diagonal):
            rows = pl.ds(pl.multiple_of(j * tq, tq), tq)
            kk = k_ref[rows, :]
            s = lax.dot_general(qv, kk, (((1,), (1,)), ((), ())), preferred_element_type=F32) * scale
            s = s + (c_q - _pick_row(c_ref[j], h))
            if diagonal:
                s = _causal(s)
            p = jnp.exp(s - lse_q)
            dp = lax.dot_general(dov, v_ref[rows, :], (((1,), (1,)), ((), ())), preferred_element_type=F32)
            ds = p * (dp - dl_q)
            return acc + jnp.dot(ds.astype(BF16), kk, preferred_element_type=F32)

        acc = lax.fori_loop(0, i, lambda j, a: block(j, a, False), jnp.zeros((tq, HEAD_DIM), F32))
        acc = block(i, acc, True)
        dq_ref[...] = (acc * scale).astype(dq_ref.dtype)

    qspec = pl.BlockSpec((tq, HEAD_DIM), lambda h, i: (i, h))
    kspec = pl.BlockSpec((n_rows, HEAD_DIM), lambda h, i: (0, h))
    vec = pl.BlockSpec((1, tq, 1), lambda h, i: (h, i, 0))
    return pl.pallas_call(
        body, name=name, out_shape=jax.ShapeDtypeStruct((n_rows, d), BF16),
        grid=(n_heads, nq),
        in_specs=[qspec, kspec, kspec, qspec, pl.BlockSpec((nq, n_heads, tq), lambda h, i: (0, 0, 0)), vec, vec],
        out_specs=qspec,
        compiler_params=_params(("parallel", "arbitrary")),
    )(q, k, v, do, c_rows, lse, delta)


def flash_bwd_dkv(q, k, v, do, c_rows, c_cols, lse_rows, delta_rows, name):
    n_rows, d = q.shape
    n_heads = d // HEAD_DIM
    tq = _attn_tile(n_rows)
    nq = n_rows // tq
    scale = HEAD_DIM ** -0.5

    def body(q_ref, k_ref, v_ref, do_ref, c_ref, ck_ref, lse_ref, dl_ref, dk_ref, dv_ref, dc_ref):
        h, j = pl.program_id(0), pl.program_id(1)
        kk, vv = k_ref[...], v_ref[...]
        c_k = ck_ref[0]

        def block(i, carry, diagonal):
            dk, dv, dc = carry
            rows = pl.ds(pl.multiple_of(i * tq, tq), tq)
            qi, doi = q_ref[rows, :], do_ref[rows, :]
            st = lax.dot_general(kk, qi, (((1,), (1,)), ((), ())), preferred_element_type=F32) * scale
            st = st + (_pick_first(c_ref[i], h) - c_k)
            if diagonal:
                st = _causal(st, keys_on_rows=True)
            pt = jnp.exp(st - _pick_row(lse_ref[i], h))
            dpt = lax.dot_general(vv, doi, (((1,), (1,)), ((), ())), preferred_element_type=F32)
            dst = pt * (dpt - _pick_row(dl_ref[i], h))
            dv = dv + jnp.dot(pt.astype(BF16), doi, preferred_element_type=F32)
            dk = dk + jnp.dot(dst.astype(BF16), qi, preferred_element_type=F32)
            return dk, dv, dc - jnp.sum(dst, axis=1, keepdims=True)

        zero = jnp.zeros((tq, HEAD_DIM), F32)
        carry = block(j, (zero, zero, jnp.zeros((tq, 1), F32)), True)
        dk, dv, dc = lax.fori_loop(j + 1, nq, lambda i, c: block(i, c, False), carry)
        dk_ref[...] = dk * scale
        dv_ref[...] = dv
        dc_ref[0] = dc

    kspec = pl.BlockSpec((tq, HEAD_DIM), lambda h, j: (j, h))
    qspec = pl.BlockSpec((n_rows, HEAD_DIM), lambda h, j: (0, h))
    rows3 = pl.BlockSpec((nq, n_heads, tq), lambda h, j: (0, 0, 0))
    col = pl.BlockSpec((1, tq, 1), lambda h, j: (h, j, 0))
    return pl.pallas_call(
        body, name=name,
        out_shape=[jax.ShapeDtypeStruct((n_rows, d), F32), jax.ShapeDtypeStruct((n_rows, d), F32),
                   jax.ShapeDtypeStruct((n_heads, n_rows, 1), F32)],
        grid=(n_heads, nq),
        in_specs=[qspec, kspec, kspec, qspec, rows3, col, rows3, rows3],
        out_specs=[kspec, kspec, col],
        compiler_params=_params(("parallel", "arbitrary")),
    )(q, k, v, do, c_rows, c_cols, lse_rows, delta_rows)


S5_PARAM_SHAPES = [(1, S5_2P)] * 3 + [(S5_2P, 1)] * 3 + [(S5_GROUP, S5_2P)] * 4 + [(S5_2P, S5_TC)] * 4 + [(S5_TC, 1)]


def _tile_rows(a, reps):
    return jnp.concatenate([a] * reps, axis=0)


def _s5_build(lrl, lil, lsl, lrc, lic, lsc, cri, cir, bt2, bt2s, brit, birt, ctt, ctts):
    first = lax.broadcasted_iota(jnp.int32, (1, S5_2P), 1) < S5_STATE
    sgn = jnp.where(first, -1.0, 1.0)
    dt_l = jnp.exp(lsl)
    a_l, b_l = lrl * dt_l, lil * dt_l
    mag1 = jnp.exp(a_l)
    lbr, lbi = mag1 * jnp.cos(b_l), mag1 * jnp.sin(b_l)
    n = lrl * lrl + lil * lil
    kr = ((lbr - 1.0) * lrl + lbi * lil) / n
    ki = (lbi * lrl - (lbr - 1.0) * lil) / n
    blk = lax.shift_right_logical(lax.broadcasted_iota(jnp.int32, (S5_TC, 1), 0), 4).astype(F32)
    e_end = (S5_CHUNK - 1.0) - blk
    mag_e = jnp.exp(e_end * a_l)
    pr, pi = mag_e * jnp.cos(e_end * b_l), mag_e * jnp.sin(e_end * b_l)
    cr, ci = pr * kr - pi * ki, pr * ki + pi * kr
    w_end = cr * _tile_rows(bt2, S5_CHUNK) + (sgn * ci) * _tile_rows(bt2s, S5_CHUNK)
    mag_j = jnp.exp(blk * a_l)
    qr, qi = mag_j * jnp.cos(blk * b_l), mag_j * jnp.sin(blk * b_l)
    xmat = qr * _tile_rows(cri, S5_CHUNK) + (sgn * qi) * _tile_rows(cir, S5_CHUNK)
    dt_c = jnp.exp(lsc)
    a_c, b_c = lrc * dt_c, lic * dt_c
    m1c = jnp.exp(a_c)
    lbrc, lbic = m1c * jnp.cos(b_c), m1c * jnp.sin(b_c)
    n_c = lrc * lrc + lic * lic
    krc = ((lbrc - 1.0) * lrc + lbic * lic) / n_c
    kic = (lbic * lrc - (lbrc - 1.0) * lic) / n_c
    top = lax.broadcasted_iota(jnp.int32, (S5_2P, 1), 0) < S5_STATE
    rhs = jnp.where(top, krc, -krc) * brit + (-kic) * birt
    z = jnp.dot(xmat, rhs, precision=HI, preferred_element_type=F32)
    e_out = (lax.shift_right_logical(lax.broadcasted_iota(jnp.int32, (1, S5_TC), 1), 4) + 1).astype(F32)
    mag_o = jnp.exp(a_c * e_out)
    o_r, o_i = mag_o * jnp.cos(b_c * e_out), mag_o * jnp.sin(b_c * e_out)
    w_out = jnp.where(top, o_r, -o_r) * ctt + (-o_i) * ctts
    m16 = jnp.exp(float(S5_CHUNK) * a_l)
    a1 = m16 * jnp.cos(float(S5_CHUNK) * b_l)
    a2 = sgn * (m16 * jnp.sin(float(S5_CHUNK) * b_l))
    return z, w_end, w_out, a1, a2


def _lane_block_bit(b):
    blk = lax.shift_right_logical(lax.broadcasted_iota(jnp.int32, (S5_TC, S5_TC), 1), 4)
    return (lax.shift_right_logical(blk, b) & 1) == 1


def _toeplitz(z, dcol):
    for b in range(4):
        s = S5_GROUP << b
        moved = jnp.concatenate([jnp.zeros((s, S5_TC), F32), z[:S5_TC - s]], axis=0)
        z = jnp.where(_lane_block_bit(b), moved, z)
    r = lax.broadcasted_iota(jnp.int32, (S5_TC, S5_TC), 0)
    c = lax.broadcasted_iota(jnp.int32, (S5_TC, S5_TC), 1)
    return z + jnp.where(r == c, dcol, 0.0)


def _toeplitz_t(dt):
    r = lax.broadcasted_iota(jnp.int32, (S5_TC, S5_TC), 0)
    c = lax.broadcasted_iota(jnp.int32, (S5_TC, S5_TC), 1)
    ddcol = jnp.sum(jnp.where(r == c, dt, 0.0), axis=1, keepdims=True)
    for b in range(4):
        s = S5_GROUP << b
        moved = jnp.concatenate([dt[s:], jnp.zeros((s, S5_TC), F32)], axis=0)
        dt = jnp.where(_lane_block_bit(b), moved, dt)
    return dt, ddcol


def _s5_powers(a1, a2, n):
    first = lax.broadcasted_iota(jnp.int32, a1.shape, 1) < S5_STATE
    ar, ai = a1, jnp.where(first, -a2, a2)
    out = []
    d = 1
    while d < n:
        out.append((ar, jnp.where(first, -ai, ai)))
        ar, ai = ar * ar - ai * ai, 2.0 * ar * ai
        d *= 2
    return out


def _s5_prefix(e, pows, reverse):
    n = e.shape[0]
    row = lax.broadcasted_iota(jnp.int32, e.shape, 0)
    x = e
    for b, (p1, p2) in enumerate(pows):
        d = 1 << b
        if reverse:
            sh = jnp.where(row < n - d, pltpu.roll(x, n - d, 0), 0.0)
            x = x + p1 * sh - p2 * pltpu.roll(sh, S5_STATE, 1)
        else:
            sh = jnp.where(row >= d, pltpu.roll(x, d, 0), 0.0)
            x = x + p1 * sh + p2 * pltpu.roll(sh, S5_STATE, 1)
    return x


def _s5_scan(e, pows):
    n = e.shape[0]
    row = lax.broadcasted_iota(jnp.int32, e.shape, 0)
    return jnp.where(row >= 1, pltpu.roll(_s5_prefix(e, pows, False), 1, 0), 0.0)


def _s5_scan_t(dsin, pows):
    n = dsin.shape[0]
    row = lax.broadcasted_iota(jnp.int32, dsin.shape, 0)
    return jnp.where(row < n - 1, pltpu.roll(_s5_prefix(dsin, pows, True), n - 1, 0), 0.0)


def _dot16(a, b, dims):
    return lax.dot_general(a.astype(BF16), b.astype(BF16), (dims, ((), ())), preferred_element_type=F32)


_NN = ((1,), (0,))
_NT = ((1,), (1,))
_TN = ((0,), (0,))


def _s5_specs(gb):
    return [pl.BlockSpec((gb, *s), lambda i: (i, 0, 0)) for s in S5_PARAM_SHAPES]


def s5_fwd(u, prm, name):
    n_groups, nk, _ = u.shape
    gb = min(4, n_groups)
    n_prm = len(S5_PARAM_SHAPES)

    def body(*refs):
        u_ref = refs[0]
        p_refs = refs[1:1 + n_prm]
        o_ref = refs[1 + n_prm]
        sin_sc, toep_sc = refs[2 + n_prm:]
        for g in range(gb):
            z, w_end, w_out, a1, a2 = _s5_build(*[r[g] for r in p_refs[:-1]])
            toep_sc[...] = _toeplitz(z, p_refs[-1][g])
            ug = u_ref[g].astype(F32)
            sin_sc[...] = _s5_scan(_dot16(ug, w_end, _NN), _s5_powers(a1, a2, nk))
            y = _dot16(ug, toep_sc[...], _NT) + _dot16(sin_sc[...], w_out, _NN)
            o_ref[g] = _gelu(y).astype(o_ref.dtype)

    blk = pl.BlockSpec((gb, nk, S5_TC), lambda i: (i, 0, 0))
    return pl.pallas_call(
        body, name=name, out_shape=jax.ShapeDtypeStruct(u.shape, BF16), grid=(n_groups // gb,),
        in_specs=[blk] + _s5_specs(gb), out_specs=blk,
        scratch_shapes=[pltpu.VMEM((nk, S5_2P), F32), pltpu.VMEM((S5_TC, S5_TC), F32)],
        compiler_params=_params(("parallel",)),
    )(u, *prm)


def s5_bwd(u, dyg, prm, name):
    n_groups, nk, _ = u.shape
    gb = min(4, n_groups)
    n_prm = len(S5_PARAM_SHAPES)

    def body(*refs):
        u_ref, dyg_ref = refs[0], refs[1]
        p_refs = refs[2:2 + n_prm]
        du_ref = refs[2 + n_prm]
        dp_refs = refs[3 + n_prm:3 + 2 * n_prm]
        sin_sc, de_sc, dy_sc, toep_sc = refs[3 + 2 * n_prm:]
        for g in range(gb):
            outs, vjp = jax.vjp(_s5_build, *[r[g] for r in p_refs[:-1]])
            z, w_end, w_out, a1, a2 = outs
            toep_sc[...] = _toeplitz(z, p_refs[-1][g])
            pows = _s5_powers(a1, a2, nk)
            ug = u_ref[g].astype(F32)
            sin_sc[...] = _s5_scan(_dot16(ug, w_end, _NN), pows)
            y = _dot16(ug, toep_sc[...], _NT) + _dot16(sin_sc[...], w_out, _NN)
            dy_sc[...] = dyg_ref[g].astype(F32) * _gelu_grad(y)
            dy = dy_sc[...]
            de_sc[...] = _s5_scan_t(_dot16(dy, w_out, _NT), pows)
            de, sin = de_sc[...], sin_sc[...]
            du_ref[g] = (_dot16(dy, toep_sc[...], _NN) + _dot16(de, w_end, _NT)).astype(du_ref.dtype)
            dz, ddcol = _toeplitz_t(_dot16(dy, ug, _TN))
            da1 = jnp.sum(de * sin, axis=0, keepdims=True)
            da2 = jnp.sum(de * pltpu.roll(sin, S5_STATE, 1), axis=0, keepdims=True)
            grads = vjp((dz, _dot16(ug, de, _TN), _dot16(sin, dy, _TN), da1, da2))
            for r, v in zip(dp_refs[:-1], grads):
                r[g] = v
            dp_refs[-1][g] = ddcol

    blk = pl.BlockSpec((gb, nk, S5_TC), lambda i: (i, 0, 0))
    res = pl.pallas_call(
        body, name=name,
        out_shape=[jax.ShapeDtypeStruct(u.shape, BF16)]
        + [jax.ShapeDtypeStruct((n_groups, *s), F32) for s in S5_PARAM_SHAPES],
        grid=(n_groups // gb,),
        in_specs=[blk, blk] + _s5_specs(gb), out_specs=[blk] + _s5_specs(gb),
        scratch_shapes=[pltpu.VMEM((nk, S5_2P), F32), pltpu.VMEM((nk, S5_2P), F32),
                        pltpu.VMEM((nk, S5_TC), F32), pltpu.VMEM((S5_TC, S5_TC), F32)],
        compiler_params=_params(("parallel",)),
    )(u, dyg, *prm)
    return res[0], res[1:]


def s5_prepare(lam_re, lam_im, log_step, b_re, b_im, c_re, c_im, d):
    g = lam_re.shape[0]
    lane = lambda a: a.reshape(g, 1, -1)
    col = lambda a: a.reshape(g, -1, 1)
    ls = jnp.broadcast_to(log_step[:, None], (g, S5_2P))
    brt = jnp.swapaxes(b_re, 1, 2)
    bit = jnp.swapaxes(b_im, 1, 2)
    crt = jnp.swapaxes(c_re, 1, 2)
    cit = jnp.swapaxes(c_im, 1, 2)
    cat = jnp.concatenate
    return [
        lane(cat([lam_re, lam_re], 1)), lane(cat([lam_im, lam_im], 1)), lane(ls),
        col(cat([lam_re, lam_re], 1)), col(cat([lam_im, lam_im], 1)), col(ls),
        cat([c_re, c_im], 2), cat([c_im, c_re], 2),
        cat([brt, bit], 2), cat([bit, brt], 2),
        jnp.tile(cat([b_re, b_im], 1), (1, 1, S5_CHUNK)), jnp.tile(cat([b_im, b_re], 1), (1, 1, S5_CHUNK)),
        jnp.tile(cat([crt, cit], 1), (1, 1, S5_CHUNK)), jnp.tile(cat([cit, crt], 1), (1, 1, S5_CHUNK)),
        col(jnp.tile(d, (1, S5_CHUNK))),
    ]


def s5_param_grads(prm, grads):
    _, vjp = jax.vjp(s5_prepare, *prm)
    return vjp(list(grads))


def s5_to_groups(a):
    n_rows, d = a.shape
    g, nk = d // S5_GROUP, n_rows // S5_CHUNK
    return a.reshape(nk, S5_CHUNK, g, S5_GROUP).transpose(2, 0, 1, 3).reshape(g, nk, S5_TC)


def s5_from_groups(a):
    g, nk, _ = a.shape
    return a.reshape(g, nk, S5_CHUNK, S5_GROUP).transpose(1, 2, 0, 3).reshape(nk * S5_CHUNK, g * S5_GROUP)


def _adamw_math(w, g, m, v):
    m = ADAM_B1 * m + (1.0 - ADAM_B1) * g
    v = ADAM_B2 * v + (1.0 - ADAM_B2) * (g * g)
    m_hat = m / (1.0 - ADAM_B1 ** ADAM_STEP)
    v_hat = v / (1.0 - ADAM_B2 ** ADAM_STEP)
    delta = -ADAM_LR * (m_hat / (jnp.sqrt(v_hat) + ADAM_EPS) + ADAM_WD * w)
    return delta, m, v


def adamw_sum(parts, w, m, v, name):
    r, c = w.shape
    n_parts = parts.shape[0]
    tr = _pick(r, (128, 64, 32, 16, 8))

    def body(p_ref, w_ref, m_ref, v_ref, g_out, d_out, m_out, v_out):
        g = p_ref[0].astype(F32)
        for s in range(1, n_parts):
            g = g + p_ref[s].astype(F32)
        delta, m_new, v_new = _adamw_math(w_ref[...], g, m_ref[...], v_ref[...])
        g_out[...] = g
        d_out[...] = delta
        m_out[...] = m_new
        v_out[...] = v_new

    spec = pl.BlockSpec((tr, c), lambda i: (i, 0))
    return pl.pallas_call(
        body, name=name, out_shape=[jax.ShapeDtypeStruct((r, c), F32)] * 4, grid=(r // tr,),
        in_specs=[pl.BlockSpec((n_parts, tr, c), lambda i: (0, i, 0)), spec, spec, spec],
        out_specs=[spec] * 4, compiler_params=_params(("parallel",)),
    )(parts, w, m, v)


def sum8(parts, name):
    r, c = parts.shape[1:]
    tr = _pick(r, (256, 128, 64, 32, 16, 8))

    def body(p_ref, o_ref):
        g = p_ref[0]
        for s in range(1, N_DEV):
            g = g + p_ref[s]
        o_ref[...] = g

    return pl.pallas_call(
        body, name=name, out_shape=jax.ShapeDtypeStruct((r, c), F32), grid=(r // tr,),
        in_specs=[pl.BlockSpec((N_DEV, tr, c), lambda i: (0, i, 0))],
        out_specs=pl.BlockSpec((tr, c), lambda i: (i, 0)), compiler_params=_params(("parallel",)),
    )(parts)


def adamw_plain(g, w, m, v, name):
    r, c = w.shape
    tr = _pick(r, (256, 128, 64, 32, 16, 8))

    def body(g_ref, w_ref, m_ref, v_ref, d_out, m_out, v_out):
        delta, m_new, v_new = _adamw_math(w_ref[...], g_ref[...], m_ref[...], v_ref[...])
        d_out[...] = delta
        m_out[...] = m_new
        v_out[...] = v_new

    spec = pl.BlockSpec((tr, c), lambda i: (i, 0))
    return pl.pallas_call(
        body, name=name, out_shape=[jax.ShapeDtypeStruct((r, c), F32)] * 3, grid=(r // tr,),
        in_specs=[spec] * 4, out_specs=[spec] * 3, compiler_params=_params(("parallel",)),
    )(g, w, m, v)


PACK_LANES = 1024


def _pack(arrays):
    pieces = []
    for a in arrays:
        flat = a.reshape(-1).astype(F32)
        pieces.append(jnp.pad(flat, (0, (-flat.shape[0]) % PACK_LANES)))
    flat = jnp.concatenate(pieces)
    return jnp.pad(flat, (0, (-flat.shape[0]) % (8 * PACK_LANES))).reshape(-1, PACK_LANES)


def _unpack(buf, shapes):
    flat = buf.reshape(-1)
    out, off = [], 0
    for s in shapes:
        n = math.prod(s)
        out.append(flat[off:off + n].reshape(s))
        off += n + ((-n) % PACK_LANES)
    return out


WEIGHT_NAMES = ['a_norm', 'a_w_in', 'a_lambda_re', 'a_lambda_im', 'a_log_step', 'a_b_re', 'a_b_im', 'a_c_re',
                'a_c_im', 'a_d', 'a_w_glu', 'kv_norm', 'w_k', 'w_v', 'w_f', 'b_f', 'b_norm', 'b_w_q', 'b_w_o',
                'ffn_norm', 'ffn_w_up', 'ffn_conv_w', 'ffn_conv_b', 'ffn_w_down', 'final_norm']
BIG = {'a_w_in': 0, 'a_w_glu': 1, 'w_k': 0, 'w_v': 0, 'b_w_q': 0, 'b_w_o': 0, 'ffn_w_up': 1, 'ffn_w_down': 0}
SMALL_SHARDED = {'a_norm': 1, 'w_f': 0, 'ffn_conv_w': 2}


def _ffn_fwd(x, gain, w_up, conv_w, conv_b, w_down, tag):
    h = norm_fwd(x, gain, f"norm_fwd_{tag}")
    up = matmul(h, w_up, out_dtype=BF16, name=f"mm_up_{tag}")
    a = convglu_fwd(up, conv_w, conv_b, f"convglu_fwd_{tag}")
    x_out = matmul(a, w_down, add=x, name=f"mm_down_{tag}")
    return x_out, (x, h, up, a)


def _ffn_bwd(dx, dxb, saved, gain, w_up, conv_w, conv_b, w_down, tag):
    x, h, up, a = saved
    dw_down = matmul(a, dxb, ta=True, out_dtype=BF16, name=f"mm_dwdown_{tag}")
    da = matmul(dxb, w_down, tb=True, name=f"mm_da_{tag}")
    dup, dconv_w, dconv_b = convglu_bwd(up, da, conv_w, conv_b, f"convglu_bwd_{tag}")
    dw_up = matmul(h, dup, ta=True, out_dtype=BF16, name=f"mm_dwup_{tag}")
    dh = matmul(dup, w_up, tb=True, name=f"mm_dh_{tag}")
    dx_in, dxb_in, dgain = norm_bwd(x, gain, dh, dx, f"norm_bwd_{tag}")
    return dx_in, dxb_in, dict(norm=dgain, w_up=dw_up, conv_w=dconv_w, conv_b=dconv_b, w_down=dw_down)


def kernel(x, a_norm, a_w_in, a_lambda_re, a_lambda_im, a_log_step, a_b_re, a_b_im, a_c_re, a_c_im, a_d, a_w_glu, kv_norm, w_k, w_v, w_f, b_f, b_norm, b_w_q, b_w_o, ffn_norm, ffn_w_up, ffn_conv_w, ffn_conv_b, ffn_w_down, final_norm, loss_target, m_a_norm, m_a_w_in, m_a_lambda_re, m_a_lambda_im, m_a_log_step, m_a_b_re, m_a_b_im, m_a_c_re, m_a_c_im, m_a_d, m_a_w_glu, m_kv_norm, m_w_k, m_w_v, m_w_f, m_b_f, m_b_norm, m_b_w_q, m_b_w_o, m_ffn_norm, m_ffn_w_up, m_ffn_conv_w, m_ffn_conv_b, m_ffn_w_down, m_final_norm, v_a_norm, v_a_w_in, v_a_lambda_re, v_a_lambda_im, v_a_log_step, v_a_b_re, v_a_b_im, v_a_c_re, v_a_c_im, v_a_d, v_a_w_glu, v_kv_norm, v_w_k, v_w_v, v_w_f, v_b_f, v_b_norm, v_b_w_q, v_b_w_o, v_ffn_norm, v_ffn_w_up, v_ffn_conv_w, v_ffn_conv_b, v_ffn_w_down, v_final_norm):
    args = locals()
    w = {n: args[n] for n in WEIGHT_NAMES}
    mom = {n: args["m_" + n] for n in WEIGHT_NAMES}
    var = {n: args["v_" + n] for n in WEIGHT_NAMES}
    x0 = x[0]
    target = loss_target[0]
    n_rows, d = x0.shape
    n_heads = d // HEAD_DIM
    n_a = a_w_in.shape[0]
    n_b = b_w_q.shape[0]
    me = _my_index()

    def gathered(name, layer=None):
        ws = w[name] if layer is None else w[name][layer]
        tag = name if layer is None else f"{name}{layer}"
        return all_gather(ws.astype(BF16), BIG[name], f"ag_{tag}")

    small_local = [w[n] for n in SMALL_SHARDED]
    small_full = all_gather(_pack(small_local)[None], 0, "ag_small")
    per_dev = [_unpack(small_full[s], [a.shape for a in small_local]) for s in range(N_DEV)]
    full_small = {n: jnp.concatenate([per_dev[s][i] for s in range(N_DEV)], axis=SMALL_SHARDED[n])
                  for i, n in enumerate(SMALL_SHARDED)}
    a_norm_f, w_f_f, conv_w_f = full_small['a_norm'], full_small['w_f'], full_small['ffn_conv_w']
    w_f_pad = jnp.pad(w_f_f, ((0, 0), (0, 128 - n_heads))).astype(BF16)
    b_f_pad = jnp.pad(b_f, (0, 128 - n_heads)).reshape(1, 128)

    grads = {}
    big_grads = {}

    xs = x0
    saved_a, saved_b, saved_ffn = [], [], []
    weights_a, weights_b, weights_ffn = [], [], []
    for i in range(n_a):
        w_in = gathered('a_w_in', i)
        w_glu = gathered('a_w_glu', i)
        prm = s5_prepare(a_lambda_re[i], a_lambda_im[i], a_log_step[i], a_b_re[i], a_b_im[i],
                         a_c_re[i], a_c_im[i], a_d[i])
        gain = a_norm_f[i].reshape(1, d)
        h = norm_fwd(xs, gain, f"norm_fwd_a{i}")
        u = s5_to_groups(matmul(h, w_in, out_dtype=BF16, name=f"mm_win_{i}"))
        yg = s5_from_groups(s5_fwd(u, prm, f"s5_fwd_{i}"))
        z = matmul(yg, w_glu, name=f"mm_wglu_{i}")
        x_mid = glu_fwd(z, xs, f"glu_fwd_{i}")
        saved_a.append((xs, h, u, yg, z))
        weights_a.append((w_in, w_glu, prm, gain))
        fw = (ffn_norm[i].reshape(1, d), gathered('ffn_w_up', i), conv_w_f[i], ffn_conv_b[i].reshape(1, -1),
              gathered('ffn_w_down', i))
        xs, sv = _ffn_fwd(x_mid, *fw, f"f{i}")
        saved_ffn.append(sv)
        weights_ffn.append(fw)

    x_kv = xs
    kv_gain = kv_norm.reshape(1, d)
    wk_f, wv_f = gathered('w_k'), gathered('w_v')
    h_kv = norm_fwd(x_kv, kv_gain, "norm_fwd_kv")
    k = matmul(h_kv, wk_f, out_dtype=BF16, name="mm_wk")
    v = matmul(h_kv, wv_f, out_dtype=BF16, name="mm_wv")
    zf = matmul(h_kv, w_f_pad, name="mm_wf")
    ct = fgate_fwd(zf, b_f_pad, n_heads, "fgate_fwd")
    c_rows, c_cols = attn_layouts(ct)

    for j in range(n_b):
        li = n_a + j
        w_q, w_o = gathered('b_w_q', j), gathered('b_w_o', j)
        gain = b_norm[j].reshape(1, d)
        h = norm_fwd(xs, gain, f"norm_fwd_b{j}")
        q = matmul(h, w_q, out_dtype=BF16, name=f"mm_wq_{j}")
        o, o32, lse = flash_fwd(q, k, v, c_rows, f"flash_fwd_{j}")
        x_mid = matmul(o, w_o, add=xs, name=f"mm_wo_{j}")
        saved_b.append((xs, h, q, o, o32, lse))
        weights_b.append((w_q, w_o, gain))
        fw = (ffn_norm[li].reshape(1, d), gathered('ffn_w_up', li), conv_w_f[li], ffn_conv_b[li].reshape(1, -1),
              gathered('ffn_w_down', li))
        xs, sv = _ffn_fwd(x_mid, *fw, f"f{li}")
        saved_ffn.append(sv)
        weights_ffn.append(fw)

    dx, dxb, d_final, sq = loss_and_grad(xs, final_norm.reshape(1, d), target, "loss")
    loss = lax.psum(0.5 * jnp.sum(sq) / d, MESH_AXES)
    grads['final_norm'] = d_final.reshape(d)

    ffn_g = [None] * (n_a + n_b)
    b_g = [None] * n_b
    dk_tot = dv_tot = dc_tot = None
    for j in reversed(range(n_b)):
        li = n_a + j
        dx, dxb, ffn_g[li] = _ffn_bwd(dx, dxb, saved_ffn[li], *weights_ffn[li], f"f{li}")
        xs_in, h, q, o, o32, lse = saved_b[j]
        w_q, w_o, gain = weights_b[j]
        dw_o = matmul(o, dxb, ta=True, out_dtype=BF16, name=f"mm_dwo_{j}")
        do = matmul(dxb, w_o, tb=True, out_dtype=BF16, name=f"mm_do_{j}")
        delta = attn_delta(do, o32, f"attn_delta_{j}")
        dq = flash_bwd_dq(q, k, v, do, c_rows, lse, delta, f"flash_dq_{j}")
        lse_rows = attn_layouts(lse.reshape(n_heads, n_rows))[0]
        delta_rows = attn_layouts(delta.reshape(n_heads, n_rows))[0]
        dk_j, dv_j, dc_j = flash_bwd_dkv(q, k, v, do, c_rows, c_cols, lse_rows, delta_rows, f"flash_dkv_{j}")
        dc_j = dc_j.reshape(n_heads, n_rows)
        if dk_tot is None:
            dk_tot, dv_tot, dc_tot = [dk_j], [dv_j], [dc_j]
        else:
            dk_tot.append(dk_j), dv_tot.append(dv_j), dc_tot.append(dc_j)
        dw_q = matmul(h, dq, ta=True, out_dtype=BF16, name=f"mm_dwq_{j}")
        dh = matmul(dq, w_q, tb=True, name=f"mm_dhq_{j}")
        dx, dxb, dgain = norm_bwd(xs_in, gain, dh, dx, f"norm_bwd_b{j}")
        b_g[j] = dict(norm=dgain, w_q=dw_q, w_o=dw_o)

    def total(parts, tag):
        if len(parts) == 1:
            return parts[0].astype(BF16)
        acc = parts[0]
        for n_, p_ in enumerate(parts[1:-1]):
            acc = acc + p_
        return add_cast(acc, parts[-1], f"add_{tag}")

    dk_b, dv_b = total(dk_tot, "dk"), total(dv_tot, "dv")
    dc_a = dc_tot[0]
    dc_b = dc_tot[1] if len(dc_tot) > 1 else jnp.zeros_like(dc_a)
    for extra in dc_tot[2:]:
        dc_b = dc_b + extra
    dzf, db_f = fgate_bwd(dc_a, dc_b, zf, b_f_pad, "fgate_bwd")
    big_grads[('w_k', None)] = matmul(h_kv, dk_b, ta=True, out_dtype=BF16, name="mm_dwk")
    big_grads[('w_v', None)] = matmul(h_kv, dv_b, ta=True, out_dtype=BF16, name="mm_dwv")
    dw_f = matmul(h_kv, dzf, ta=True, name="mm_dwf")[:, :n_heads]
    dh_kv = matmul(dk_b, wk_f, tb=True, name="mm_dhk")
    dh_kv = matmul(dv_b, wv_f, tb=True, add=dh_kv, name="mm_dhv")
    dh_kv = matmul(dzf, w_f_pad, tb=True, add=dh_kv, name="mm_dhf")
    dx, dxb, d_kv_gain = norm_bwd(x_kv, kv_gain, dh_kv, dx, "norm_bwd_kv")
    grads['kv_norm'] = d_kv_gain.reshape(d)
    grads['b_f'] = db_f[0, :n_heads]

    a_g = [None] * n_a
    for i in reversed(range(n_a)):
        dx, dxb, ffn_g[i] = _ffn_bwd(dx, dxb, saved_ffn[i], *weights_ffn[i], f"f{i}")
        xs_in, h, u, yg, z = saved_a[i]
        w_in, w_glu, prm, gain = weights_a[i]
        dz = glu_bwd(z, dx, f"glu_bwd_{i}")
        dw_glu = matmul(yg, dz, ta=True, out_dtype=BF16, name=f"mm_dwglu_{i}")
        dyg = s5_to_groups(matmul(dz, w_glu, tb=True, out_dtype=BF16, name=f"mm_dyg_{i}"))
        du_g, dprm = s5_bwd(u, dyg, prm, f"s5_bwd_{i}")
        du = s5_from_groups(du_g)
        dw_in = matmul(h, du, ta=True, out_dtype=BF16, name=f"mm_dwin_{i}")
        dh = matmul(du, w_in, tb=True, name=f"mm_dhin_{i}")
        dx, dxb, dgain = norm_bwd(xs_in, gain, dh, dx, f"norm_bwd_a{i}")
        a_g[i] = dict(norm=dgain, w_in=dw_in, w_glu=dw_glu, prm=s5_param_grads(
            (a_lambda_re[i], a_lambda_im[i], a_log_step[i], a_b_re[i], a_b_im[i], a_c_re[i], a_c_im[i],
             a_d[i]), dprm))
    grad_x = dx[None]

    for i in range(n_a):
        big_grads[('a_w_in', i)] = a_g[i]['w_in']
        big_grads[('a_w_glu', i)] = a_g[i]['w_glu']
    for j in range(n_b):
        big_grads[('b_w_q', j)] = b_g[j]['w_q']
        big_grads[('b_w_o', j)] = b_g[j]['w_o']
    for li in range(n_a + n_b):
        big_grads[('ffn_w_up', li)] = ffn_g[li]['w_up']
        big_grads[('ffn_w_down', li)] = ffn_g[li]['w_down']
    stack = lambda parts: jnp.stack(parts, axis=0)
    grads['a_norm'] = stack([a_g[i]['norm'].reshape(d) for i in range(n_a)])
    for pi, pname in enumerate(['a_lambda_re', 'a_lambda_im', 'a_log_step', 'a_b_re', 'a_b_im', 'a_c_re',
                                'a_c_im', 'a_d']):
        grads[pname] = stack([a_g[i]['prm'][pi] for i in range(n_a)])
    grads['w_f'] = dw_f
    grads['b_norm'] = stack([b_g[j]['norm'].reshape(d) for j in range(n_b)])
    grads['ffn_norm'] = stack([g_['norm'].reshape(d) for g_ in ffn_g])
    grads['ffn_conv_w'] = stack([g_['conv_w'] for g_ in ffn_g])
    grads['ffn_conv_b'] = stack([g_['conv_b'].reshape(-1) for g_ in ffn_g])

    small_names = [n for n in WEIGHT_NAMES if n not in BIG]
    small_shapes = [grads[n].shape for n in small_names]
    g_parts = all_gather(_pack([grads[n] for n in small_names])[None], 0, "ag_grads_small")
    g_sum = _unpack(sum8(g_parts, "sum_grads_small"), small_shapes)
    g_full = dict(zip(small_names, g_sum))
    g_local = {}
    for n in small_names:
        if n in SMALL_SHARDED:
            ax = SMALL_SHARDED[n]
            size = w[n].shape[ax]
            g_local[n] = lax.dynamic_slice_in_dim(g_full[n], me * size, size, axis=ax)
        else:
            g_local[n] = g_full[n]
    local_shapes = [w[n].shape for n in small_names]
    d_s, m_s, v_s = adamw_plain(_pack([g_local[n] for n in small_names]), _pack([w[n] for n in small_names]),
                                _pack([mom[n] for n in small_names]), _pack([var[n] for n in small_names]),
                                "adamw_small")
    out_g, out_d, out_m, out_v = dict(g_local), {}, {}, {}
    for n, dd, mm, vv in zip(small_names, _unpack(d_s, local_shapes), _unpack(m_s, local_shapes),
                             _unpack(v_s, local_shapes)):
        out_d[n], out_m[n], out_v[n] = dd, mm, vv

    for name, ax in BIG.items():
        layered = w[name].ndim == 3
        layers = range(w[name].shape[0]) if layered else [None]
        res = []
        for layer in layers:
            tag = name if layer is None else f"{name}{layer}"
            g_mat = big_grads[(name, layer)]
            theirs = sibling_exchange(g_mat, ax, f"rs_d2d_{tag}")
            parts = owner_exchange(chip_sum(own_blocks(g_mat, ax), theirs, f"rs_add_{tag}"), f"rs_ici_{tag}")
            pick = (lambda a: a) if layer is None else (lambda a: a[layer])
            res.append(adamw_sum(parts, pick(w[name]), pick(mom[name]), pick(var[name]), f"adamw_{tag}"))
        for dst, idx in ((out_g, 0), (out_d, 1), (out_m, 2), (out_v, 3)):
            dst[name] = stack([r[idx] for r in res]) if layered else res[0][idx]

    return (loss, grad_x, *[out_g[n] for n in WEIGHT_NAMES], *[out_d[n] for n in WEIGHT_NAMES],
            *[out_m[n] for n in WEIGHT_NAMES], *[out_v[n] for n in WEIGHT_NAMES])
```
